```python
import math
import jax, jax.numpy as jnp
from jax import lax
import numpy as np

D_MODEL = 1024
BATCH = 8
SEQ = 2048
DEPTH = 2

GRID_W = 64
CTX_LEN = 256
BLOCK = 128
WINDOW = 128
ROPE_BASE = 10000.0
EPS = 1e-6
HQ_A = 16
HKV_A = 2
HD_A = 64
GROUP_A = HQ_A // HKV_A
HQ_B = 8
HKV_B = 2
HD_B = 128
GROUP_B = HQ_B // HKV_B
N_EXPERTS = 32
TOP_K = 4
D_FF = D_MODEL
SWIGLU_LIMIT = 7.0
SWIGLU_ALPHA = 1.702
N_LAYERS_A = (DEPTH + 1) // 2
N_LAYERS_B = DEPTH // 2

kernel_name = "hybrid_dit_window_sink_gqa_qknorm_moe"


def rmsnorm(x, g):
    xf = x.astype(jnp.float32)
    y = xf * lax.rsqrt(jnp.mean(xf * xf, axis=-1, keepdims=True) + EPS)
    return (y * g.astype(jnp.float32)).astype(x.dtype)


def modulate(h, shift, scale):
    return h * (1 + scale) + shift


def axial_rope_tables(rows, head_dim):
    quarter = head_dim // 4
    inv_freq = jnp.float32(ROPE_BASE) ** (-jnp.arange(quarter, dtype=jnp.float32) / quarter)
    row_idx = jnp.repeat(jnp.arange(rows, dtype=jnp.float32), GRID_W)
    col_idx = jnp.tile(jnp.arange(GRID_W, dtype=jnp.float32), rows)
    ang_r = row_idx[:, None] * inv_freq[None, :]
    ang_c = col_idx[:, None] * inv_freq[None, :]
    return (jnp.cos(ang_r), jnp.sin(ang_r), jnp.cos(ang_c), jnp.sin(ang_c))


def _rotate(xp, cos, sin):
    x1, x2 = jnp.split(xp, 2, axis=-1)
    return jnp.concatenate([x1 * cos - x2 * sin, x2 * cos + x1 * sin], axis=-1)


def apply_axial_rope(x, tabs):
    cr, sr, cc, sc = tabs
    shape = (x.shape[1],) + (1,) * (x.ndim - 3) + (-1,)
    cast = lambda t: t.reshape(shape).astype(x.dtype)
    xr, xc = jnp.split(x, 2, axis=-1)
    return jnp.concatenate([_rotate(xr, cast(cr), cast(sr)), _rotate(xc, cast(cc), cast(sc))], axis=-1)


def split_qkv(h, w_qkv, hq, hkv, hd):
    b, n, _ = h.shape
    qkv = h @ w_qkv
    q, k, v = jnp.split(qkv, [hq * hd, (hq + hkv) * hd], axis=-1)
    return (q.reshape(b, n, hkv, hq // hkv, hd), k.reshape(b, n, hkv, hd), v.reshape(b, n, hkv, hd))


def ctx_self_attention(q, k, v, sink):
    scale = q.shape[-1] ** -0.5
    s = jnp.einsum('bqgrd,bkgd->bgrqk', q, k).astype(jnp.float32) * scale
    if sink is not None:
        sb = jnp.broadcast_to(sink.astype(jnp.float32)[None, :, :, None, None], s.shape[:-1] + (1,))
        p = jax.nn.softmax(jnp.concatenate([s, sb], axis=-1), axis=-1)[..., :-1]
    else:
        p = jax.nn.softmax(s, axis=-1)
    o = jnp.einsum('bgrqk,bkgd->bqgrd', p.astype(v.dtype), v)
    return o.reshape(o.shape[0], o.shape[1], -1)


def mixer_window_sink(h_lat, h_ctx, w_qkv, w_o, sink, rope_tabs, need_ctx):
    b, n, _ = h_lat.shape
    nb = n // BLOCK
    scale = HD_A ** -0.5
    q, k, v = split_qkv(h_lat, w_qkv, HQ_A, HKV_A, HD_A)
    qc, kc, vc = split_qkv(h_ctx, w_qkv, HQ_A, HKV_A, HD_A)
    q = apply_axial_rope(q, rope_tabs)
    k = apply_axial_rope(k, rope_tabs)
    sink_gr = sink.reshape(HKV_A, GROUP_A)

    qb = q.reshape(b, nb, BLOCK, HKV_A, GROUP_A, HD_A)
    pad = ((0, 0), (BLOCK, BLOCK), (0, 0), (0, 0))
    kp = jnp.pad(k, pad).reshape(b, nb + 2, BLOCK, HKV_A, HD_A)
    vp = jnp.pad(v, pad).reshape(b, nb + 2, BLOCK, HKV_A, HD_A)
    kw = jnp.concatenate([kp[:, :-2], kp[:, 1:-1], kp[:, 2:]], axis=2)
    vw = jnp.concatenate([vp[:, :-2], vp[:, 1:-1], vp[:, 2:]], axis=2)

    qpos = jnp.arange(n).reshape(nb, BLOCK)
    kpos = jnp.arange(nb)[:, None] * BLOCK - BLOCK + jnp.arange(3 * BLOCK)[None, :]
    mask = (jnp.abs(qpos[:, :, None] - kpos[:, None, :]) <= WINDOW) & (kpos[:, None, :] >= 0) & (kpos[:, None, :] < n)

    s_win = jnp.einsum('bnqgrd,bnkgd->bngrqk', qb, kw).astype(jnp.float32) * scale
    s_win = jnp.where(mask[None, :, None, None, :, :], s_win, jnp.float32(-1e30))
    s_ctx = jnp.einsum('bnqgrd,bcgd->bngrqc', qb, kc).astype(jnp.float32) * scale
    s_sink = jnp.broadcast_to(sink_gr.astype(jnp.float32)[None, None, :, :, None, None], s_win.shape[:-1] + (1,))
    p = jax.nn.softmax(jnp.concatenate([s_win, s_ctx, s_sink], axis=-1), axis=-1)
    p_win = p[..., :3 * BLOCK].astype(v.dtype)
    p_ctx = p[..., 3 * BLOCK:3 * BLOCK + kc.shape[1]].astype(v.dtype)
    o = jnp.einsum('bngrqk,bnkgd->bnqgrd', p_win, vw) + jnp.einsum('bngrqc,bcgd->bnqgrd', p_ctx, vc)
    o_lat = o.reshape(b, n, HQ_A * HD_A) @ w_o
    o_ctx = ctx_self_attention(qc, kc, vc, sink_gr) @ w_o if need_ctx else None
    return o_lat, o_ctx


def mixer_qknorm_gqa(h_lat, h_ctx, w_qkv, q_norm, k_norm, w_o, rope_tabs, need_ctx):
    b, n, _ = h_lat.shape
    nb = n // BLOCK
    scale = HD_B ** -0.5
    q, k, v = split_qkv(h_lat, w_qkv, HQ_B, HKV_B, HD_B)
    qc, kc, vc = split_qkv(h_ctx, w_qkv, HQ_B, HKV_B, HD_B)
    q, qc = rmsnorm(q, q_norm), rmsnorm(qc, q_norm)
    k, kc = rmsnorm(k, k_norm), rmsnorm(kc, k_norm)
    q = apply_axial_rope(q, rope_tabs)
    k = apply_axial_rope(k, rope_tabs)
    k_all = jnp.concatenate([k, kc], axis=1)
    v_all = jnp.concatenate([v, vc], axis=1)

    qb = q.reshape(b, nb, BLOCK, HKV_B, GROUP_B, HD_B).transpose(1, 0, 2, 3, 4, 5)

    def attend_block(q_blk):
        s = jnp.einsum('bqgrd,bkgd->bgrqk', q_blk, k_all).astype(jnp.float32) * scale
        p = jax.nn.softmax(s, axis=-1).astype(v_all.dtype)
        return jnp.einsum('bgrqk,bkgd->bqgrd', p, v_all)

    o = lax.map(attend_block, qb).transpose(1, 0, 2, 3, 4, 5)
    o_lat = o.reshape(b, n, HQ_B * HD_B) @ w_o
    o_ctx = ctx_self_attention(qc, kc, vc, None) @ w_o if need_ctx else None
    return o_lat, o_ctx


def moe_ffn(h, w_router, b_router, w1, b1, w2, b2):
    logits = (h @ w_router + b_router).astype(jnp.float32)
    top_v, top_i = lax.top_k(logits, TOP_K)
    top_w = jax.nn.softmax(top_v, axis=-1)
    gates = jnp.sum(jax.nn.one_hot(top_i, N_EXPERTS, dtype=jnp.float32) * top_w[..., None], axis=-2)
    gates = gates.astype(h.dtype)
    out = jnp.zeros_like(h)
    for e in range(N_EXPERTS):
        u = h @ w1[e] + b1[e]
        glu, lin = u[:, :D_FF], u[:, D_FF:]
        glu = jnp.minimum(glu, SWIGLU_LIMIT)
        lin = jnp.clip(lin, -SWIGLU_LIMIT, SWIGLU_LIMIT)
        act = glu * jax.nn.sigmoid(SWIGLU_ALPHA * glu) * (lin + 1)
        out = out + gates[:, e:e + 1] * (act @ w2[e] + b2[e])
    return out


def setup_inputs(seed: int = 0) -> dict:
    key = jax.random.key(seed)
    ks = jax.random.split(key, 20)
    D = D_MODEL
    nrm = jax.random.normal
    f32 = jnp.float32
    return {
        "x": nrm(ks[0], (BATCH, SEQ, D), f32),
        "c": nrm(ks[1], (BATCH, D), f32),
        "ctx": nrm(ks[2], (BATCH, CTX_LEN, D), f32),
        "c_ctx": nrm(ks[3], (D,), f32),
        "w_ada": nrm(ks[4], (DEPTH, D, 6 * D), f32) * (0.5 * D ** -0.5),
        "b_ada": nrm(ks[5], (DEPTH, 6 * D), f32) * 0.02,
        "norm_g": 1.0 + 0.1 * nrm(ks[6], (DEPTH, 4, D), f32),
        "a_w_qkv": nrm(ks[7], (N_LAYERS_A, D, (HQ_A + 2 * HKV_A) * HD_A), f32) * D ** -0.5,
        "a_w_o": nrm(ks[8], (N_LAYERS_A, HQ_A * HD_A, D), f32) * (HQ_A * HD_A) ** -0.5,
        "a_sink": nrm(ks[9], (N_LAYERS_A, HQ_A), f32) * 0.5,
        "b_w_qkv": nrm(ks[10], (N_LAYERS_B, D, (HQ_B + 2 * HKV_B) * HD_B), f32) * D ** -0.5,
        "b_q_norm": 1.0 + 0.1 * nrm(ks[11], (N_LAYERS_B, HD_B), f32),
        "b_k_norm": 1.0 + 0.1 * nrm(ks[12], (N_LAYERS_B, HD_B), f32),
        "b_w_o": nrm(ks[13], (N_LAYERS_B, HQ_B * HD_B, D), f32) * (HQ_B * HD_B) ** -0.5,
        "moe_w_router": nrm(ks[14], (DEPTH, D, N_EXPERTS), f32) * D ** -0.5,
        "moe_b_router": nrm(ks[15], (DEPTH, N_EXPERTS), f32) * 0.01,
        "moe_w1": nrm(ks[16], (DEPTH, N_EXPERTS, D, 2 * D_FF), f32) * D ** -0.5,
        "moe_b1": nrm(ks[17], (DEPTH, N_EXPERTS, 2 * D_FF), f32) * 0.01,
        "moe_w2": nrm(ks[18], (DEPTH, N_EXPERTS, D_FF, D), f32) * D_FF ** -0.5,
        "moe_b2": nrm(ks[19], (DEPTH, N_EXPERTS, D), f32) * 0.01,
    }


def reference(x, c, ctx, c_ctx, w_ada, b_ada, norm_g, a_w_qkv, a_w_o, a_sink,
              b_w_qkv, b_q_norm, b_k_norm, b_w_o,
              moe_w_router, moe_b_router, moe_w1, moe_b1, moe_w2, moe_b2):
    b, n, d = x.shape
    rows = n // GRID_W
    tabs_a = axial_rope_tables(rows, HD_A)
    tabs_b = axial_rope_tables(rows, HD_B)
    silu_c = jax.nn.silu(c)
    silu_cc = jax.nn.silu(c_ctx)
    for i in range(DEPTH):
        last = i == DEPTH - 1
        mod_lat = silu_c @ w_ada[i] + b_ada[i]
        mod_ctx = silu_cc @ w_ada[i] + b_ada[i]
        sh1, sc1, g1, sh2, sc2, g2 = [m[:, None, :] for m in jnp.split(mod_lat, 6, axis=-1)]
        csh1, csc1, cg1, csh2, csc2, cg2 = jnp.split(mod_ctx, 6, axis=-1)

        h_lat = modulate(rmsnorm(x, norm_g[i, 0]), sh1, sc1)
        h_ctx = modulate(rmsnorm(ctx, norm_g[i, 0]), csh1, csc1)
        if i % 2 == 0:
            la = i // 2
            a_lat, a_ctx = mixer_window_sink(h_lat, h_ctx, a_w_qkv[la], a_w_o[la], a_sink[la], tabs_a, not last)
        else:
            lb = i // 2
            a_lat, a_ctx = mixer_qknorm_gqa(h_lat, h_ctx, b_w_qkv[lb], b_q_norm[lb], b_k_norm[lb], b_w_o[lb], tabs_b, not last)
        x = x + g1 * rmsnorm(a_lat, norm_g[i, 1])
        if not last:
            ctx = ctx + cg1 * rmsnorm(a_ctx, norm_g[i, 1])

        f_lat_in = modulate(rmsnorm(x, norm_g[i, 2]), sh2, sc2).reshape(-1, d)
        if not last:
            f_ctx_in = modulate(rmsnorm(ctx, norm_g[i, 2]), csh2, csc2).reshape(-1, d)
            f_in = jnp.concatenate([f_lat_in, f_ctx_in], axis=0)
        else:
            f_in = f_lat_in
        f_out = moe_ffn(f_in, moe_w_router[i], moe_b_router[i], moe_w1[i], moe_b1[i], moe_w2[i], moe_b2[i])
        f_lat = f_out[:b * n].reshape(b, n, d)
        x = x + g2 * rmsnorm(f_lat, norm_g[i, 3])
        if not last:
            f_ctx = f_out[b * n:].reshape(b, ctx.shape[1], d)
            ctx = ctx + cg2 * rmsnorm(f_ctx, norm_g[i, 3])
    return x
```

```python
import functools

import jax
import jax.numpy as jnp
from jax import lax
from jax.experimental import pallas as pl
from jax.experimental.pallas import tpu as pltpu

D_MODEL = 1024
BATCH = 8
SEQ = 2048
DEPTH = 2
GRID_W = 64
CTX_LEN = 256
BLOCK = 128
WINDOW = 128
ROPE_BASE = 10000.0
EPS = 1e-6
HQ_A, HKV_A, HD_A = 16, 2, 64
HQ_B, HKV_B, HD_B = 8, 2, 128
N_EXPERTS = 32
TOP_K = 4
D_FF = D_MODEL
SWIGLU_LIMIT = 7.0
SWIGLU_ALPHA = 1.702

T_LAT = BATCH * SEQ
T_CTX = BATCH * CTX_LEN
T_ALL = T_LAT + T_CTX
LANES = 128
TM = 256
LAT_TILES = T_LAT // TM
ALL_TILES = T_ALL // TM
TILES_PER_SEQ = SEQ // TM
MOD_ROWS = 16
CTX_MOD_ROW = BATCH
TMM = 1024
NEG = -1e30
VMEM_LIMIT = 56 * 1024 * 1024

f32 = jnp.float32
bf16 = jnp.bfloat16


def _seg_of_tile(i, tiles_per_seq, lat_tiles):
    return jnp.where(i < lat_tiles, i // tiles_per_seq, CTX_MOD_ROW)


def _adaln_kernel(c_ref, w_ref, b_ref, o_ref):
    c = c_ref[...]
    s = c * jax.nn.sigmoid(c)
    o_ref[...] = jnp.dot(s, w_ref[...], precision=lax.Precision.HIGHEST,
                         preferred_element_type=f32) + b_ref[...]


def _adaln(c_all, w_ada, b_ada):
    tn = 1536
    return pl.pallas_call(
        _adaln_kernel,
        grid=(DEPTH, 6 * D_MODEL // tn),
        in_specs=[
            pl.BlockSpec((MOD_ROWS, D_MODEL), lambda l, j: (0, 0)),
            pl.BlockSpec((None, D_MODEL, tn), lambda l, j: (l, 0, j)),
            pl.BlockSpec((None, 1, tn), lambda l, j: (l, 0, j)),
        ],
        out_specs=pl.BlockSpec((None, MOD_ROWS, tn), lambda l, j: (l, 0, j)),
        out_shape=jax.ShapeDtypeStruct((DEPTH, MOD_ROWS, 6 * D_MODEL), f32),
        compiler_params=pltpu.CompilerParams(
            dimension_semantics=("arbitrary", "arbitrary"), vmem_limit_bytes=VMEM_LIMIT),
        name="adaln",
    )(c_all, w_ada, b_ada.reshape(DEPTH, 1, 6 * D_MODEL))


def _rms(x, g):
    return x * lax.rsqrt(jnp.mean(x * x, axis=-1, keepdims=True) + EPS) * g


def _qkv_kernel(x_ref, mod_ref, g_ref, w_ref, cos_ref, sa_ref, sb_ref, qn_ref, kn_ref,
                q_ref, k_ref, v_ref, *, nq, nk, hd, qk_norm):
    h = _rms(x_ref[...], g_ref[...]) * (1.0 + mod_ref[1:2, :]) + mod_ref[0:1, :]
    qkv = jnp.dot(h.astype(bf16), w_ref[...], preferred_element_type=f32)
    cos, sa, sb = cos_ref[...], sa_ref[...], sb_ref[...]
    quarter = hd // 4
    scale = hd ** -0.5

    def rope(c):
        return c * cos + pltpu.roll(c, quarter, 1) * sa + pltpu.roll(c, LANES - quarter, 1) * sb

    for j in range(nq // LANES):
        c = qkv[:, j * LANES:(j + 1) * LANES]
        if qk_norm:
            c = _rms(c, qn_ref[...])
        q_ref[:, j * LANES:(j + 1) * LANES] = (rope(c) * scale).astype(bf16)
    for j in range(nk // LANES):
        c = qkv[:, nq + j * LANES:nq + (j + 1) * LANES]
        if qk_norm:
            c = _rms(c, kn_ref[...])
        k_ref[:, j * LANES:(j + 1) * LANES] = rope(c).astype(bf16)
    v_ref[...] = qkv[:, nq + nk:].astype(bf16)


def _qkv(xs, mods, layer, g, w_bf16, tabs, qn, kn, *, hq, hkv, hd, qk_norm):
    nq, nk = hq * hd, hkv * hd
    seg = functools.partial(_seg_of_tile, tiles_per_seq=TILES_PER_SEQ, lat_tiles=LAT_TILES)
    tab_idx = lambda i: (jnp.where(i < LAT_TILES, i % TILES_PER_SEQ, TILES_PER_SEQ), 0)
    tab_spec = pl.BlockSpec((TM, LANES), tab_idx)
    row = lambda n: pl.BlockSpec((TM, n), lambda i: (i, 0))
    return pl.pallas_call(
        functools.partial(_qkv_kernel, nq=nq, nk=nk, hd=hd, qk_norm=qk_norm),
        grid=(ALL_TILES,),
        in_specs=[
            row(D_MODEL),
            pl.BlockSpec((None, None, 6, D_MODEL), lambda i: (layer, seg(i), 0, 0)),
            pl.BlockSpec((1, D_MODEL), lambda i: (0, 0)),
            pl.BlockSpec((D_MODEL, nq + 2 * nk), lambda i: (0, 0)),
            tab_spec, tab_spec, tab_spec,
            pl.BlockSpec((1, LANES), lambda i: (0, 0)),
            pl.BlockSpec((1, LANES), lambda i: (0, 0)),
        ],
        out_specs=[row(nq), row(nk), row(nk)],
        out_shape=[jax.ShapeDtypeStruct((T_ALL, nq), bf16),
                   jax.ShapeDtypeStruct((T_ALL, nk), bf16),
                   jax.ShapeDtypeStruct((T_ALL, nk), bf16)],
        compiler_params=pltpu.CompilerParams(
            dimension_semantics=("arbitrary",), vmem_limit_bytes=VMEM_LIMIT),
        name=f"qkv{layer}",
    )(xs, mods, g, w_bf16, *tabs, qn, kn)


def _rope_tables(hd):
    quarter = hd // 4
    inv_freq = jnp.float32(ROPE_BASE) ** (-jnp.arange(quarter, dtype=f32) / quarter)
    t = jnp.arange(SEQ)
    ang_r = (t // GRID_W).astype(f32)[:, None] * inv_freq[None, :]
    ang_c = (t % GRID_W).astype(f32)[:, None] * inv_freq[None, :]
    z = jnp.zeros_like(ang_r)
    cos = jnp.concatenate([jnp.cos(ang_r)] * 2 + [jnp.cos(ang_c)] * 2, axis=-1)
    sa = jnp.concatenate([z, jnp.sin(ang_r), z, jnp.sin(ang_c)], axis=-1)
    sb = jnp.concatenate([-jnp.sin(ang_r), z, -jnp.sin(ang_c), z], axis=-1)
    rep = LANES // hd
    pad = lambda a, v: jnp.concatenate(
        [jnp.tile(a, (1, rep)), jnp.full((TM, LANES), v, f32)], axis=0)
    return pad(cos, 1.0), pad(sa, 0.0), pad(sb, 0.0)


def _pair_operand(x, g):
    lane = lax.broadcasted_iota(jnp.int32, x.shape, 1)
    swapped = pltpu.roll(x, HD_A, 1)
    lo_src, hi_src = (x, swapped) if g == 0 else (swapped, x)
    lo = jnp.where(lane < HD_A, lo_src, 0.0)
    hi = jnp.where(lane >= HD_A, hi_src, 0.0)
    return jnp.concatenate([lo, hi], axis=0).astype(bf16)


def _attend_pairs(q_ref, o_ref, sink_ref, kcat, vcat, mask):
    nkeys = kcat.shape[0]
    lane = lax.broadcasted_iota(jnp.int32, (BLOCK, LANES), 1)
    for g in range(HKV_A):
        kp = _pair_operand(kcat, g)
        vp = _pair_operand(vcat, g)
        for p in range(HQ_A // HKV_A // 2):
            col = (g * (HQ_A // HKV_A // 2) + p) * LANES
            s = lax.dot_general(q_ref[:, col:col + LANES], kp, (((1,), (1,)), ((), ())),
                                preferred_element_type=f32)
            es, inv = [], []
            for hh in range(2):
                sh = s[:, hh * nkeys:(hh + 1) * nkeys]
                if mask is not None:
                    sh = jnp.where(mask, sh, NEG)
                sk = sink_ref[col // HD_A + hh]
                m = jnp.maximum(jnp.max(sh, axis=-1, keepdims=True), sk)
                e = jnp.exp(sh - m)
                inv.append(1.0 / (jnp.sum(e, axis=-1, keepdims=True) + jnp.exp(sk - m)))
                es.append(e.astype(bf16))
            o = jnp.dot(jnp.concatenate(es, axis=1), vp, preferred_element_type=f32)
            o_ref[:, col:col + LANES] = (o * jnp.where(lane < HD_A, inv[0], inv[1])).astype(bf16)


def _attn_a_kernel(sink_ref, q_ref, k_ref, v_ref, kc_ref, vc_ref, o_ref):
    j = pl.program_id(1)
    nblk = SEQ // BLOCK
    kc = kc_ref[...].astype(f32)
    vc = vc_ref[...].astype(f32)

    @pl.when(j < nblk)
    def _():
        wlen = 3 * BLOCK
        s0 = pl.multiple_of(jnp.clip((j - 1) * BLOCK, 0, SEQ - wlen), BLOCK)
        kcat = jnp.concatenate([k_ref[pl.ds(s0, wlen), :].astype(f32), kc], axis=0)
        vcat = jnp.concatenate([v_ref[pl.ds(s0, wlen), :].astype(f32), vc], axis=0)
        qpos = j * BLOCK + lax.broadcasted_iota(jnp.int32, (BLOCK, wlen + CTX_LEN), 0)
        col = lax.broadcasted_iota(jnp.int32, (BLOCK, wlen + CTX_LEN), 1)
        mask = (jnp.abs(qpos - (s0 + col)) <= WINDOW) | (col >= wlen)
        _attend_pairs(q_ref, o_ref, sink_ref, kcat, vcat, mask)

    @pl.when(j >= nblk)
    def _():
        _attend_pairs(q_ref, o_ref, sink_ref, kc, vc, None)


def _attn_a(q, k, v, sink):
    nblk = SEQ // BLOCK
    cblk = CTX_LEN // BLOCK
    nk = HKV_A * HD_A
    qrow = lambda b, j, s: (jnp.where(j < nblk, b * nblk + j, T_LAT // BLOCK + b * cblk + (j - nblk)), 0)
    lat_kv = pl.BlockSpec((SEQ, nk), lambda b, j, s: (b, 0))
    ctx_kv = pl.BlockSpec((CTX_LEN, nk), lambda b, j, s: (T_LAT // CTX_LEN + b, 0))
    return pl.pallas_call(
        _attn_a_kernel,
        grid_spec=pltpu.PrefetchScalarGridSpec(
            num_scalar_prefetch=1,
            grid=(BATCH, nblk + cblk),
            in_specs=[pl.BlockSpec((BLOCK, HQ_A * HD_A), qrow), lat_kv, lat_kv, ctx_kv, ctx_kv],
            out_specs=pl.BlockSpec((BLOCK, HQ_A * HD_A), qrow),
        ),
        out_shape=jax.ShapeDtypeStruct((T_ALL, HQ_A * HD_A), bf16),
        compiler_params=pltpu.CompilerParams(
            dimension_semantics=("arbitrary", "arbitrary"), vmem_limit_bytes=VMEM_LIMIT),
        name="attn_a",
    )(sink, q, k, v, k, v)


def _attn_b_kernel(q_ref, k_ref, v_ref, kc_ref, vc_ref, o_ref):
    rep = HQ_B // HKV_B
    nt = (((1,), (1,)), ((), ()))
    for g in range(HKV_B):
        gs = slice(g * HD_B, (g + 1) * HD_B)
        qs = jnp.concatenate(
            [q_ref[:, (g * rep + r) * HD_B:(g * rep + r + 1) * HD_B] for r in range(rep)], axis=0)
        s1 = lax.dot_general(qs, k_ref[:, gs], nt, preferred_element_type=f32)
        s2 = lax.dot_general(qs, kc_ref[:, gs], nt, preferred_element_type=f32)
        m = jnp.maximum(jnp.max(s1, axis=-1, keepdims=True), jnp.max(s2, axis=-1, keepdims=True))
        e1 = jnp.exp(s1 - m)
        e2 = jnp.exp(s2 - m)
        inv = 1.0 / (jnp.sum(e1, axis=-1, keepdims=True) + jnp.sum(e2, axis=-1, keepdims=True))
        o = (jnp.dot(e1.astype(bf16), v_ref[:, gs], preferred_element_type=f32)
             + jnp.dot(e2.astype(bf16), vc_ref[:, gs], preferred_element_type=f32)) * inv
        for r in range(rep):
            o_ref[:, (g * rep + r) * HD_B:(g * rep + r + 1) * HD_B] = (
                o[r * BLOCK:(r + 1) * BLOCK].astype(bf16))


def _attn_b(q, k, v):
    nblk = SEQ // BLOCK
    nk = HKV_B * HD_B
    qrow = lambda b, j: (b * nblk + j, 0)
    lat_kv = pl.BlockSpec((SEQ, nk), lambda b, j: (b, 0))
    ctx_kv = pl.BlockSpec((CTX_LEN, nk), lambda b, j: (T_LAT // CTX_LEN + b, 0))
    return pl.pallas_call(
        _attn_b_kernel,
        grid=(BATCH, nblk),
        in_specs=[pl.BlockSpec((BLOCK, HQ_B * HD_B), qrow), lat_kv, lat_kv, ctx_kv, ctx_kv],
        out_specs=pl.BlockSpec((BLOCK, HQ_B * HD_B), qrow),
        out_shape=jax.ShapeDtypeStruct((T_LAT, HQ_B * HD_B), bf16),
        compiler_params=pltpu.CompilerParams(
            dimension_semantics=("arbitrary", "arbitrary"), vmem_limit_bytes=VMEM_LIMIT),
        name="attn_b",
    )(q, k, v, k, v)


def _post_attn_kernel(o_ref, wo_ref, x_ref, mod_ref, g1_ref, g2_ref, wr_ref, br_ref,
                      xn_ref, f_ref, gate_ref):
    a = jnp.dot(o_ref[...], wo_ref[...], preferred_element_type=f32)
    x = x_ref[...] + mod_ref[2:3, :] * _rms(a, g1_ref[...])
    xn_ref[...] = x
    f = _rms(x, g2_ref[...]) * (1.0 + mod_ref[4:5, :]) + mod_ref[3:4, :]
    f_ref[...] = f.astype(bf16)
    logits = jnp.dot(f, wr_ref[...], precision=lax.Precision.HIGHEST,
                     preferred_element_type=f32) + br_ref[...]
    lane = lax.broadcasted_iota(jnp.int32, logits.shape, 1)
    rest = logits
    top_v, top_i = [], []
    for _ in range(TOP_K):
        m = jnp.max(rest, axis=-1, keepdims=True)
        idx = jnp.min(jnp.where(rest == m, lane, N_EXPERTS), axis=-1, keepdims=True)
        top_v.append(m)
        top_i.append(idx)
        rest = jnp.where(lane == idx, -jnp.inf, rest)
    es = [jnp.exp(v - top_v[0]) for v in top_v]
    inv = 1.0 / (es[0] + es[1] + es[2] + es[3])
    gates = jnp.zeros_like(logits)
    for e, idx in zip(es, top_i):
        gates = gates + jnp.where(lane == idx, e * inv, 0.0)
    gate_ref[...] = gates


def _post_attn(o, wo_bf16, xs, mods, layer, g1, g2, wr, br, n_tiles):
    seg = functools.partial(_seg_of_tile, tiles_per_seq=TILES_PER_SEQ, lat_tiles=LAT_TILES)
    rows = n_tiles * TM
    row = lambda n: pl.BlockSpec((TM, n), lambda i: (i, 0))
    const = lambda a, b: pl.BlockSpec((a, b), lambda i: (0, 0))
    return pl.pallas_call(
        _post_attn_kernel,
        grid=(n_tiles,),
        in_specs=[
            row(D_MODEL), const(D_MODEL, D_MODEL), row(D_MODEL),
            pl.BlockSpec((None, None, 6, D_MODEL), lambda i: (layer, seg(i), 0, 0)),
            const(1, D_MODEL), const(1, D_MODEL), const(D_MODEL, N_EXPERTS), const(1, N_EXPERTS),
        ],
        out_specs=[row(D_MODEL), row(D_MODEL), row(N_EXPERTS)],
        out_shape=[jax.ShapeDtypeStruct((rows, D_MODEL), f32),
                   jax.ShapeDtypeStruct((rows, D_MODEL), bf16),
                   jax.ShapeDtypeStruct((rows, N_EXPERTS), f32)],
        compiler_params=pltpu.CompilerParams(
            dimension_semantics=("arbitrary",), vmem_limit_bytes=VMEM_LIMIT),
        name=f"post_attn{layer}",
    )(o, wo_bf16, xs, mods, g1, g2, wr, br)


def _moe_dense_kernel(f_ref, gate_ref, w1_ref, b1_ref, w2_ref, b2_ref, x_ref, mod_ref, g_ref,
                      xn_ref, acc_ref):
    e = pl.program_id(1)

    @pl.when(e == 0)
    def _():
        acc_ref[...] = jnp.zeros_like(acc_ref)

    u = jnp.dot(f_ref[...], w1_ref[...], preferred_element_type=f32) + b1_ref[...]
    glu = jnp.minimum(u[:, :D_FF], SWIGLU_LIMIT)
    lin = jnp.clip(u[:, D_FF:], -SWIGLU_LIMIT, SWIGLU_LIMIT)
    act = glu * jax.nn.sigmoid(SWIGLU_ALPHA * glu) * (lin + 1.0)
    y = jnp.dot(act.astype(bf16), w2_ref[...], preferred_element_type=f32) + b2_ref[...]
    gates = gate_ref[...]
    lane = lax.broadcasted_iota(jnp.int32, gates.shape, 1)
    gcol = jnp.sum(jnp.where(lane == e, gates, 0.0), axis=-1, keepdims=True)
    acc_ref[...] += gcol * y

    @pl.when(e == N_EXPERTS - 1)
    def _():
        xn_ref[...] = x_ref[...] + mod_ref[5:6, :] * _rms(acc_ref[...], g_ref[...])


def _moe_dense(f, gates, w1, b1, w2, b2, xs, mods, layer, g, n_tiles):
    seg = functools.partial(_seg_of_tile, tiles_per_seq=SEQ // TMM, lat_tiles=T_LAT // TMM)
    rows = n_tiles * TMM
    row = lambda n: pl.BlockSpec((TMM, n), lambda i, e: (i, 0))
    return pl.pallas_call(
        _moe_dense_kernel,
        grid=(n_tiles, N_EXPERTS),
        in_specs=[
            row(D_MODEL), row(N_EXPERTS),
            pl.BlockSpec((None, D_MODEL, 2 * D_FF), lambda i, e: (e, 0, 0)),
            pl.BlockSpec((None, 1, 2 * D_FF), lambda i, e: (e, 0, 0)),
            pl.BlockSpec((None, D_FF, D_MODEL), lambda i, e: (e, 0, 0)),
            pl.BlockSpec((None, 1, D_MODEL), lambda i, e: (e, 0, 0)),
            row(D_MODEL),
            pl.BlockSpec((None, None, 6, D_MODEL), lambda i, e: (layer, seg(i), 0, 0)),
            pl.BlockSpec((1, D_MODEL), lambda i, e: (0, 0)),
        ],
        out_specs=row(D_MODEL),
        out_shape=jax.ShapeDtypeStruct((rows, D_MODEL), f32),
        scratch_shapes=[pltpu.VMEM((TMM, D_MODEL), f32)],
        compiler_params=pltpu.CompilerParams(
            dimension_semantics=("arbitrary", "arbitrary"), vmem_limit_bytes=VMEM_LIMIT),
        name=f"moe{layer}",
    )(f, gates, w1, b1.reshape(N_EXPERTS, 1, 2 * D_FF), w2, b2.reshape(N_EXPERTS, 1, D_MODEL),
      xs, mods, g)


def kernel(x, c, ctx, c_ctx, w_ada, b_ada, norm_g, a_w_qkv, a_w_o, a_sink, b_w_qkv, b_q_norm, b_k_norm,
           b_w_o, moe_w_router, moe_b_router, moe_w1, moe_b1, moe_w2, moe_b2):
    assert DEPTH == 2 and x.shape == (BATCH, SEQ, D_MODEL) and ctx.shape == (BATCH, CTX_LEN, D_MODEL)
    xs = jnp.concatenate([x.reshape(T_LAT, D_MODEL), ctx.reshape(T_CTX, D_MODEL)], axis=0)
    c_all = jnp.concatenate(
        [c, c_ctx[None, :], jnp.zeros((MOD_ROWS - BATCH - 1, D_MODEL), f32)], axis=0)
    mods = _adaln(c_all, w_ada, b_ada).reshape(DEPTH, MOD_ROWS, 6, D_MODEL)
    ones = jnp.ones((1, LANES), f32)
    g = lambda i, j: norm_g[i, j][None, :]

    q, k, v = _qkv(xs, mods, 0, g(0, 0), a_w_qkv[0].astype(bf16), _rope_tables(HD_A), ones, ones,
                   hq=HQ_A, hkv=HKV_A, hd=HD_A, qk_norm=False)
    o = _attn_a(q, k, v, a_sink[0])
    xs, f, gates = _post_attn(o, a_w_o[0].astype(bf16), xs, mods, 0, g(0, 1), g(0, 2),
                              moe_w_router[0], moe_b_router[0][None, :], ALL_TILES)
    xs = _moe_dense(f, gates, moe_w1[0].astype(bf16), moe_b1[0], moe_w2[0].astype(bf16), moe_b2[0],
                    xs, mods, 0, g(0, 3), T_ALL // TMM)

    q, k, v = _qkv(xs, mods, 1, g(1, 0), b_w_qkv[0].astype(bf16), _rope_tables(HD_B),
                   b_q_norm[0][None, :], b_k_norm[0][None, :],
                   hq=HQ_B, hkv=HKV_B, hd=HD_B, qk_norm=True)
    o = _attn_b(q, k, v)
    xl, f, gates = _post_attn(o, b_w_o[0].astype(bf16), xs, mods, 1, g(1, 1), g(1, 2),
                              moe_w_router[1], moe_b_router[1][None, :], LAT_TILES)
    xl = _moe_dense(f, gates, moe_w1[1].astype(bf16), moe_b1[1], moe_w2[1].astype(bf16), moe_b2[1],
                    xl, mods, 1, g(1, 3), T_LAT // TMM)
    return xl.reshape(BATCH, SEQ, D_MODEL)
```

```python
import functools

import jax
import jax.numpy as jnp
from jax import lax
from jax.experimental import pallas as pl
from jax.experimental.pallas import tpu as pltpu

D_MODEL = 1024
BATCH = 8
SEQ = 2048
DEPTH = 2
GRID_W = 64
CTX_LEN = 256
BLOCK = 128
WINDOW = 128
ROPE_BASE = 10000.0
EPS = 1e-6
HQ_A, HKV_A, HD_A = 16, 2, 64
HQ_B, HKV_B, HD_B = 8, 2, 128
N_EXPERTS = 32
TOP_K = 4
D_FF = D_MODEL
SWIGLU_LIMIT = 7.0
SWIGLU_ALPHA = 1.702

T_LAT = BATCH * SEQ
T_CTX = BATCH * CTX_LEN
T_ALL = T_LAT + T_CTX
LANES = 128
TM = 256
LAT_TILES = T_LAT // TM
ALL_TILES = T_ALL // TM
TILES_PER_SEQ = SEQ // TM
MOD_ROWS = 16
CTX_MOD_ROW = BATCH
TME = 256
NEG = -1e30
VMEM_LIMIT = 56 * 1024 * 1024

f32 = jnp.float32
bf16 = jnp.bfloat16


def _seg_of_tile(i, tiles_per_seq, lat_tiles):
    return jnp.where(i < lat_tiles, i // tiles_per_seq, CTX_MOD_ROW)


def _adaln_kernel(c_ref, w_ref, b_ref, o_ref):
    c = c_ref[...]
    s = c * jax.nn.sigmoid(c)
    o_ref[...] = jnp.dot(s, w_ref[...], precision=lax.Precision.HIGHEST,
                         preferred_element_type=f32) + b_ref[...]


def _adaln(c_all, w_ada, b_ada):
    tn = 1536
    return pl.pallas_call(
        _adaln_kernel,
        grid=(DEPTH, 6 * D_MODEL // tn),
        in_specs=[
            pl.BlockSpec((MOD_ROWS, D_MODEL), lambda l, j: (0, 0)),
            pl.BlockSpec((None, D_MODEL, tn), lambda l, j: (l, 0, j)),
            pl.BlockSpec((None, 1, tn), lambda l, j: (l, 0, j)),
        ],
        out_specs=pl.BlockSpec((None, MOD_ROWS, tn), lambda l, j: (l, 0, j)),
        out_shape=jax.ShapeDtypeStruct((DEPTH, MOD_ROWS, 6 * D_MODEL), f32),
        compiler_params=pltpu.CompilerParams(
            dimension_semantics=("arbitrary", "arbitrary"), vmem_limit_bytes=VMEM_LIMIT),
        name="adaln",
    )(c_all, w_ada, b_ada.reshape(DEPTH, 1, 6 * D_MODEL))


def _rms(x, g):
    return x * lax.rsqrt(jnp.mean(x * x, axis=-1, keepdims=True) + EPS) * g


def _qkv_kernel(x_ref, mod_ref, g_ref, w_ref, cos_ref, sa_ref, sb_ref, qn_ref, kn_ref,
                q_ref, k_ref, v_ref, *, nq, nk, hd, qk_norm):
    h = _rms(x_ref[...], g_ref[...]) * (1.0 + mod_ref[1:2, :]) + mod_ref[0:1, :]
    qkv = jnp.dot(h.astype(bf16), w_ref[...], preferred_element_type=f32)
    cos, sa, sb = cos_ref[...], sa_ref[...], sb_ref[...]
    quarter = hd // 4
    scale = hd ** -0.5

    def rope(c):
        return c * cos + pltpu.roll(c, quarter, 1) * sa + pltpu.roll(c, LANES - quarter, 1) * sb

    for j in range(nq // LANES):
        c = qkv[:, j * LANES:(j + 1) * LANES]
        if qk_norm:
            c = _rms(c, qn_ref[...])
        q_ref[:, j * LANES:(j + 1) * LANES] = (rope(c) * scale).astype(bf16)
    for j in range(nk // LANES):
        c = qkv[:, nq + j * LANES:nq + (j + 1) * LANES]
        if qk_norm:
            c = _rms(c, kn_ref[...])
        k_ref[:, j * LANES:(j + 1) * LANES] = rope(c).astype(bf16)
    v_ref[...] = qkv[:, nq + nk:].astype(bf16)


def _qkv(xs, mods, layer, g, w_bf16, tabs, qn, kn, *, hq, hkv, hd, qk_norm):
    nq, nk = hq * hd, hkv * hd
    seg = functools.partial(_seg_of_tile, tiles_per_seq=TILES_PER_SEQ, lat_tiles=LAT_TILES)
    tab_idx = lambda i: (jnp.where(i < LAT_TILES, i % TILES_PER_SEQ, TILES_PER_SEQ), 0)
    tab_spec = pl.BlockSpec((TM, LANES), tab_idx)
    row = lambda n: pl.BlockSpec((TM, n), lambda i: (i, 0))
    return pl.pallas_call(
        functools.partial(_qkv_kernel, nq=nq, nk=nk, hd=hd, qk_norm=qk_norm),
        grid=(ALL_TILES,),
        in_specs=[
            row(D_MODEL),
            pl.BlockSpec((None, None, 6, D_MODEL), lambda i: (layer, seg(i), 0, 0)),
            pl.BlockSpec((1, D_MODEL), lambda i: (0, 0)),
            pl.BlockSpec((D_MODEL, nq + 2 * nk), lambda i: (0, 0)),
            tab_spec, tab_spec, tab_spec,
            pl.BlockSpec((1, LANES), lambda i: (0, 0)),
            pl.BlockSpec((1, LANES), lambda i: (0, 0)),
        ],
        out_specs=[row(nq), row(nk), row(nk)],
        out_shape=[jax.ShapeDtypeStruct((T_ALL, nq), bf16),
                   jax.ShapeDtypeStruct((T_ALL, nk), bf16),
                   jax.ShapeDtypeStruct((T_ALL, nk), bf16)],
        compiler_params=pltpu.CompilerParams(
            dimension_semantics=("arbitrary",), vmem_limit_bytes=VMEM_LIMIT),
        name=f"qkv{layer}",
    )(xs, mods, g, w_bf16, *tabs, qn, kn)


def _rope_tables(hd):
    quarter = hd // 4
    inv_freq = jnp.float32(ROPE_BASE) ** (-jnp.arange(quarter, dtype=f32) / quarter)
    t = jnp.arange(SEQ)
    ang_r = (t // GRID_W).astype(f32)[:, None] * inv_freq[None, :]
    ang_c = (t % GRID_W).astype(f32)[:, None] * inv_freq[None, :]
    z = jnp.zeros_like(ang_r)
    cos = jnp.concatenate([jnp.cos(ang_r)] * 2 + [jnp.cos(ang_c)] * 2, axis=-1)
    sa = jnp.concatenate([z, jnp.sin(ang_r), z, jnp.sin(ang_c)], axis=-1)
    sb = jnp.concatenate([-jnp.sin(ang_r), z, -jnp.sin(ang_c), z], axis=-1)
    rep = LANES // hd
    pad = lambda a, v: jnp.concatenate(
        [jnp.tile(a, (1, rep)), jnp.full((TM, LANES), v, f32)], axis=0)
    return pad(cos, 1.0), pad(sa, 0.0), pad(sb, 0.0)


def _pair_operand(x, g):
    lane = lax.broadcasted_iota(jnp.int32, x.shape, 1)
    swapped = pltpu.roll(x, HD_A, 1)
    lo_src, hi_src = (x, swapped) if g == 0 else (swapped, x)
    lo = jnp.where(lane < HD_A, lo_src, 0.0)
    hi = jnp.where(lane >= HD_A, hi_src, 0.0)
    return jnp.concatenate([lo, hi], axis=0).astype(bf16)


def _attend_pairs(q_ref, o_ref, sink_ref, kcat, vcat, mask):
    nkeys = kcat.shape[0]
    lane = lax.broadcasted_iota(jnp.int32, (BLOCK, LANES), 1)
    for g in range(HKV_A):
        kp = _pair_operand(kcat, g)
        vp = _pair_operand(vcat, g)
        for p in range(HQ_A // HKV_A // 2):
            col = (g * (HQ_A // HKV_A // 2) + p) * LANES
            s = lax.dot_general(q_ref[:, col:col + LANES], kp, (((1,), (1,)), ((), ())),
                                preferred_element_type=f32)
            es, inv = [], []
            for hh in range(2):
                sh = s[:, hh * nkeys:(hh + 1) * nkeys]
                if mask is not None:
                    sh = jnp.where(mask, sh, NEG)
                sk = sink_ref[col // HD_A + hh]
                m = jnp.maximum(jnp.max(sh, axis=-1, keepdims=True), sk)
                e = jnp.exp(sh - m)
                inv.append(1.0 / (jnp.sum(e, axis=-1, keepdims=True) + jnp.exp(sk - m)))
                es.append(e.astype(bf16))
            o = jnp.dot(jnp.concatenate(es, axis=1), vp, preferred_element_type=f32)
            o_ref[:, col:col + LANES] = (o * jnp.where(lane < HD_A, inv[0], inv[1])).astype(bf16)


def _attn_a_kernel(sink_ref, q_ref, k_ref, v_ref, kc_ref, vc_ref, o_ref):
    j = pl.program_id(1)
    nblk = SEQ // BLOCK
    kc = kc_ref[...].astype(f32)
    vc = vc_ref[...].astype(f32)

    @pl.when(j < nblk)
    def _():
        wlen = 3 * BLOCK
        s0 = pl.multiple_of(jnp.clip((j - 1) * BLOCK, 0, SEQ - wlen), BLOCK)
        kcat = jnp.concatenate([k_ref[pl.ds(s0, wlen), :].astype(f32), kc], axis=0)
        vcat = jnp.concatenate([v_ref[pl.ds(s0, wlen), :].astype(f32), vc], axis=0)
        qpos = j * BLOCK + lax.broadcasted_iota(jnp.int32, (BLOCK, wlen + CTX_LEN), 0)
        col = lax.broadcasted_iota(jnp.int32, (BLOCK, wlen + CTX_LEN), 1)
        mask = (jnp.abs(qpos - (s0 + col)) <= WINDOW) | (col >= wlen)
        _attend_pairs(q_ref, o_ref, sink_ref, kcat, vcat, mask)

    @pl.when(j >= nblk)
    def _():
        _attend_pairs(q_ref, o_ref, sink_ref, kc, vc, None)


def _attn_a(q, k, v, sink):
    nblk = SEQ // BLOCK
    cblk = CTX_LEN // BLOCK
    nk = HKV_A * HD_A
    qrow = lambda b, j, s: (jnp.where(j < nblk, b * nblk + j, T_LAT // BLOCK + b * cblk + (j - nblk)), 0)
    lat_kv = pl.BlockSpec((SEQ, nk), lambda b, j, s: (b, 0))
    ctx_kv = pl.BlockSpec((CTX_LEN, nk), lambda b, j, s: (T_LAT // CTX_LEN + b, 0))
    return pl.pallas_call(
        _attn_a_kernel,
        grid_spec=pltpu.PrefetchScalarGridSpec(
            num_scalar_prefetch=1,
            grid=(BATCH, nblk + cblk),
            in_specs=[pl.BlockSpec((BLOCK, HQ_A * HD_A), qrow), lat_kv, lat_kv, ctx_kv, ctx_kv],
            out_specs=pl.BlockSpec((BLOCK, HQ_A * HD_A), qrow),
        ),
        out_shape=jax.ShapeDtypeStruct((T_ALL, HQ_A * HD_A), bf16),
        compiler_params=pltpu.CompilerParams(
            dimension_semantics=("arbitrary", "arbitrary"), vmem_limit_bytes=VMEM_LIMIT),
        name="attn_a",
    )(sink, q, k, v, k, v)


def _attn_b_kernel(q_ref, k_ref, v_ref, kc_ref, vc_ref, o_ref):
    rep = HQ_B // HKV_B
    nt = (((1,), (1,)), ((), ()))
    for g in range(HKV_B):
        gs = slice(g * HD_B, (g + 1) * HD_B)
        qs = jnp.concatenate(
            [q_ref[:, (g * rep + r) * HD_B:(g * rep + r + 1) * HD_B] for r in range(rep)], axis=0)
        s1 = lax.dot_general(qs, k_ref[:, gs], nt, preferred_element_type=f32)
        s2 = lax.dot_general(qs, kc_ref[:, gs], nt, preferred_element_type=f32)
        m = jnp.maximum(jnp.max(s1, axis=-1, keepdims=True), jnp.max(s2, axis=-1, keepdims=True))
        e1 = jnp.exp(s1 - m)
        e2 = jnp.exp(s2 - m)
        inv = 1.0 / (jnp.sum(e1, axis=-1, keepdims=True) + jnp.sum(e2, axis=-1, keepdims=True))
        o = (jnp.dot(e1.astype(bf16), v_ref[:, gs], preferred_element_type=f32)
             + jnp.dot(e2.astype(bf16), vc_ref[:, gs], preferred_element_type=f32)) * inv
        for r in range(rep):
            o_ref[:, (g * rep + r) * HD_B:(g * rep + r + 1) * HD_B] = (
                o[r * BLOCK:(r + 1) * BLOCK].astype(bf16))


def _attn_b(q, k, v):
    nblk = SEQ // BLOCK
    nk = HKV_B * HD_B
    qrow = lambda b, j: (b * nblk + j, 0)
    lat_kv = pl.BlockSpec((SEQ, nk), lambda b, j: (b, 0))
    ctx_kv = pl.BlockSpec((CTX_LEN, nk), lambda b, j: (T_LAT // CTX_LEN + b, 0))
    return pl.pallas_call(
        _attn_b_kernel,
        grid=(BATCH, nblk),
        in_specs=[pl.BlockSpec((BLOCK, HQ_B * HD_B), qrow), lat_kv, lat_kv, ctx_kv, ctx_kv],
        out_specs=pl.BlockSpec((BLOCK, HQ_B * HD_B), qrow),
        out_shape=jax.ShapeDtypeStruct((T_LAT, HQ_B * HD_B), bf16),
        compiler_params=pltpu.CompilerParams(
            dimension_semantics=("arbitrary", "arbitrary"), vmem_limit_bytes=VMEM_LIMIT),
        name="attn_b",
    )(q, k, v, k, v)


def _post_attn_kernel(o_ref, wo_ref, x_ref, mod_ref, g1_ref, g2_ref, wr_ref, br_ref,
                      xn_ref, f_ref, ti_ref, tw_ref):
    a = jnp.dot(o_ref[...], wo_ref[...], preferred_element_type=f32)
    x = x_ref[...] + mod_ref[2:3, :] * _rms(a, g1_ref[...])
    xn_ref[...] = x
    f = _rms(x, g2_ref[...]) * (1.0 + mod_ref[4:5, :]) + mod_ref[3:4, :]
    f_ref[...] = f
    logits = jnp.dot(f, wr_ref[...], precision=lax.Precision.HIGHEST,
                     preferred_element_type=f32) + br_ref[...]
    lane = lax.broadcasted_iota(jnp.int32, logits.shape, 1)
    rest = logits
    top_v, top_i = [], []
    for _ in range(TOP_K):
        m = jnp.max(rest, axis=-1, keepdims=True)
        idx = jnp.min(jnp.where(rest == m, lane, N_EXPERTS), axis=-1, keepdims=True)
        top_v.append(m)
        top_i.append(idx)
        rest = jnp.where(lane == idx, -jnp.inf, rest)
    es = [jnp.exp(v - top_v[0]) for v in top_v]
    inv = 1.0 / (es[0] + es[1] + es[2] + es[3])
    slot = lax.broadcasted_iota(jnp.int32, (logits.shape[0], TOP_K), 1)
    ti = jnp.zeros(slot.shape, jnp.int32)
    tw = jnp.zeros(slot.shape, f32)
    for k in range(TOP_K):
        ti = jnp.where(slot == k, top_i[k], ti)
        tw = jnp.where(slot == k, es[k] * inv, tw)
    ti_ref[...] = ti
    tw_ref[...] = tw


def _post_attn(o, wo_bf16, xs, mods, layer, g1, g2, wr, br, n_tiles):
    seg = functools.partial(_seg_of_tile, tiles_per_seq=TILES_PER_SEQ, lat_tiles=LAT_TILES)
    rows = n_tiles * TM
    row = lambda n: pl.BlockSpec((TM, n), lambda i: (i, 0))
    const = lambda a, b: pl.BlockSpec((a, b), lambda i: (0, 0))
    return pl.pallas_call(
        _post_attn_kernel,
        grid=(n_tiles,),
        in_specs=[
            row(D_MODEL), const(D_MODEL, D_MODEL), row(D_MODEL),
            pl.BlockSpec((None, None, 6, D_MODEL), lambda i: (layer, seg(i), 0, 0)),
            const(1, D_MODEL), const(1, D_MODEL), const(D_MODEL, N_EXPERTS), const(1, N_EXPERTS),
        ],
        out_specs=[row(D_MODEL), row(D_MODEL), row(TOP_K), row(TOP_K)],
        out_shape=[jax.ShapeDtypeStruct((rows, D_MODEL), f32),
                   jax.ShapeDtypeStruct((rows, D_MODEL), f32),
                   jax.ShapeDtypeStruct((rows, TOP_K), jnp.int32),
                   jax.ShapeDtypeStruct((rows, TOP_K), f32)],
        compiler_params=pltpu.CompilerParams(
            dimension_semantics=("arbitrary",), vmem_limit_bytes=VMEM_LIMIT),
        name=f"post_attn{layer}",
    )(o, wo_bf16, xs, mods, g1, g2, wr, br)


def _route_meta(top_i, n_tiles_max):
    a = top_i.shape[0] * TOP_K
    e = top_i.reshape(a)
    oh = (e[:, None] == jnp.arange(N_EXPERTS, dtype=jnp.int32)[None, :]).astype(jnp.int32)
    csum = jnp.cumsum(oh, axis=0)
    rank = jnp.sum(csum * oh, axis=1) - 1
    n = csum[-1]
    pn = ((n + TME - 1) // TME) * TME
    ends = jnp.cumsum(pn)
    pos = jnp.sum(oh * (ends - pn)[None, :], axis=1) + rank
    src = jnp.zeros((n_tiles_max * TME,), jnp.int32).at[pos].set(
        jnp.arange(a, dtype=jnp.int32) // TOP_K)
    tile = jnp.arange(n_tiles_max, dtype=jnp.int32)
    n_active = ends[-1] // TME
    te = jnp.sum((jnp.minimum(tile, n_active - 1) * TME)[:, None] >= ends[None, :], axis=1)
    return (src.reshape(n_tiles_max, 1, TME), pos, te.astype(jnp.int32),
            n_active.reshape(1).astype(jnp.int32))


def _gather_rows(idx_ref, n, src_hbm, dst, sem, dst_row):
    def body(j, carry):
        pltpu.make_async_copy(src_hbm.at[pl.ds(idx_ref[0, j], 1)],
                              dst.at[pl.ds(dst_row(j), 1)], sem).start()
        return carry
    lax.fori_loop(0, n, body, 0, unroll=8)


def _moe_ffn_kernel(te_ref, na_ref, src_ref, src_next_ref, f_hbm, w1_ref, b1_ref, w2_ref, b2_ref,
                    y_ref, xbuf, sem, w1b, w2b):
    i = pl.program_id(0)
    na = na_ref[0]
    slot = i % 2
    same = lambda j: j

    @pl.when(i == 0)
    def _():
        _gather_rows(src_ref, TME, f_hbm, xbuf.at[0], sem.at[0], same)

    @pl.when(i + 1 < na)
    def _():
        _gather_rows(src_next_ref, TME, f_hbm, xbuf.at[1 - slot], sem.at[1 - slot], same)

    @pl.when(i >= na)
    def _():
        y_ref[...] = jnp.zeros_like(y_ref)

    @pl.when(i < na)
    def _():
        pltpu.make_async_copy(f_hbm.at[pl.ds(0, TME)], xbuf.at[slot], sem.at[slot]).wait()

        @pl.when((i == 0) | (te_ref[i] != te_ref[jnp.maximum(i - 1, 0)]))
        def _():
            w1b[...] = w1_ref[...].astype(bf16)
            w2b[...] = w2_ref[...].astype(bf16)

        u = jnp.dot(xbuf[slot].astype(bf16), w1b[...], preferred_element_type=f32) + b1_ref[...]
        glu = jnp.minimum(u[:, :D_FF], SWIGLU_LIMIT)
        lin = jnp.clip(u[:, D_FF:], -SWIGLU_LIMIT, SWIGLU_LIMIT)
        act = glu * jax.nn.sigmoid(SWIGLU_ALPHA * glu) * (lin + 1.0)
        y_ref[...] = jnp.dot(act.astype(bf16), w2b[...], preferred_element_type=f32) + b2_ref[...]


def _moe_ffn(f, src, te, na, w1, b1, w2, b2, layer, n_tiles_max):
    last = n_tiles_max - 1
    ex = lambda i, te, na: (layer, te[i], 0, 0)
    return pl.pallas_call(
        _moe_ffn_kernel,
        grid_spec=pltpu.PrefetchScalarGridSpec(
            num_scalar_prefetch=2,
            grid=(n_tiles_max,),
            in_specs=[
                pl.BlockSpec((None, 1, TME), lambda i, te, na: (i, 0, 0), memory_space=pltpu.SMEM),
                pl.BlockSpec((None, 1, TME), lambda i, te, na: (jnp.minimum(i + 1, last), 0, 0),
                             memory_space=pltpu.SMEM),
                pl.BlockSpec(memory_space=pl.ANY),
                pl.BlockSpec((None, None, D_MODEL, 2 * D_FF), ex),
                pl.BlockSpec((None, None, 1, 2 * D_FF), ex),
                pl.BlockSpec((None, None, D_FF, D_MODEL), ex),
                pl.BlockSpec((None, None, 1, D_MODEL), ex),
            ],
            out_specs=pl.BlockSpec((TME, D_MODEL), lambda i, te, na: (i, 0)),
            scratch_shapes=[pltpu.VMEM((2, TME, D_MODEL), f32), pltpu.SemaphoreType.DMA((2,)),
                            pltpu.VMEM((D_MODEL, 2 * D_FF), bf16), pltpu.VMEM((D_FF, D_MODEL), bf16)],
        ),
        out_shape=jax.ShapeDtypeStruct((n_tiles_max * TME, D_MODEL), f32),
        compiler_params=pltpu.CompilerParams(
            dimension_semantics=("arbitrary",), vmem_limit_bytes=VMEM_LIMIT),
        name=f"moe_ffn{layer}",
    )(te, na, src, src, f, w1, b1.reshape(DEPTH, N_EXPERTS, 1, 2 * D_FF), w2,
      b2.reshape(DEPTH, N_EXPERTS, 1, D_MODEL))


def _combine_kernel(pos_ref, pos_next_ref, y_hbm, tw_ref, x_ref, mod_ref, g_ref, xn_ref, ybuf, sem):
    i = pl.program_id(0)
    slot = i % 2
    slot_major = lambda j: (j % TOP_K) * TM + j // TOP_K

    @pl.when(i == 0)
    def _():
        _gather_rows(pos_ref, TOP_K * TM, y_hbm, ybuf.at[0], sem.at[0], slot_major)

    @pl.when(i + 1 < pl.num_programs(0))
    def _():
        _gather_rows(pos_next_ref, TOP_K * TM, y_hbm, ybuf.at[1 - slot], sem.at[1 - slot], slot_major)

    pltpu.make_async_copy(y_hbm.at[pl.ds(0, TOP_K * TM)], ybuf.at[slot], sem.at[slot]).wait()
    tw = tw_ref[...]
    acc = tw[:, 0:1] * ybuf[slot, 0:TM, :]
    for k in range(1, TOP_K):
        acc = acc + tw[:, k:k + 1] * ybuf[slot, k * TM:(k + 1) * TM, :]
    xn_ref[...] = x_ref[...] + mod_ref[5:6, :] * _rms(acc, g_ref[...])


def _combine(y, pos, tw, xs, mods, layer, g, n_tiles):
    seg = functools.partial(_seg_of_tile, tiles_per_seq=TILES_PER_SEQ, lat_tiles=LAT_TILES)
    row = lambda n: pl.BlockSpec((TM, n), lambda i: (i, 0))
    pos3 = pos.reshape(n_tiles, 1, TOP_K * TM)
    return pl.pallas_call(
        _combine_kernel,
        grid=(n_tiles,),
        in_specs=[
            pl.BlockSpec((None, 1, TOP_K * TM), lambda i: (i, 0, 0), memory_space=pltpu.SMEM),
            pl.BlockSpec((None, 1, TOP_K * TM), lambda i: (jnp.minimum(i + 1, n_tiles - 1), 0, 0),
                         memory_space=pltpu.SMEM),
            pl.BlockSpec(memory_space=pl.ANY),
            row(TOP_K), row(D_MODEL),
            pl.BlockSpec((None, None, 6, D_MODEL), lambda i: (layer, seg(i), 0, 0)),
            pl.BlockSpec((1, D_MODEL), lambda i: (0, 0)),
        ],
        out_specs=row(D_MODEL),
        out_shape=jax.ShapeDtypeStruct((n_tiles * TM, D_MODEL), f32),
        scratch_shapes=[pltpu.VMEM((2, TOP_K * TM, D_MODEL), f32), pltpu.SemaphoreType.DMA((2,))],
        compiler_params=pltpu.CompilerParams(
            dimension_semantics=("arbitrary",), vmem_limit_bytes=VMEM_LIMIT),
        name=f"combine{layer}",
    )(pos3, pos3, y, tw, xs, mods, g)


def _moe(f, ti, tw, xs, mods, layer, g, w1, b1, w2, b2, n_tiles):
    n_tiles_max = n_tiles * TM * TOP_K // TME + N_EXPERTS
    src, pos, te, na = _route_meta(ti, n_tiles_max)
    y = _moe_ffn(f, src, te, na, w1, b1, w2, b2, layer, n_tiles_max)
    return _combine(y, pos, tw, xs, mods, layer, g, n_tiles)


def kernel(x, c, ctx, c_ctx, w_ada, b_ada, norm_g, a_w_qkv, a_w_o, a_sink, b_w_qkv, b_q_norm, b_k_norm,
           b_w_o, moe_w_router, moe_b_router, moe_w1, moe_b1, moe_w2, moe_b2):
    assert DEPTH == 2 and x.shape == (BATCH, SEQ, D_MODEL) and ctx.shape == (BATCH, CTX_LEN, D_MODEL)
    xs = jnp.concatenate([x.reshape(T_LAT, D_MODEL), ctx.reshape(T_CTX, D_MODEL)], axis=0)
    c_all = jnp.concatenate(
        [c, c_ctx[None, :], jnp.zeros((MOD_ROWS - BATCH - 1, D_MODEL), f32)], axis=0)
    mods = _adaln(c_all, w_ada, b_ada).reshape(DEPTH, MOD_ROWS, 6, D_MODEL)
    ones = jnp.ones((1, LANES), f32)
    g = lambda i, j: norm_g[i, j][None, :]

    q, k, v = _qkv(xs, mods, 0, g(0, 0), a_w_qkv[0].astype(bf16), _rope_tables(HD_A), ones, ones,
                   hq=HQ_A, hkv=HKV_A, hd=HD_A, qk_norm=False)
    o = _attn_a(q, k, v, a_sink[0])
    xs, f, ti, tw = _post_attn(o, a_w_o[0].astype(bf16), xs, mods, 0, g(0, 1), g(0, 2),
                               moe_w_router[0], moe_b_router[0][None, :], ALL_TILES)
    xs = _moe(f, ti, tw, xs, mods, 0, g(0, 3), moe_w1, moe_b1, moe_w2, moe_b2, ALL_TILES)

    q, k, v = _qkv(xs, mods, 1, g(1, 0), b_w_qkv[0].astype(bf16), _rope_tables(HD_B),
                   b_q_norm[0][None, :], b_k_norm[0][None, :],
                   hq=HQ_B, hkv=HKV_B, hd=HD_B, qk_norm=True)
    o = _attn_b(q, k, v)
    xl, f, ti, tw = _post_attn(o, b_w_o[0].astype(bf16), xs, mods, 1, g(1, 1), g(1, 2),
                               moe_w_router[1], moe_b_router[1][None, :], LAT_TILES)
    xl = _moe(f, ti, tw, xl, mods, 1, g(1, 3), moe_w1, moe_b1, moe_w2, moe_b2, LAT_TILES)
    return xl.reshape(BATCH, SEQ, D_MODEL)
```

```python
import functools

import jax
import jax.numpy as jnp
from jax import lax
from jax.experimental import pallas as pl
from jax.experimental.pallas import tpu as pltpu

D_MODEL = 1024
BATCH = 8
SEQ = 2048
DEPTH = 2
GRID_W = 64
CTX_LEN = 256
BLOCK = 128
WINDOW = 128
ROPE_BASE = 10000.0
EPS = 1e-6
HQ_A, HKV_A, HD_A = 16, 2, 64
HQ_B, HKV_B, HD_B = 8, 2, 128
N_EXPERTS = 32
TOP_K = 4
D_FF = D_MODEL
SWIGLU_LIMIT = 7.0
SWIGLU_ALPHA = 1.702

T_LAT = BATCH * SEQ
T_CTX = BATCH * CTX_LEN
T_ALL = T_LAT + T_CTX
LANES = 128
TM = 256
LAT_TILES = T_LAT // TM
ALL_TILES = T_ALL // TM
TILES_PER_SEQ = SEQ // TM
MOD_ROWS = 16
CTX_MOD_ROW = BATCH
TME = 256
SLOT_ROWS = 8
RUN_ALIGN = 8
LOCAL_ROWS = -(-(TM * TOP_K + N_EXPERTS * (RUN_ALIGN - 1)) // LANES) * LANES
NEG = -1e30
VMEM_LIMIT = 56 * 1024 * 1024

f32 = jnp.float32
bf16 = jnp.bfloat16


def _seg_of_tile(i, tiles_per_seq, lat_tiles):
    return jnp.where(i < lat_tiles, i // tiles_per_seq, CTX_MOD_ROW)


def _adaln_kernel(c_ref, w_ref, b_ref, o_ref):
    c = c_ref[...]
    s = c * jax.nn.sigmoid(c)
    o_ref[...] = jnp.dot(s, w_ref[...], precision=lax.Precision.HIGHEST,
                         preferred_element_type=f32) + b_ref[...]


def _adaln(c_all, w_ada, b_ada):
    tn = 1536
    return pl.pallas_call(
        _adaln_kernel,
        grid=(DEPTH, 6 * D_MODEL // tn),
        in_specs=[
            pl.BlockSpec((MOD_ROWS, D_MODEL), lambda l, j: (0, 0)),
            pl.BlockSpec((None, D_MODEL, tn), lambda l, j: (l, 0, j)),
            pl.BlockSpec((None, 1, tn), lambda l, j: (l, 0, j)),
        ],
        out_specs=pl.BlockSpec((None, MOD_ROWS, tn), lambda l, j: (l, 0, j)),
        out_shape=jax.ShapeDtypeStruct((DEPTH, MOD_ROWS, 6 * D_MODEL), f32),
        compiler_params=pltpu.CompilerParams(
            dimension_semantics=("arbitrary", "arbitrary"), vmem_limit_bytes=VMEM_LIMIT),
        name="adaln",
    )(c_all, w_ada, b_ada.reshape(DEPTH, 1, 6 * D_MODEL))


def _rms(x, g):
    return x * lax.rsqrt(jnp.mean(x * x, axis=-1, keepdims=True) + EPS) * g


def _qkv_kernel(x_ref, mod_ref, g_ref, w_ref, cos_ref, sa_ref, sb_ref, qn_ref, kn_ref,
                q_ref, k_ref, v_ref, *, nq, nk, hd, qk_norm):
    h = _rms(x_ref[...], g_ref[...]) * (1.0 + mod_ref[1:2, :]) + mod_ref[0:1, :]
    qkv = jnp.dot(h.astype(bf16), w_ref[...], preferred_element_type=f32)
    cos, sa, sb = cos_ref[...], sa_ref[...], sb_ref[...]
    quarter = hd // 4
    scale = hd ** -0.5

    def rope(c):
        return c * cos + pltpu.roll(c, quarter, 1) * sa + pltpu.roll(c, LANES - quarter, 1) * sb

    for j in range(nq // LANES):
        c = qkv[:, j * LANES:(j + 1) * LANES]
        if qk_norm:
            c = _rms(c, qn_ref[...])
        q_ref[:, j * LANES:(j + 1) * LANES] = (rope(c) * scale).astype(bf16)
    for j in range(nk // LANES):
        c = qkv[:, nq + j * LANES:nq + (j + 1) * LANES]
        if qk_norm:
            c = _rms(c, kn_ref[...])
        k_ref[:, j * LANES:(j + 1) * LANES] = rope(c).astype(bf16)
    v_ref[...] = qkv[:, nq + nk:].astype(bf16)


def _qkv(xs, mods, layer, g, w_bf16, tabs, qn, kn, *, hq, hkv, hd, qk_norm):
    nq, nk = hq * hd, hkv * hd
    seg = functools.partial(_seg_of_tile, tiles_per_seq=TILES_PER_SEQ, lat_tiles=LAT_TILES)
    tab_idx = lambda i: (jnp.where(i < LAT_TILES, i % TILES_PER_SEQ, TILES_PER_SEQ), 0)
    tab_spec = pl.BlockSpec((TM, LANES), tab_idx)
    row = lambda n: pl.BlockSpec((TM, n), lambda i: (i, 0))
    return pl.pallas_call(
        functools.partial(_qkv_kernel, nq=nq, nk=nk, hd=hd, qk_norm=qk_norm),
        grid=(ALL_TILES,),
        in_specs=[
            row(D_MODEL),
            pl.BlockSpec((None, None, 6, D_MODEL), lambda i: (layer, seg(i), 0, 0)),
            pl.BlockSpec((1, D_MODEL), lambda i: (0, 0)),
            pl.BlockSpec((D_MODEL, nq + 2 * nk), lambda i: (0, 0)),
            tab_spec, tab_spec, tab_spec,
            pl.BlockSpec((1, LANES), lambda i: (0, 0)),
            pl.BlockSpec((1, LANES), lambda i: (0, 0)),
        ],
        out_specs=[row(nq), row(nk), row(nk)],
        out_shape=[jax.ShapeDtypeStruct((T_ALL, nq), bf16),
                   jax.ShapeDtypeStruct((T_ALL, nk), bf16),
                   jax.ShapeDtypeStruct((T_ALL, nk), bf16)],
        compiler_params=pltpu.CompilerParams(
            dimension_semantics=("arbitrary",), vmem_limit_bytes=VMEM_LIMIT),
        name=f"qkv{layer}",
    )(xs, mods, g, w_bf16, *tabs, qn, kn)


def _rope_tables(hd):
    quarter = hd // 4
    inv_freq = jnp.float32(ROPE_BASE) ** (-jnp.arange(quarter, dtype=f32) / quarter)
    t = jnp.arange(SEQ)
    ang_r = (t // GRID_W).astype(f32)[:, None] * inv_freq[None, :]
    ang_c = (t % GRID_W).astype(f32)[:, None] * inv_freq[None, :]
    z = jnp.zeros_like(ang_r)
    cos = jnp.concatenate([jnp.cos(ang_r)] * 2 + [jnp.cos(ang_c)] * 2, axis=-1)
    sa = jnp.concatenate([z, jnp.sin(ang_r), z, jnp.sin(ang_c)], axis=-1)
    sb = jnp.concatenate([-jnp.sin(ang_r), z, -jnp.sin(ang_c), z], axis=-1)
    rep = LANES // hd
    pad = lambda a, v: jnp.concatenate(
        [jnp.tile(a, (1, rep)), jnp.full((TM, LANES), v, f32)], axis=0)
    return pad(cos, 1.0), pad(sa, 0.0), pad(sb, 0.0)


def _pair_operand(x, g):
    lane = lax.broadcasted_iota(jnp.int32, x.shape, 1)
    swapped = pltpu.roll(x, HD_A, 1)
    lo_src, hi_src = (x, swapped) if g == 0 else (swapped, x)
    lo = jnp.where(lane < HD_A, lo_src, 0.0)
    hi = jnp.where(lane >= HD_A, hi_src, 0.0)
    return jnp.concatenate([lo, hi], axis=0).astype(bf16)


def _attend_pairs(q_ref, o_ref, sink_ref, kcat, vcat, mask):
    nkeys = kcat.shape[0]
    lane = lax.broadcasted_iota(jnp.int32, (BLOCK, LANES), 1)
    for g in range(HKV_A):
        kp = _pair_operand(kcat, g)
        vp = _pair_operand(vcat, g)
        for p in range(HQ_A // HKV_A // 2):
            col = (g * (HQ_A // HKV_A // 2) + p) * LANES
            s = lax.dot_general(q_ref[:, col:col + LANES], kp, (((1,), (1,)), ((), ())),
                                preferred_element_type=f32)
            es, inv = [], []
            for hh in range(2):
                sh = s[:, hh * nkeys:(hh + 1) * nkeys]
                if mask is not None:
                    sh = jnp.where(mask, sh, NEG)
                sk = sink_ref[col // HD_A + hh]
                m = jnp.maximum(jnp.max(sh, axis=-1, keepdims=True), sk)
                e = jnp.exp(sh - m)
                inv.append(1.0 / (jnp.sum(e, axis=-1, keepdims=True) + jnp.exp(sk - m)))
                es.append(e.astype(bf16))
            o = jnp.dot(jnp.concatenate(es, axis=1), vp, preferred_element_type=f32)
            o_ref[:, col:col + LANES] = (o * jnp.where(lane < HD_A, inv[0], inv[1])).astype(bf16)


def _attn_a_kernel(sink_ref, q_ref, k_ref, v_ref, kc_ref, vc_ref, o_ref):
    j = pl.program_id(1)
    nblk = SEQ // BLOCK
    kc = kc_ref[...].astype(f32)
    vc = vc_ref[...].astype(f32)

    @pl.when(j < nblk)
    def _():
        wlen = 3 * BLOCK
        s0 = pl.multiple_of(jnp.clip((j - 1) * BLOCK, 0, SEQ - wlen), BLOCK)
        kcat = jnp.concatenate([k_ref[pl.ds(s0, wlen), :].astype(f32), kc], axis=0)
        vcat = jnp.concatenate([v_ref[pl.ds(s0, wlen), :].astype(f32), vc], axis=0)
        qpos = j * BLOCK + lax.broadcasted_iota(jnp.int32, (BLOCK, wlen + CTX_LEN), 0)
        col = lax.broadcasted_iota(jnp.int32, (BLOCK, wlen + CTX_LEN), 1)
        mask = (jnp.abs(qpos - (s0 + col)) <= WINDOW) | (col >= wlen)
        _attend_pairs(q_ref, o_ref, sink_ref, kcat, vcat, mask)

    @pl.when(j >= nblk)
    def _():
        _attend_pairs(q_ref, o_ref, sink_ref, kc, vc, None)


def _attn_a(q, k, v, sink):
    nblk = SEQ // BLOCK
    cblk = CTX_LEN // BLOCK
    nk = HKV_A * HD_A
    qrow = lambda b, j, s: (jnp.where(j < nblk, b * nblk + j, T_LAT // BLOCK + b * cblk + (j - nblk)), 0)
    lat_kv = pl.BlockSpec((SEQ, nk), lambda b, j, s: (b, 0))
    ctx_kv = pl.BlockSpec((CTX_LEN, nk), lambda b, j, s: (T_LAT // CTX_LEN + b, 0))
    return pl.pallas_call(
        _attn_a_kernel,
        grid_spec=pltpu.PrefetchScalarGridSpec(
            num_scalar_prefetch=1,
            grid=(BATCH, nblk + cblk),
            in_specs=[pl.BlockSpec((BLOCK, HQ_A * HD_A), qrow), lat_kv, lat_kv, ctx_kv, ctx_kv],
            out_specs=pl.BlockSpec((BLOCK, HQ_A * HD_A), qrow),
        ),
        out_shape=jax.ShapeDtypeStruct((T_ALL, HQ_A * HD_A), bf16),
        compiler_params=pltpu.CompilerParams(
            dimension_semantics=("arbitrary", "arbitrary"), vmem_limit_bytes=VMEM_LIMIT),
        name="attn_a",
    )(sink, q, k, v, k, v)


def _attn_b_kernel(q_ref, k_ref, v_ref, kc_ref, vc_ref, o_ref):
    rep = HQ_B // HKV_B
    nt = (((1,), (1,)), ((), ()))
    for g in range(HKV_B):
        gs = slice(g * HD_B, (g + 1) * HD_B)
        qs = jnp.concatenate(
            [q_ref[:, (g * rep + r) * HD_B:(g * rep + r + 1) * HD_B] for r in range(rep)], axis=0)
        s1 = lax.dot_general(qs, k_ref[:, gs], nt, preferred_element_type=f32)
        s2 = lax.dot_general(qs, kc_ref[:, gs], nt, preferred_element_type=f32)
        m = jnp.maximum(jnp.max(s1, axis=-1, keepdims=True), jnp.max(s2, axis=-1, keepdims=True))
        e1 = jnp.exp(s1 - m)
        e2 = jnp.exp(s2 - m)
        inv = 1.0 / (jnp.sum(e1, axis=-1, keepdims=True) + jnp.sum(e2, axis=-1, keepdims=True))
        o = (jnp.dot(e1.astype(bf16), v_ref[:, gs], preferred_element_type=f32)
             + jnp.dot(e2.astype(bf16), vc_ref[:, gs], preferred_element_type=f32)) * inv
        for r in range(rep):
            o_ref[:, (g * rep + r) * HD_B:(g * rep + r + 1) * HD_B] = (
                o[r * BLOCK:(r + 1) * BLOCK].astype(bf16))


def _attn_b(q, k, v):
    nblk = SEQ // BLOCK
    nk = HKV_B * HD_B
    qrow = lambda b, j: (b * nblk + j, 0)
    lat_kv = pl.BlockSpec((SEQ, nk), lambda b, j: (b, 0))
    ctx_kv = pl.BlockSpec((CTX_LEN, nk), lambda b, j: (T_LAT // CTX_LEN + b, 0))
    return pl.pallas_call(
        _attn_b_kernel,
        grid=(BATCH, nblk),
        in_specs=[pl.BlockSpec((BLOCK, HQ_B * HD_B), qrow), lat_kv, lat_kv, ctx_kv, ctx_kv],
        out_specs=pl.BlockSpec((BLOCK, HQ_B * HD_B), qrow),
        out_shape=jax.ShapeDtypeStruct((T_LAT, HQ_B * HD_B), bf16),
        compiler_params=pltpu.CompilerParams(
            dimension_semantics=("arbitrary", "arbitrary"), vmem_limit_bytes=VMEM_LIMIT),
        name="attn_b",
    )(q, k, v, k, v)


NT_DIMS = (((1,), (1,)), ((), ()))


def _post_attn_kernel(o_ref, wo_ref, x_ref, mod_ref, g1_ref, g2_ref, wrt_ref, brt_ref,
                      xn_ref, f_ref, tit_ref, ti_ref, tw_ref, cnt_ref):
    a = jnp.dot(o_ref[...], wo_ref[...], preferred_element_type=f32)
    x = x_ref[...] + mod_ref[2:3, :] * _rms(a, g1_ref[...])
    xn_ref[...] = x
    f = _rms(x, g2_ref[...]) * (1.0 + mod_ref[4:5, :]) + mod_ref[3:4, :]
    f_ref[...] = f.astype(bf16)
    logits = lax.dot_general(wrt_ref[...], f, NT_DIMS, precision=lax.Precision.HIGHEST,
                             preferred_element_type=f32) + brt_ref[...]
    sub = lax.broadcasted_iota(jnp.int32, logits.shape, 0)
    rest = logits
    top_v, top_i = [], []
    for _ in range(TOP_K):
        m = jnp.max(rest, axis=0, keepdims=True)
        idx = jnp.min(jnp.where(rest == m, sub, N_EXPERTS), axis=0, keepdims=True)
        top_v.append(m)
        top_i.append(idx)
        rest = jnp.where(sub == idx, -jnp.inf, rest)
    es = [jnp.exp(v - top_v[0]) for v in top_v]
    inv = 1.0 / (es[0] + es[1] + es[2] + es[3])
    k8 = lax.broadcasted_iota(jnp.int32, (SLOT_ROWS, TM), 0)
    tit = jnp.full((SLOT_ROWS, TM), -1.0, f32)
    twt = jnp.zeros((SLOT_ROWS, TM), f32)
    for k in range(TOP_K):
        tit = jnp.where(k8 == k, top_i[k].astype(f32), tit)
        twt = jnp.where(k8 == k, es[k] * inv, twt)
    tit_ref[...] = tit
    eye = (lax.broadcasted_iota(jnp.int32, (TM, TM), 0)
           == lax.broadcasted_iota(jnp.int32, (TM, TM), 1)).astype(f32)
    to_rows = lambda t: lax.dot_general(eye, t, NT_DIMS, precision=lax.Precision.HIGHEST,
                                        preferred_element_type=f32)
    ti = to_rows(tit)
    ti_ref[...] = ti
    tw_ref[...] = to_rows(twt)
    lane = lax.broadcasted_iota(jnp.int32, (TM, N_EXPERTS), 1).astype(f32)
    cnt = jnp.zeros((TM, N_EXPERTS), f32)
    for k in range(TOP_K):
        cnt = cnt + (ti[:, k:k + 1] == lane).astype(f32)
    cnt_ref[...] = jnp.sum(cnt, axis=0, keepdims=True)


def _post_attn(o, wo_bf16, xs, mods, layer, g1, g2, wr, br, n_tiles):
    seg = functools.partial(_seg_of_tile, tiles_per_seq=TILES_PER_SEQ, lat_tiles=LAT_TILES)
    rows = n_tiles * TM
    row = lambda n: pl.BlockSpec((TM, n), lambda i: (i, 0))
    const = lambda a, b: pl.BlockSpec((a, b), lambda i: (0, 0))
    return pl.pallas_call(
        _post_attn_kernel,
        grid=(n_tiles,),
        in_specs=[
            row(D_MODEL), const(D_MODEL, D_MODEL), row(D_MODEL),
            pl.BlockSpec((None, None, 6, D_MODEL), lambda i: (layer, seg(i), 0, 0)),
            const(1, D_MODEL), const(1, D_MODEL), const(N_EXPERTS, D_MODEL), const(N_EXPERTS, 1),
        ],
        out_specs=[row(D_MODEL), row(D_MODEL),
                   pl.BlockSpec((SLOT_ROWS, TM), lambda i: (0, i)),
                   row(SLOT_ROWS), row(SLOT_ROWS),
                   pl.BlockSpec((None, 1, N_EXPERTS), lambda i: (i, 0, 0))],
        out_shape=[jax.ShapeDtypeStruct((rows, D_MODEL), f32),
                   jax.ShapeDtypeStruct((rows, D_MODEL), bf16),
                   jax.ShapeDtypeStruct((SLOT_ROWS, rows), f32),
                   jax.ShapeDtypeStruct((rows, SLOT_ROWS), f32),
                   jax.ShapeDtypeStruct((rows, SLOT_ROWS), f32),
                   jax.ShapeDtypeStruct((n_tiles, 1, N_EXPERTS), f32)],
        compiler_params=pltpu.CompilerParams(
            dimension_semantics=("arbitrary",), vmem_limit_bytes=VMEM_LIMIT),
        name=f"post_attn{layer}",
    )(o, wo_bf16, xs, mods, g1, g2, wr.T, br.reshape(N_EXPERTS, 1))


def _ffn_tiles_max(n_tiles):
    rows = n_tiles * (TM * TOP_K + N_EXPERTS * (RUN_ALIGN - 1)) + N_EXPERTS * (TME - RUN_ALIGN)
    return -(-rows // TME)


def _route_tables(cnt, n_tiles_max):
    n = cnt.reshape(-1, N_EXPERTS).astype(jnp.int32)
    run = (n + RUN_ALIGN - 1) // RUN_ALIGN * RUN_ALIGN
    loff = jnp.cumsum(run, axis=1) - run
    tot = jnp.sum(run, axis=0)
    gsz = (tot + TME - 1) // TME * TME
    ends = jnp.cumsum(gsz)
    goff = (ends - gsz)[None, :] + jnp.cumsum(run, axis=0) - run
    tab = jnp.concatenate([loff, goff, run // RUN_ALIGN], axis=1)[:, None, :]
    n_active = ends[-1] // TME
    tail = jnp.concatenate([ends - gsz + tot, (gsz - tot) // RUN_ALIGN, n_active[None]])[None, :]
    tile = jnp.arange(n_tiles_max, dtype=jnp.int32)
    te = jnp.sum((jnp.minimum(tile, n_active - 1) * TME)[:, None] >= ends[None, :], axis=1)
    loff_f = loff.astype(f32)
    return (tab, tail, loff_f[:, None, :], loff_f[:, :, None], te.astype(jnp.int32),
            n_active.reshape(1).astype(jnp.int32))


def _for_each_run(tab_ref, fn):
    for e in range(N_EXPERTS):
        def body(c, carry, e=e):
            fn(e, c)
            return carry
        lax.fori_loop(0, tab_ref[0, 2 * N_EXPERTS + e], body, 0)


def _chunk(base, c):
    return pl.ds(pl.multiple_of(base + c * RUN_ALIGN, RUN_ALIGN), RUN_ALIGN)


def _dispatch_kernel(tab_ref, tail_ref, f_ref, tit_ref, loffc_ref, xs_hbm, lbuf, zbuf, sem):
    i = pl.program_id(0)
    tit = tit_ref[...]
    sub = lax.broadcasted_iota(jnp.int32, (N_EXPERTS, TM), 0).astype(f32)
    hit = [tit[k:k + 1, :] == sub for k in range(TOP_K)]
    cnt = sum(h.astype(f32) for h in hit)
    before = (lax.broadcasted_iota(jnp.int32, (TM, TM), 0)
              < lax.broadcasted_iota(jnp.int32, (TM, TM), 1)).astype(bf16)
    base = loffc_ref[...] + jnp.dot(cnt.astype(bf16), before, preferred_element_type=f32)
    rows = lax.broadcasted_iota(jnp.int32, (LOCAL_ROWS, TM), 0).astype(f32)
    onehot = jnp.zeros((LOCAL_ROWS, TM), f32)
    for k in range(TOP_K):
        lp = jnp.sum(jnp.where(hit[k], base, 0.0), axis=0, keepdims=True)
        onehot = onehot + (rows == lp).astype(f32)
    lbuf[...] = jnp.dot(onehot.astype(bf16), f_ref[...], preferred_element_type=f32)

    run_copy = lambda e, c: pltpu.make_async_copy(
        lbuf.at[_chunk(tab_ref[0, e], c)], xs_hbm.at[_chunk(tab_ref[0, N_EXPERTS + e], c)], sem)
    _for_each_run(tab_ref, lambda e, c: run_copy(e, c).start())

    tail_copy = lambda e, c: pltpu.make_async_copy(zbuf, xs_hbm.at[_chunk(tail_ref[0, e], c)], sem)

    def for_each_tail(fn):
        for e in range(N_EXPERTS):
            def body(c, carry, e=e):
                fn(e, c)
                return carry
            lax.fori_loop(0, tail_ref[0, N_EXPERTS + e], body, 0)

    @pl.when(i == pl.num_programs(0) - 1)
    def _():
        zbuf[...] = jnp.zeros_like(zbuf)
        for_each_tail(lambda e, c: tail_copy(e, c).start())
        for_each_tail(lambda e, c: tail_copy(e, c).wait())

    _for_each_run(tab_ref, lambda e, c: run_copy(e, c).wait())

    @pl.when(i == pl.num_programs(0) - 1)
    def _():
        lbuf[0:TME, :] = jnp.zeros((TME, D_MODEL), f32)
        n_active = tail_ref[0, 2 * N_EXPERTS]
        n_spare = xs_hbm.shape[0] // TME - n_active
        spare_copy = lambda j: pltpu.make_async_copy(
            lbuf.at[0:TME], xs_hbm.at[pl.ds(pl.multiple_of((n_active + j) * TME, TME), TME)], sem)
        lax.fori_loop(0, n_spare, lambda j, carry: (spare_copy(j).start(), carry)[1], 0)
        lax.fori_loop(0, n_spare, lambda j, carry: (spare_copy(j).wait(), carry)[1], 0)


def _dispatch(f, tit, tab, tail, loffc, layer, n_tiles, n_tiles_max):
    smem = functools.partial(pl.BlockSpec, memory_space=pltpu.SMEM)
    return pl.pallas_call(
        _dispatch_kernel,
        grid=(n_tiles,),
        in_specs=[
            smem((None, 1, 3 * N_EXPERTS), lambda i: (i, 0, 0)),
            smem((1, 2 * N_EXPERTS + 1), lambda i: (0, 0)),
            pl.BlockSpec((TM, D_MODEL), lambda i: (i, 0)),
            pl.BlockSpec((SLOT_ROWS, TM), lambda i: (0, i)),
            pl.BlockSpec((None, N_EXPERTS, 1), lambda i: (i, 0, 0)),
        ],
        out_specs=pl.BlockSpec(memory_space=pl.ANY),
        out_shape=jax.ShapeDtypeStruct((n_tiles_max * TME, D_MODEL), f32),
        scratch_shapes=[pltpu.VMEM((LOCAL_ROWS, D_MODEL), f32), pltpu.VMEM((RUN_ALIGN, D_MODEL), f32),
                        pltpu.SemaphoreType.DMA(())],
        compiler_params=pltpu.CompilerParams(
            dimension_semantics=("arbitrary",), vmem_limit_bytes=VMEM_LIMIT),
        name=f"dispatch{layer}",
    )(tab, tail, f, tit, loffc)


def _moe_ffn_kernel(te_ref, na_ref, x_ref, w1_ref, b1_ref, w2_ref, b2_ref, y_ref, w1b, w2b):
    i = pl.program_id(0)
    na = na_ref[0]

    @pl.when(i >= na)
    def _():
        y_ref[...] = jnp.zeros_like(y_ref)

    @pl.when(i < na)
    def _():
        @pl.when((i == 0) | (te_ref[i] != te_ref[jnp.maximum(i - 1, 0)]))
        def _():
            w1b[...] = w1_ref[...].astype(bf16)
            w2b[...] = w2_ref[...].astype(bf16)

        u = jnp.dot(x_ref[...].astype(bf16), w1b[...], preferred_element_type=f32) + b1_ref[...]
        glu = jnp.minimum(u[:, :D_FF], SWIGLU_LIMIT)
        lin = jnp.clip(u[:, D_FF:], -SWIGLU_LIMIT, SWIGLU_LIMIT)
        act = glu * jax.nn.sigmoid(SWIGLU_ALPHA * glu) * (lin + 1.0)
        y_ref[...] = jnp.dot(act.astype(bf16), w2b[...], preferred_element_type=f32) + b2_ref[...]


def _moe_ffn(xs_sorted, te, na, w1, b1, w2, b2, layer, n_tiles_max):
    ex = lambda i, te, na: (layer, te[i], 0, 0)
    return pl.pallas_call(
        _moe_ffn_kernel,
        grid_spec=pltpu.PrefetchScalarGridSpec(
            num_scalar_prefetch=2,
            grid=(n_tiles_max,),
            in_specs=[
                pl.BlockSpec((TME, D_MODEL), lambda i, te, na: (jnp.minimum(i, na[0] - 1), 0)),
                pl.BlockSpec((None, None, D_MODEL, 2 * D_FF), ex),
                pl.BlockSpec((None, None, 1, 2 * D_FF), ex),
                pl.BlockSpec((None, None, D_FF, D_MODEL), ex),
                pl.BlockSpec((None, None, 1, D_MODEL), ex),
            ],
            out_specs=pl.BlockSpec((TME, D_MODEL), lambda i, te, na: (i, 0)),
            scratch_shapes=[pltpu.VMEM((D_MODEL, 2 * D_FF), bf16), pltpu.VMEM((D_FF, D_MODEL), bf16)],
        ),
        out_shape=jax.ShapeDtypeStruct((n_tiles_max * TME, D_MODEL), f32),
        compiler_params=pltpu.CompilerParams(
            dimension_semantics=("arbitrary",), vmem_limit_bytes=VMEM_LIMIT),
        name=f"moe_ffn{layer}",
    )(te, na, xs_sorted, w1, b1.reshape(DEPTH, N_EXPERTS, 1, 2 * D_FF), w2,
      b2.reshape(DEPTH, N_EXPERTS, 1, D_MODEL))


def _combine_kernel(tab_ref, tab_next_ref, y_hbm, ti_ref, tw_ref, loffr_ref, x_ref, mod_ref, g_ref,
                    xn_ref, ybuf, sem):
    i = pl.program_id(0)
    slot = i % 2

    def run_copy(tab, s):
        return lambda e, c: pltpu.make_async_copy(
            y_hbm.at[_chunk(tab[0, N_EXPERTS + e], c)], ybuf.at[s, _chunk(tab[0, e], c)], sem.at[s])

    @pl.when(i == 0)
    def _():
        ybuf[...] = jnp.zeros_like(ybuf)
        _for_each_run(tab_ref, lambda e, c: run_copy(tab_ref, 0)(e, c).start())

    @pl.when(i + 1 < pl.num_programs(0))
    def _():
        _for_each_run(tab_next_ref, lambda e, c: run_copy(tab_next_ref, 1 - slot)(e, c).start())

    _for_each_run(tab_ref, lambda e, c: run_copy(tab_ref, slot)(e, c).wait())

    ti = ti_ref[...]
    tw = tw_ref[...]
    lane = lax.broadcasted_iota(jnp.int32, (TM, N_EXPERTS), 1).astype(f32)
    hit = [ti[:, k:k + 1] == lane for k in range(TOP_K)]
    cnt = sum(h.astype(f32) for h in hit)
    before = (lax.broadcasted_iota(jnp.int32, (TM, TM), 1)
              < lax.broadcasted_iota(jnp.int32, (TM, TM), 0)).astype(bf16)
    base = loffr_ref[...] + jnp.dot(before, cnt.astype(bf16), preferred_element_type=f32)
    cols = lax.broadcasted_iota(jnp.int32, (TM, LOCAL_ROWS), 1).astype(f32)
    w = jnp.zeros((TM, LOCAL_ROWS), f32)
    for k in range(TOP_K):
        lp = jnp.sum(jnp.where(hit[k], base, 0.0), axis=1, keepdims=True)
        w = jnp.where(cols == lp, tw[:, k:k + 1], w)
    w_hi = w.astype(bf16)
    w_lo = (w - w_hi.astype(f32)).astype(bf16)
    yl = ybuf[slot].astype(bf16)
    acc = (jnp.dot(w_hi, yl, preferred_element_type=f32)
           + jnp.dot(w_lo, yl, preferred_element_type=f32))
    xn_ref[...] = x_ref[...] + mod_ref[5:6, :] * _rms(acc, g_ref[...])


def _combine(y, tab, ti, tw, loffr, xs, mods, layer, g, n_tiles):
    seg = functools.partial(_seg_of_tile, tiles_per_seq=TILES_PER_SEQ, lat_tiles=LAT_TILES)
    row = lambda n: pl.BlockSpec((TM, n), lambda i: (i, 0))
    smem = functools.partial(pl.BlockSpec, memory_space=pltpu.SMEM)
    return pl.pallas_call(
        _combine_kernel,
        grid=(n_tiles,),
        in_specs=[
            smem((None, 1, 3 * N_EXPERTS), lambda i: (i, 0, 0)),
            smem((None, 1, 3 * N_EXPERTS), lambda i: (jnp.minimum(i + 1, n_tiles - 1), 0, 0)),
            pl.BlockSpec(memory_space=pl.ANY),
            row(SLOT_ROWS), row(SLOT_ROWS),
            pl.BlockSpec((None, 1, N_EXPERTS), lambda i: (i, 0, 0)),
            row(D_MODEL),
            pl.BlockSpec((None, None, 6, D_MODEL), lambda i: (layer, seg(i), 0, 0)),
            pl.BlockSpec((1, D_MODEL), lambda i: (0, 0)),
        ],
        out_specs=row(D_MODEL),
        out_shape=jax.ShapeDtypeStruct((n_tiles * TM, D_MODEL), f32),
        scratch_shapes=[pltpu.VMEM((2, LOCAL_ROWS, D_MODEL), f32), pltpu.SemaphoreType.DMA((2,))],
        compiler_params=pltpu.CompilerParams(
            dimension_semantics=("arbitrary",), vmem_limit_bytes=VMEM_LIMIT),
        name=f"combine{layer}",
    )(tab, tab, y, ti, tw, loffr, xs, mods, g)


def _moe(f, tit, ti, tw, cnt, xs, mods, layer, g, w1, b1, w2, b2, n_tiles):
    n_tiles_max = _ffn_tiles_max(n_tiles)
    tab, tail, loffr, loffc, te, na = _route_tables(cnt, n_tiles_max)
    xs_sorted = _dispatch(f, tit, tab, tail, loffc, layer, n_tiles, n_tiles_max)
    y = _moe_ffn(xs_sorted, te, na, w1, b1, w2, b2, layer, n_tiles_max)
    return _combine(y, tab, ti, tw, loffr, xs, mods, layer, g, n_tiles)


def kernel(x, c, ctx, c_ctx, w_ada, b_ada, norm_g, a_w_qkv, a_w_o, a_sink, b_w_qkv, b_q_norm, b_k_norm,
           b_w_o, moe_w_router, moe_b_router, moe_w1, moe_b1, moe_w2, moe_b2):
    assert DEPTH == 2 and x.shape == (BATCH, SEQ, D_MODEL) and ctx.shape == (BATCH, CTX_LEN, D_MODEL)
    xs = jnp.concatenate([x.reshape(T_LAT, D_MODEL), ctx.reshape(T_CTX, D_MODEL)], axis=0)
    c_all = jnp.concatenate(
        [c, c_ctx[None, :], jnp.zeros((MOD_ROWS - BATCH - 1, D_MODEL), f32)], axis=0)
    mods = _adaln(c_all, w_ada, b_ada).reshape(DEPTH, MOD_ROWS, 6, D_MODEL)
    ones = jnp.ones((1, LANES), f32)
    g = lambda i, j: norm_g[i, j][None, :]

    q, k, v = _qkv(xs, mods, 0, g(0, 0), a_w_qkv[0].astype(bf16), _rope_tables(HD_A), ones, ones,
                   hq=HQ_A, hkv=HKV_A, hd=HD_A, qk_norm=False)
    o = _attn_a(q, k, v, a_sink[0])
    xs, *routed = _post_attn(o, a_w_o[0].astype(bf16), xs, mods, 0, g(0, 1), g(0, 2),
                             moe_w_router[0], moe_b_router[0], ALL_TILES)
    xs = _moe(*routed, xs, mods, 0, g(0, 3), moe_w1, moe_b1, moe_w2, moe_b2, ALL_TILES)

    q, k, v = _qkv(xs, mods, 1, g(1, 0), b_w_qkv[0].astype(bf16), _rope_tables(HD_B),
                   b_q_norm[0][None, :], b_k_norm[0][None, :],
                   hq=HQ_B, hkv=HKV_B, hd=HD_B, qk_norm=True)
    o = _attn_b(q, k, v)
    xl, *routed = _post_attn(o, b_w_o[0].astype(bf16), xs, mods, 1, g(1, 1), g(1, 2),
                             moe_w_router[1], moe_b_router[1], LAT_TILES)
    xl = _moe(*routed, xl, mods, 1, g(1, 3), moe_w1, moe_b1, moe_w2, moe_b2, LAT_TILES)
    return xl.reshape(BATCH, SEQ, D_MODEL)
```

```python
import functools

import jax
import jax.numpy as jnp
from jax import lax
from jax.experimental import pallas as pl
from jax.experimental.pallas import tpu as pltpu

D_MODEL = 1024
BATCH = 8
SEQ = 2048
DEPTH = 2
GRID_W = 64
CTX_LEN = 256
BLOCK = 128
WINDOW = 128
ROPE_BASE = 10000.0
EPS = 1e-6
HQ_A, HKV_A, HD_A = 16, 2, 64
HQ_B, HKV_B, HD_B = 8, 2, 128
N_EXPERTS = 32
TOP_K = 4
D_FF = D_MODEL
SWIGLU_LIMIT = 7.0
SWIGLU_ALPHA = 1.702

T_LAT = BATCH * SEQ
T_CTX = BATCH * CTX_LEN
T_ALL = T_LAT + T_CTX
LANES = 128
TM = 256
LAT_TILES = T_LAT // TM
ALL_TILES = T_ALL // TM
TILES_PER_SEQ = SEQ // TM
MOD_ROWS = 16
CTX_MOD_ROW = BATCH
TME = 256
SLOT_ROWS = 8
RUN_ALIGN = 8
LOCAL_ROWS = -(-(TM * TOP_K + N_EXPERTS * (RUN_ALIGN - 1)) // LANES) * LANES
NEG = -1e30
LOG2E = 1.4426950408889634
NT_DIMS = (((1,), (1,)), ((), ()))
VMEM_LIMIT = 56 * 1024 * 1024

f32 = jnp.float32
bf16 = jnp.bfloat16


def _seg_of_tile(i, tiles_per_seq, lat_tiles):
    return jnp.where(i < lat_tiles, i // tiles_per_seq, CTX_MOD_ROW)


def _adaln_kernel(c_ref, w_ref, b_ref, o_ref):
    c = c_ref[...]
    s = c * jax.nn.sigmoid(c)
    o_ref[...] = jnp.dot(s, w_ref[...], precision=lax.Precision.HIGHEST,
                         preferred_element_type=f32) + b_ref[...]


def _adaln(c_all, w_ada, b_ada):
    tn = 1536
    return pl.pallas_call(
        _adaln_kernel,
        grid=(DEPTH, 6 * D_MODEL // tn),
        in_specs=[
            pl.BlockSpec((MOD_ROWS, D_MODEL), lambda l, j: (0, 0)),
            pl.BlockSpec((None, D_MODEL, tn), lambda l, j: (l, 0, j)),
            pl.BlockSpec((None, 1, tn), lambda l, j: (l, 0, j)),
        ],
        out_specs=pl.BlockSpec((None, MOD_ROWS, tn), lambda l, j: (l, 0, j)),
        out_shape=jax.ShapeDtypeStruct((DEPTH, MOD_ROWS, 6 * D_MODEL), f32),
        compiler_params=pltpu.CompilerParams(
            dimension_semantics=("arbitrary", "arbitrary"), vmem_limit_bytes=VMEM_LIMIT),
        name="adaln",
    )(c_all, w_ada, b_ada.reshape(DEPTH, 1, 6 * D_MODEL))


def _rms(x, g):
    return x * lax.rsqrt(jnp.mean(x * x, axis=-1, keepdims=True) + EPS) * g


def _qkv_kernel(x_ref, mod_ref, g_ref, w_ref, cos_ref, sa_ref, sb_ref, qn_ref, kn_ref,
                q_ref, k_ref, v_ref, *, nq, nk, hd, qk_norm, v_ones):
    h = _rms(x_ref[...], g_ref[...]) * (1.0 + mod_ref[1:2, :]) + mod_ref[0:1, :]
    qkv = jnp.dot(h.astype(bf16), w_ref[...], preferred_element_type=f32)
    cos, sa, sb = cos_ref[...], sa_ref[...], sb_ref[...]
    quarter = hd // 4
    scale = hd ** -0.5 * LOG2E

    def rope(c):
        return c * cos + pltpu.roll(c, quarter, 1) * sa + pltpu.roll(c, LANES - quarter, 1) * sb

    for j in range(nq // LANES):
        c = qkv[:, j * LANES:(j + 1) * LANES]
        if qk_norm:
            c = _rms(c, qn_ref[...])
        q_ref[:, j * LANES:(j + 1) * LANES] = (rope(c) * scale).astype(bf16)
    for j in range(nk // LANES):
        c = qkv[:, nq + j * LANES:nq + (j + 1) * LANES]
        if qk_norm:
            c = _rms(c, kn_ref[...])
        k_ref[:, j * LANES:(j + 1) * LANES] = rope(c).astype(bf16)
    if v_ones:
        for j in range(nk // LANES):
            v_ref[:, 2 * j * LANES:(2 * j + 1) * LANES] = (
                qkv[:, nq + nk + j * LANES:nq + nk + (j + 1) * LANES].astype(bf16))
            v_ref[:, (2 * j + 1) * LANES:(2 * j + 2) * LANES] = jnp.ones((TM, LANES), bf16)
    else:
        v_ref[...] = qkv[:, nq + nk:].astype(bf16)


def _qkv(xs, mods, layer, g, w_bf16, tabs, qn, kn, *, hq, hkv, hd, qk_norm, v_ones):
    nq, nk = hq * hd, hkv * hd
    nv = 2 * nk if v_ones else nk
    seg = functools.partial(_seg_of_tile, tiles_per_seq=TILES_PER_SEQ, lat_tiles=LAT_TILES)
    tab_idx = lambda i: (jnp.where(i < LAT_TILES, i % TILES_PER_SEQ, TILES_PER_SEQ), 0)
    tab_spec = pl.BlockSpec((TM, LANES), tab_idx)
    row = lambda n: pl.BlockSpec((TM, n), lambda i: (i, 0))
    return pl.pallas_call(
        functools.partial(_qkv_kernel, nq=nq, nk=nk, hd=hd, qk_norm=qk_norm, v_ones=v_ones),
        grid=(ALL_TILES,),
        in_specs=[
            row(D_MODEL),
            pl.BlockSpec((None, None, 6, D_MODEL), lambda i: (layer, seg(i), 0, 0)),
            pl.BlockSpec((1, D_MODEL), lambda i: (0, 0)),
            pl.BlockSpec((D_MODEL, nq + 2 * nk), lambda i: (0, 0)),
            tab_spec, tab_spec, tab_spec,
            pl.BlockSpec((1, LANES), lambda i: (0, 0)),
            pl.BlockSpec((1, LANES), lambda i: (0, 0)),
        ],
        out_specs=[row(nq), row(nk), row(nv)],
        out_shape=[jax.ShapeDtypeStruct((T_ALL, nq), bf16),
                   jax.ShapeDtypeStruct((T_ALL, nk), bf16),
                   jax.ShapeDtypeStruct((T_ALL, nv), bf16)],
        compiler_params=pltpu.CompilerParams(
            dimension_semantics=("arbitrary",), vmem_limit_bytes=VMEM_LIMIT),
        name=f"qkv{layer}",
    )(xs, mods, g, w_bf16, *tabs, qn, kn)


def _rope_tables(hd):
    quarter = hd // 4
    inv_freq = jnp.float32(ROPE_BASE) ** (-jnp.arange(quarter, dtype=f32) / quarter)
    t = jnp.arange(SEQ)
    ang_r = (t // GRID_W).astype(f32)[:, None] * inv_freq[None, :]
    ang_c = (t % GRID_W).astype(f32)[:, None] * inv_freq[None, :]
    z = jnp.zeros_like(ang_r)
    cos = jnp.concatenate([jnp.cos(ang_r)] * 2 + [jnp.cos(ang_c)] * 2, axis=-1)
    sa = jnp.concatenate([z, jnp.sin(ang_r), z, jnp.sin(ang_c)], axis=-1)
    sb = jnp.concatenate([-jnp.sin(ang_r), z, -jnp.sin(ang_c), z], axis=-1)
    rep = LANES // hd
    pad = lambda a, v: jnp.concatenate(
        [jnp.tile(a, (1, rep)), jnp.full((TM, LANES), v, f32)], axis=0)
    return pad(cos, 1.0), pad(sa, 0.0), pad(sb, 0.0)


def _pair_operand(x, g):
    lane = lax.broadcasted_iota(jnp.int32, x.shape, 1)
    swapped = pltpu.roll(x, HD_A, 1)
    lo_src, hi_src = (x, swapped) if g == 0 else (swapped, x)
    lo = jnp.where(lane < HD_A, lo_src, 0.0)
    hi = jnp.where(lane >= HD_A, hi_src, 0.0)
    return jnp.concatenate([lo, hi], axis=0)


def _attend_pairs(q_ref, o_ref, sink_ref, kcat, vcat, mask):
    n = kcat.shape[0]
    pairs = HQ_A // HKV_A // 2
    rows = pairs * BLOCK
    lane = lax.broadcasted_iota(jnp.int32, (rows, LANES), 1)
    pair_of_row = lax.broadcasted_iota(jnp.int32, (rows, 1), 0) // BLOCK
    ind_row = lax.broadcasted_iota(jnp.int32, (2 * n, LANES), 0)
    ind_lane = lax.broadcasted_iota(jnp.int32, (2 * n, LANES), 1)
    ind = jnp.where(ind_lane == ind_row // n, 1.0, 0.0)
    if mask is not None:
        mask = jnp.concatenate([mask] * pairs, axis=0)
    for g in range(HKV_A):
        kp = _pair_operand(kcat, g).astype(bf16)
        vp = jnp.concatenate([_pair_operand(vcat, g), ind], axis=1).astype(bf16)
        cols = [(g * pairs + p) * LANES for p in range(pairs)]
        qs = jnp.concatenate([q_ref[:, c:c + LANES] for c in cols], axis=0)
        s = lax.dot_general(qs, kp, NT_DIMS, preferred_element_type=f32)
        es, ms, sks = [], [], []
        for hh in range(2):
            sh = s[:, hh * n:(hh + 1) * n]
            if mask is not None:
                sh = jnp.where(mask, sh, NEG)
            sk = jnp.zeros((rows, 1), f32)
            for p in range(pairs):
                sk = jnp.where(pair_of_row == p, sink_ref[cols[p] // HD_A + hh] * LOG2E, sk)
            m = jnp.maximum(jnp.max(sh, axis=-1, keepdims=True), sk)
            es.append(jnp.exp2(sh - m).astype(bf16))
            ms.append(m)
            sks.append(sk)
        oe = jnp.dot(jnp.concatenate(es, axis=1), vp, preferred_element_type=f32)
        inv = [1.0 / (oe[:, LANES + hh:LANES + hh + 1] + jnp.exp2(sks[hh] - ms[hh])) for hh in range(2)]
        o = oe[:, :LANES] * jnp.where(lane < HD_A, inv[0], inv[1])
        for p in range(pairs):
            o_ref[:, cols[p]:cols[p] + LANES] = o[p * BLOCK:(p + 1) * BLOCK].astype(bf16)


def _attn_a_kernel(sink_ref, q_ref, k_ref, v_ref, kc_ref, vc_ref, o_ref):
    j = pl.program_id(1)
    nblk = SEQ // BLOCK
    kc = kc_ref[...].astype(f32)
    vc = vc_ref[...].astype(f32)

    @pl.when(j < nblk)
    def _():
        wlen = 3 * BLOCK
        s0 = pl.multiple_of(jnp.clip((j - 1) * BLOCK, 0, SEQ - wlen), BLOCK)
        kcat = jnp.concatenate([k_ref[pl.ds(s0, wlen), :].astype(f32), kc], axis=0)
        vcat = jnp.concatenate([v_ref[pl.ds(s0, wlen), :].astype(f32), vc], axis=0)
        qpos = j * BLOCK + lax.broadcasted_iota(jnp.int32, (BLOCK, wlen + CTX_LEN), 0)
        col = lax.broadcasted_iota(jnp.int32, (BLOCK, wlen + CTX_LEN), 1)
        mask = (jnp.abs(qpos - (s0 + col)) <= WINDOW) | (col >= wlen)
        _attend_pairs(q_ref, o_ref, sink_ref, kcat, vcat, mask)

    @pl.when(j >= nblk)
    def _():
        _attend_pairs(q_ref, o_ref, sink_ref, kc, vc, None)


def _attn_a(q, k, v, sink):
    nblk = SEQ // BLOCK
    cblk = CTX_LEN // BLOCK
    nk = HKV_A * HD_A
    qrow = lambda b, j, s: (jnp.where(j < nblk, b * nblk + j, T_LAT // BLOCK + b * cblk + (j - nblk)), 0)
    lat_kv = pl.BlockSpec((SEQ, nk), lambda b, j, s: (b, 0))
    ctx_kv = pl.BlockSpec((CTX_LEN, nk), lambda b, j, s: (T_LAT // CTX_LEN + b, 0))
    return pl.pallas_call(
        _attn_a_kernel,
        grid_spec=pltpu.PrefetchScalarGridSpec(
            num_scalar_prefetch=1,
            grid=(BATCH, nblk + cblk),
            in_specs=[pl.BlockSpec((BLOCK, HQ_A * HD_A), qrow), lat_kv, lat_kv, ctx_kv, ctx_kv],
            out_specs=pl.BlockSpec((BLOCK, HQ_A * HD_A), qrow),
        ),
        out_shape=jax.ShapeDtypeStruct((T_ALL, HQ_A * HD_A), bf16),
        compiler_params=pltpu.CompilerParams(
            dimension_semantics=("arbitrary", "arbitrary"), vmem_limit_bytes=VMEM_LIMIT),
        name="attn_a",
    )(sink, q, k, v, k, v)


def _attn_b_kernel(q_ref, k_ref, v_ref, kc_ref, vc_ref, o_ref):
    rep = HQ_B // HKV_B
    for g in range(HKV_B):
        gs = slice(g * HD_B, (g + 1) * HD_B)
        vs = slice(2 * g * HD_B, 2 * (g + 1) * HD_B)
        qs = jnp.concatenate(
            [q_ref[:, (g * rep + r) * HD_B:(g * rep + r + 1) * HD_B] for r in range(rep)], axis=0)
        s1 = lax.dot_general(qs, k_ref[:, gs], NT_DIMS, preferred_element_type=f32)
        s2 = lax.dot_general(qs, kc_ref[:, gs], NT_DIMS, preferred_element_type=f32)
        m = jnp.maximum(jnp.max(s1, axis=-1, keepdims=True), jnp.max(s2, axis=-1, keepdims=True))
        e1 = jnp.exp2(s1 - m).astype(bf16)
        e2 = jnp.exp2(s2 - m).astype(bf16)
        oe = (jnp.dot(e1, v_ref[:, vs], preferred_element_type=f32)
              + jnp.dot(e2, vc_ref[:, vs], preferred_element_type=f32))
        o = oe[:, :HD_B] * (1.0 / oe[:, HD_B:HD_B + 1])
        for r in range(rep):
            o_ref[:, (g * rep + r) * HD_B:(g * rep + r + 1) * HD_B] = (
                o[r * BLOCK:(r + 1) * BLOCK].astype(bf16))


def _attn_b(q, k, v):
    nblk = SEQ // BLOCK
    nk = HKV_B * HD_B
    qrow = lambda b, j: (b * nblk + j, 0)
    lat = lambda n: pl.BlockSpec((SEQ, n), lambda b, j: (b, 0))
    ctx = lambda n: pl.BlockSpec((CTX_LEN, n), lambda b, j: (T_LAT // CTX_LEN + b, 0))
    return pl.pallas_call(
        _attn_b_kernel,
        grid=(BATCH, nblk),
        in_specs=[pl.BlockSpec((BLOCK, HQ_B * HD_B), qrow), lat(nk), lat(2 * nk), ctx(nk), ctx(2 * nk)],
        out_specs=pl.BlockSpec((BLOCK, HQ_B * HD_B), qrow),
        out_shape=jax.ShapeDtypeStruct((T_LAT, HQ_B * HD_B), bf16),
        compiler_params=pltpu.CompilerParams(
            dimension_semantics=("arbitrary", "arbitrary"), vmem_limit_bytes=VMEM_LIMIT),
        name="attn_b",
    )(q, k, v, k, v)


def _post_attn_kernel(o_ref, wo_ref, x_ref, mod_ref, g1_ref, g2_ref, wrt_ref, brt_ref,
                      xn_ref, f_ref, tit_ref, ti_ref, tw_ref, cnt_ref):
    a = jnp.dot(o_ref[...], wo_ref[...], preferred_element_type=f32)
    x = x_ref[...] + mod_ref[2:3, :] * _rms(a, g1_ref[...])
    xn_ref[...] = x
    f = _rms(x, g2_ref[...]) * (1.0 + mod_ref[4:5, :]) + mod_ref[3:4, :]
    f_ref[...] = f.astype(bf16)
    logits = lax.dot_general(wrt_ref[...], f, NT_DIMS, precision=lax.Precision.HIGHEST,
                             preferred_element_type=f32) + brt_ref[...]
    sub = lax.broadcasted_iota(jnp.int32, logits.shape, 0)
    rest = logits
    top_v, top_i = [], []
    for _ in range(TOP_K):
        m = jnp.max(rest, axis=0, keepdims=True)
        idx = jnp.min(jnp.where(rest == m, sub, N_EXPERTS), axis=0, keepdims=True)
        top_v.append(m)
        top_i.append(idx)
        rest = jnp.where(sub == idx, -jnp.inf, rest)
    es = [jnp.exp(v - top_v[0]) for v in top_v]
    inv = 1.0 / (es[0] + es[1] + es[2] + es[3])
    k8 = lax.broadcasted_iota(jnp.int32, (SLOT_ROWS, TM), 0)
    tit = jnp.full((SLOT_ROWS, TM), -1.0, f32)
    twt = jnp.zeros((SLOT_ROWS, TM), f32)
    for k in range(TOP_K):
        tit = jnp.where(k8 == k, top_i[k].astype(f32), tit)
        twt = jnp.where(k8 == k, es[k] * inv, twt)
    tit_ref[...] = tit
    eye = (lax.broadcasted_iota(jnp.int32, (TM, TM), 0)
           == lax.broadcasted_iota(jnp.int32, (TM, TM), 1)).astype(f32)
    to_rows = lambda t: lax.dot_general(eye, t, NT_DIMS, precision=lax.Precision.HIGHEST,
                                        preferred_element_type=f32)
    ti = to_rows(tit)
    ti_ref[...] = ti
    tw_ref[...] = to_rows(twt)
    lane = lax.broadcasted_iota(jnp.int32, (TM, N_EXPERTS), 1).astype(f32)
    cnt = jnp.zeros((TM, N_EXPERTS), f32)
    for k in range(TOP_K):
        cnt = cnt + (ti[:, k:k + 1] == lane).astype(f32)
    cnt_ref[...] = jnp.sum(cnt, axis=0, keepdims=True)


def _post_attn(o, wo_bf16, xs, mods, layer, g1, g2, wr, br, n_tiles):
    seg = functools.partial(_seg_of_tile, tiles_per_seq=TILES_PER_SEQ, lat_tiles=LAT_TILES)
    rows = n_tiles * TM
    row = lambda n: pl.BlockSpec((TM, n), lambda i: (i, 0))
    const = lambda a, b: pl.BlockSpec((a, b), lambda i: (0, 0))
    return pl.pallas_call(
        _post_attn_kernel,
        grid=(n_tiles,),
        in_specs=[
            row(D_MODEL), const(D_MODEL, D_MODEL), row(D_MODEL),
            pl.BlockSpec((None, None, 6, D_MODEL), lambda i: (layer, seg(i), 0, 0)),
            const(1, D_MODEL), const(1, D_MODEL), const(N_EXPERTS, D_MODEL), const(N_EXPERTS, 1),
        ],
        out_specs=[row(D_MODEL), row(D_MODEL),
                   pl.BlockSpec((SLOT_ROWS, TM), lambda i: (0, i)),
                   row(SLOT_ROWS), row(SLOT_ROWS),
                   pl.BlockSpec((None, 1, N_EXPERTS), lambda i: (i, 0, 0))],
        out_shape=[jax.ShapeDtypeStruct((rows, D_MODEL), f32),
                   jax.ShapeDtypeStruct((rows, D_MODEL), bf16),
                   jax.ShapeDtypeStruct((SLOT_ROWS, rows), f32),
                   jax.ShapeDtypeStruct((rows, SLOT_ROWS), f32),
                   jax.ShapeDtypeStruct((rows, SLOT_ROWS), f32),
                   jax.ShapeDtypeStruct((n_tiles, 1, N_EXPERTS), f32)],
        compiler_params=pltpu.CompilerParams(
            dimension_semantics=("arbitrary",), vmem_limit_bytes=VMEM_LIMIT),
        name=f"post_attn{layer}",
    )(o, wo_bf16, xs, mods, g1, g2, wr.T, br.reshape(N_EXPERTS, 1))


def _ffn_tiles_max(n_tiles):
    rows = n_tiles * (TM * TOP_K + N_EXPERTS * (RUN_ALIGN - 1)) + N_EXPERTS * (TME - RUN_ALIGN)
    return -(-rows // TME)


def _route_tables(cnt, n_tiles_max):
    n = cnt.reshape(-1, N_EXPERTS).astype(jnp.int32)
    run = (n + RUN_ALIGN - 1) // RUN_ALIGN * RUN_ALIGN
    loff = jnp.cumsum(run, axis=1) - run
    tot = jnp.sum(run, axis=0)
    gsz = (tot + TME - 1) // TME * TME
    ends = jnp.cumsum(gsz)
    goff = (ends - gsz)[None, :] + jnp.cumsum(run, axis=0) - run
    tab = jnp.concatenate([loff, goff, run // RUN_ALIGN], axis=1)[:, None, :]
    n_active = ends[-1] // TME
    tail = jnp.concatenate([ends - gsz + tot, (gsz - tot) // RUN_ALIGN, n_active[None]])[None, :]
    tile = jnp.arange(n_tiles_max, dtype=jnp.int32)
    te = jnp.sum((jnp.minimum(tile, n_active - 1) * TME)[:, None] >= ends[None, :], axis=1)
    loff_f = loff.astype(f32)
    return (tab, tail, loff_f[:, None, :], loff_f[:, :, None], te.astype(jnp.int32),
            n_active.reshape(1).astype(jnp.int32))


def _for_each_run(tab_ref, fn):
    for e in range(N_EXPERTS):
        def body(c, carry, e=e):
            fn(e, c)
            return carry
        lax.fori_loop(0, tab_ref[0, 2 * N_EXPERTS + e], body, 0)


def _chunk(base, c):
    return pl.ds(pl.multiple_of(base + c * RUN_ALIGN, RUN_ALIGN), RUN_ALIGN)


def _dispatch_kernel(tab_ref, tab_prev_ref, tail_ref, f_ref, tit_ref, loffc_ref, xs_hbm,
                     lbuf, zbuf, sem, zsem):
    i = pl.program_id(0)
    tit = tit_ref[...]
    sub = lax.broadcasted_iota(jnp.int32, (N_EXPERTS, TM), 0).astype(f32)
    hit = [tit[k:k + 1, :] == sub for k in range(TOP_K)]
    cnt = sum(h.astype(f32) for h in hit)
    before = (lax.broadcasted_iota(jnp.int32, (TM, TM), 0)
              < lax.broadcasted_iota(jnp.int32, (TM, TM), 1)).astype(bf16)
    base = loffc_ref[...] + jnp.dot(cnt.astype(bf16), before, preferred_element_type=f32)
    rows = lax.broadcasted_iota(jnp.int32, (LOCAL_ROWS, TM), 0).astype(f32)
    onehot = jnp.zeros((LOCAL_ROWS, TM), f32)
    for k in range(TOP_K):
        lp = jnp.sum(jnp.where(hit[k], base, 0.0), axis=0, keepdims=True)
        onehot = onehot + (rows == lp).astype(f32)
    slot = i % 2
    lbuf[slot] = jnp.dot(onehot.astype(bf16), f_ref[...], preferred_element_type=f32)

    def run_copy(tab, s):
        return lambda e, c: pltpu.make_async_copy(
            lbuf.at[s, _chunk(tab[0, e], c)], xs_hbm.at[_chunk(tab[0, N_EXPERTS + e], c)], sem.at[s])

    _for_each_run(tab_ref, lambda e, c: run_copy(tab_ref, slot)(e, c).start())

    @pl.when(i > 0)
    def _():
        _for_each_run(tab_prev_ref, lambda e, c: run_copy(tab_prev_ref, 1 - slot)(e, c).wait())

    tail_copy = lambda e, c: pltpu.make_async_copy(zbuf, xs_hbm.at[_chunk(tail_ref[0, e], c)], zsem)

    def for_each_tail(fn):
        for e in range(N_EXPERTS):
            def body(c, carry, e=e):
                fn(e, c)
                return carry
            lax.fori_loop(0, tail_ref[0, N_EXPERTS + e], body, 0)

    @pl.when(i == pl.num_programs(0) - 1)
    def _():
        zbuf[...] = jnp.zeros_like(zbuf)
        for_each_tail(lambda e, c: tail_copy(e, c).start())
        for_each_tail(lambda e, c: tail_copy(e, c).wait())
        _for_each_run(tab_ref, lambda e, c: run_copy(tab_ref, slot)(e, c).wait())
        lbuf[slot, 0:TME, :] = jnp.zeros((TME, D_MODEL), f32)
        n_active = tail_ref[0, 2 * N_EXPERTS]
        n_spare = xs_hbm.shape[0] // TME - n_active
        spare_copy = lambda j: pltpu.make_async_copy(
            lbuf.at[slot, 0:TME], xs_hbm.at[pl.ds(pl.multiple_of((n_active + j) * TME, TME), TME)], zsem)
        lax.fori_loop(0, n_spare, lambda j, carry: (spare_copy(j).start(), carry)[1], 0)
        lax.fori_loop(0, n_spare, lambda j, carry: (spare_copy(j).wait(), carry)[1], 0)


def _dispatch(f, tit, tab, tail, loffc, layer, n_tiles, n_tiles_max):
    smem = functools.partial(pl.BlockSpec, memory_space=pltpu.SMEM)
    return pl.pallas_call(
        _dispatch_kernel,
        grid=(n_tiles,),
        in_specs=[
            smem((None, 1, 3 * N_EXPERTS), lambda i: (i, 0, 0)),
            smem((None, 1, 3 * N_EXPERTS), lambda i: (jnp.maximum(i - 1, 0), 0, 0)),
            smem((1, 2 * N_EXPERTS + 1), lambda i: (0, 0)),
            pl.BlockSpec((TM, D_MODEL), lambda i: (i, 0)),
            pl.BlockSpec((SLOT_ROWS, TM), lambda i: (0, i)),
            pl.BlockSpec((None, N_EXPERTS, 1), lambda i: (i, 0, 0)),
        ],
        out_specs=pl.BlockSpec(memory_space=pl.ANY),
        out_shape=jax.ShapeDtypeStruct((n_tiles_max * TME, D_MODEL), f32),
        scratch_shapes=[pltpu.VMEM((2, LOCAL_ROWS, D_MODEL), f32), pltpu.VMEM((RUN_ALIGN, D_MODEL), f32),
                        pltpu.SemaphoreType.DMA((2,)), pltpu.SemaphoreType.DMA(())],
        compiler_params=pltpu.CompilerParams(
            dimension_semantics=("arbitrary",), vmem_limit_bytes=VMEM_LIMIT),
        name=f"dispatch{layer}",
    )(tab, tab, tail, f, tit, loffc)


def _moe_ffn_kernel(te_ref, na_ref, x_ref, w1_ref, b1_ref, w2_ref, b2_ref, y_ref, w1b, w2b):
    i = pl.program_id(0)
    na = na_ref[0]

    @pl.when(i >= na)
    def _():
        y_ref[...] = jnp.zeros_like(y_ref)

    @pl.when(i < na)
    def _():
        @pl.when((i == 0) | (te_ref[i] != te_ref[jnp.maximum(i - 1, 0)]))
        def _():
            w1b[...] = w1_ref[...].astype(bf16)
            w2b[...] = w2_ref[...].astype(bf16)

        u = jnp.dot(x_ref[...].astype(bf16), w1b[...], preferred_element_type=f32) + b1_ref[...]
        glu = jnp.minimum(u[:, :D_FF], SWIGLU_LIMIT)
        lin = jnp.clip(u[:, D_FF:], -SWIGLU_LIMIT, SWIGLU_LIMIT)
        act = glu * jax.nn.sigmoid(SWIGLU_ALPHA * glu) * (lin + 1.0)
        y_ref[...] = jnp.dot(act.astype(bf16), w2b[...], preferred_element_type=f32) + b2_ref[...]


def _moe_ffn(xs_sorted, te, na, w1, b1, w2, b2, layer, n_tiles_max):
    ex = lambda i, te, na: (layer, te[i], 0, 0)
    return pl.pallas_call(
        _moe_ffn_kernel,
        grid_spec=pltpu.PrefetchScalarGridSpec(
            num_scalar_prefetch=2,
            grid=(n_tiles_max,),
            in_specs=[
                pl.BlockSpec((TME, D_MODEL), lambda i, te, na: (jnp.maximum(jnp.minimum(i, na[0] - 1), 0), 0)),
                pl.BlockSpec((None, None, D_MODEL, 2 * D_FF), ex),
                pl.BlockSpec((None, None, 1, 2 * D_FF), ex),
                pl.BlockSpec((None, None, D_FF, D_MODEL), ex),
                pl.BlockSpec((None, None, 1, D_MODEL), ex),
            ],
            out_specs=pl.BlockSpec((TME, D_MODEL), lambda i, te, na: (i, 0)),
            scratch_shapes=[pltpu.VMEM((D_MODEL, 2 * D_FF), bf16), pltpu.VMEM((D_FF, D_MODEL), bf16)],
        ),
        out_shape=jax.ShapeDtypeStruct((n_tiles_max * TME, D_MODEL), f32),
        compiler_params=pltpu.CompilerParams(
            dimension_semantics=("arbitrary",), vmem_limit_bytes=VMEM_LIMIT),
        name=f"moe_ffn{layer}",
    )(te, na, xs_sorted, w1, b1.reshape(DEPTH, N_EXPERTS, 1, 2 * D_FF), w2,
      b2.reshape(DEPTH, N_EXPERTS, 1, D_MODEL))


def _combine_kernel(tab_ref, tab_next_ref, y_hbm, ti_ref, tw_ref, loffr_ref, x_ref, mod_ref, g_ref,
                    xn_ref, ybuf, sem):
    i = pl.program_id(0)
    slot = i % 2

    def run_copy(tab, s):
        return lambda e, c: pltpu.make_async_copy(
            y_hbm.at[_chunk(tab[0, N_EXPERTS + e], c)], ybuf.at[s, _chunk(tab[0, e], c)], sem.at[s])

    @pl.when(i == 0)
    def _():
        ybuf[...] = jnp.zeros_like(ybuf)
        _for_each_run(tab_ref, lambda e, c: run_copy(tab_ref, 0)(e, c).start())

    @pl.when(i + 1 < pl.num_programs(0))
    def _():
        _for_each_run(tab_next_ref, lambda e, c: run_copy(tab_next_ref, 1 - slot)(e, c).start())

    _for_each_run(tab_ref, lambda e, c: run_copy(tab_ref, slot)(e, c).wait())

    ti = ti_ref[...]
    tw = tw_ref[...]
    lane = lax.broadcasted_iota(jnp.int32, (TM, N_EXPERTS), 1).astype(f32)
    hit = [ti[:, k:k + 1] == lane for k in range(TOP_K)]
    cnt = sum(h.astype(f32) for h in hit)
    before = (lax.broadcasted_iota(jnp.int32, (TM, TM), 1)
              < lax.broadcasted_iota(jnp.int32, (TM, TM), 0)).astype(bf16)
    base = loffr_ref[...] + jnp.dot(before, cnt.astype(bf16), preferred_element_type=f32)
    cols = lax.broadcasted_iota(jnp.int32, (TM, LOCAL_ROWS), 1).astype(f32)
    w = jnp.zeros((TM, LOCAL_ROWS), f32)
    for k in range(TOP_K):
        lp = jnp.sum(jnp.where(hit[k], base, 0.0), axis=1, keepdims=True)
        w = jnp.where(cols == lp, tw[:, k:k + 1], w)
    w_hi = w.astype(bf16)
    w_lo = (w - w_hi.astype(f32)).astype(bf16)
    yl = ybuf[slot].astype(bf16)
    acc = (jnp.dot(w_hi, yl, preferred_element_type=f32)
           + jnp.dot(w_lo, yl, preferred_element_type=f32))
    xn_ref[...] = x_ref[...] + mod_ref[5:6, :] * _rms(acc, g_ref[...])


def _combine(y, tab, ti, tw, loffr, xs, mods, layer, g, n_tiles):
    seg = functools.partial(_seg_of_tile, tiles_per_seq=TILES_PER_SEQ, lat_tiles=LAT_TILES)
    row = lambda n: pl.BlockSpec((TM, n), lambda i: (i, 0))
    smem = functools.partial(pl.BlockSpec, memory_space=pltpu.SMEM)
    return pl.pallas_call(
        _combine_kernel,
        grid=(n_tiles,),
        in_specs=[
            smem((None, 1, 3 * N_EXPERTS), lambda i: (i, 0, 0)),
            smem((None, 1, 3 * N_EXPERTS), lambda i: (jnp.minimum(i + 1, n_tiles - 1), 0, 0)),
            pl.BlockSpec(memory_space=pl.ANY),
            row(SLOT_ROWS), row(SLOT_ROWS),
            pl.BlockSpec((None, 1, N_EXPERTS), lambda i: (i, 0, 0)),
            row(D_MODEL),
            pl.BlockSpec((None, None, 6, D_MODEL), lambda i: (layer, seg(i), 0, 0)),
            pl.BlockSpec((1, D_MODEL), lambda i: (0, 0)),
        ],
        out_specs=row(D_MODEL),
        out_shape=jax.ShapeDtypeStruct((n_tiles * TM, D_MODEL), f32),
        scratch_shapes=[pltpu.VMEM((2, LOCAL_ROWS, D_MODEL), f32), pltpu.SemaphoreType.DMA((2,))],
        compiler_params=pltpu.CompilerParams(
            dimension_semantics=("arbitrary",), vmem_limit_bytes=VMEM_LIMIT),
        name=f"combine{layer}",
    )(tab, tab, y, ti, tw, loffr, xs, mods, g)


def _moe(f, tit, ti, tw, cnt, xs, mods, layer, g, w1, b1, w2, b2, n_tiles):
    n_tiles_max = _ffn_tiles_max(n_tiles)
    tab, tail, loffr, loffc, te, na = _route_tables(cnt, n_tiles_max)
    xs_sorted = _dispatch(f, tit, tab, tail, loffc, layer, n_tiles, n_tiles_max)
    y = _moe_ffn(xs_sorted, te, na, w1, b1, w2, b2, layer, n_tiles_max)
    return _combine(y, tab, ti, tw, loffr, xs, mods, layer, g, n_tiles)


def kernel(x, c, ctx, c_ctx, w_ada, b_ada, norm_g, a_w_qkv, a_w_o, a_sink, b_w_qkv, b_q_norm, b_k_norm,
           b_w_o, moe_w_router, moe_b_router, moe_w1, moe_b1, moe_w2, moe_b2):
    assert DEPTH == 2 and x.shape == (BATCH, SEQ, D_MODEL) and ctx.shape == (BATCH, CTX_LEN, D_MODEL)
    xs = jnp.concatenate([x.reshape(T_LAT, D_MODEL), ctx.reshape(T_CTX, D_MODEL)], axis=0)
    c_all = jnp.concatenate(
        [c, c_ctx[None, :], jnp.zeros((MOD_ROWS - BATCH - 1, D_MODEL), f32)], axis=0)
    mods = _adaln(c_all, w_ada, b_ada).reshape(DEPTH, MOD_ROWS, 6, D_MODEL)
    ones = jnp.ones((1, LANES), f32)
    g = lambda i, j: norm_g[i, j][None, :]

    q, k, v = _qkv(xs, mods, 0, g(0, 0), a_w_qkv[0].astype(bf16), _rope_tables(HD_A), ones, ones,
                   hq=HQ_A, hkv=HKV_A, hd=HD_A, qk_norm=False, v_ones=False)
    o = _attn_a(q, k, v, a_sink[0])
    xs, *routed = _post_attn(o, a_w_o[0].astype(bf16), xs, mods, 0, g(0, 1), g(0, 2),
                             moe_w_router[0], moe_b_router[0], ALL_TILES)
    xs = _moe(*routed, xs, mods, 0, g(0, 3), moe_w1, moe_b1, moe_w2, moe_b2, ALL_TILES)

    q, k, v = _qkv(xs, mods, 1, g(1, 0), b_w_qkv[0].astype(bf16), _rope_tables(HD_B),
                   b_q_norm[0][None, :], b_k_norm[0][None, :],
                   hq=HQ_B, hkv=HKV_B, hd=HD_B, qk_norm=True, v_ones=True)
    o = _attn_b(q, k, v)
    xl, *routed = _post_attn(o, b_w_o[0].astype(bf16), xs, mods, 1, g(1, 1), g(1, 2),
                             moe_w_router[1], moe_b_router[1], LAT_TILES)
    xl = _moe(*routed, xl, mods, 1, g(1, 3), moe_w1, moe_b1, moe_w2, moe_b2, LAT_TILES)
    return xl.reshape(BATCH, SEQ, D_MODEL)
```

```python
import functools

import jax
import jax.numpy as jnp
from jax import lax
from jax.experimental import pallas as pl
from jax.experimental.pallas import tpu as pltpu

D_MODEL = 1024
BATCH = 8
SEQ = 2048
DEPTH = 2
GRID_W = 64
CTX_LEN = 256
BLOCK = 128
WINDOW = 128
ROPE_BASE = 10000.0
EPS = 1e-6
HQ_A, HKV_A, HD_A = 16, 2, 64
HQ_B, HKV_B, HD_B = 8, 2, 128
N_EXPERTS = 32
TOP_K = 4
D_FF = D_MODEL
SWIGLU_LIMIT = 7.0
SWIGLU_ALPHA = 1.702

T_LAT = BATCH * SEQ
T_CTX = BATCH * CTX_LEN
T_ALL = T_LAT + T_CTX
LANES = 128
TM = 256
LAT_TILES = T_LAT // TM
ALL_TILES = T_ALL // TM
TILES_PER_SEQ = SEQ // TM
MOD_ROWS = 16
CTX_MOD_ROW = BATCH
TMP = 512
TME = 256
SLOT_ROWS = 8
RUN_ALIGN = 8
LOCAL_ROWS = -(-(TM * TOP_K + N_EXPERTS * (RUN_ALIGN - 1)) // LANES) * LANES
N_CHUNKS = LOCAL_ROWS // RUN_ALIGN
MIN_CHUNKS = TM * TOP_K // RUN_ALIGN
NEG = -1e30
LOG2E = 1.4426950408889634
NT_DIMS = (((1,), (1,)), ((), ()))
VMEM_LIMIT = 56 * 1024 * 1024

f32 = jnp.float32
bf16 = jnp.bfloat16


def _seg_of_tile(i, tiles_per_seq, lat_tiles):
    return jnp.where(i < lat_tiles, i // tiles_per_seq, CTX_MOD_ROW)


def _adaln_kernel(c_ref, w_ref, b_ref, o_ref):
    c = c_ref[...]
    s = c * jax.nn.sigmoid(c)
    o_ref[...] = jnp.dot(s, w_ref[...], precision=lax.Precision.HIGHEST,
                         preferred_element_type=f32) + b_ref[...]


def _adaln(c_all, w_ada, b_ada):
    tn = 1536
    return pl.pallas_call(
        _adaln_kernel,
        grid=(DEPTH, 6 * D_MODEL // tn),
        in_specs=[
            pl.BlockSpec((MOD_ROWS, D_MODEL), lambda l, j: (0, 0)),
            pl.BlockSpec((None, D_MODEL, tn), lambda l, j: (l, 0, j)),
            pl.BlockSpec((None, 1, tn), lambda l, j: (l, 0, j)),
        ],
        out_specs=pl.BlockSpec((None, MOD_ROWS, tn), lambda l, j: (l, 0, j)),
        out_shape=jax.ShapeDtypeStruct((DEPTH, MOD_ROWS, 6 * D_MODEL), f32),
        compiler_params=pltpu.CompilerParams(
            dimension_semantics=("arbitrary", "arbitrary"), vmem_limit_bytes=VMEM_LIMIT),
        name="adaln",
    )(c_all, w_ada, b_ada.reshape(DEPTH, 1, 6 * D_MODEL))


def _rms(x, g):
    return x * lax.rsqrt(jnp.mean(x * x, axis=-1, keepdims=True) + EPS) * g


def _qkv_kernel(x_ref, mod_ref, g_ref, w_ref, cos_ref, sa_ref, sb_ref, qn_ref, kn_ref,
                q_ref, k_ref, v_ref, *, nq, nk, hd, qk_norm, v_ones):
    h = _rms(x_ref[...], g_ref[...]) * (1.0 + mod_ref[1:2, :]) + mod_ref[0:1, :]
    qkv = jnp.dot(h.astype(bf16), w_ref[...], preferred_element_type=f32)
    cos, sa, sb = cos_ref[...], sa_ref[...], sb_ref[...]
    quarter = hd // 4
    scale = hd ** -0.5 * LOG2E

    def rope(c):
        return c * cos + pltpu.roll(c, quarter, 1) * sa + pltpu.roll(c, LANES - quarter, 1) * sb

    for j in range(nq // LANES):
        c = qkv[:, j * LANES:(j + 1) * LANES]
        if qk_norm:
            c = _rms(c, qn_ref[...])
        q_ref[:, j * LANES:(j + 1) * LANES] = (rope(c) * scale).astype(bf16)
    for j in range(nk // LANES):
        c = qkv[:, nq + j * LANES:nq + (j + 1) * LANES]
        if qk_norm:
            c = _rms(c, kn_ref[...])
        k_ref[:, j * LANES:(j + 1) * LANES] = rope(c).astype(bf16)
    if v_ones:
        for j in range(nk // LANES):
            v_ref[:, 2 * j * LANES:(2 * j + 1) * LANES] = (
                qkv[:, nq + nk + j * LANES:nq + nk + (j + 1) * LANES].astype(bf16))
            v_ref[:, (2 * j + 1) * LANES:(2 * j + 2) * LANES] = jnp.ones((x_ref.shape[0], LANES), bf16)
    else:
        v_ref[...] = qkv[:, nq + nk:].astype(bf16)


def _qkv(xs, mods, layer, g, w_bf16, tabs, qn, kn, *, hq, hkv, hd, qk_norm, v_ones, tm):
    nq, nk = hq * hd, hkv * hd
    nv = 2 * nk if v_ones else nk
    per_seq, lat_tiles = SEQ // tm, T_LAT // tm
    seg = functools.partial(_seg_of_tile, tiles_per_seq=per_seq, lat_tiles=lat_tiles)
    tab_idx = lambda i: (jnp.where(i < lat_tiles, i % per_seq, per_seq), 0)
    tab_spec = pl.BlockSpec((tm, LANES), tab_idx)
    row = lambda n: pl.BlockSpec((tm, n), lambda i: (i, 0))
    return pl.pallas_call(
        functools.partial(_qkv_kernel, nq=nq, nk=nk, hd=hd, qk_norm=qk_norm, v_ones=v_ones),
        grid=(T_ALL // tm,),
        in_specs=[
            row(D_MODEL),
            pl.BlockSpec((None, None, 6, D_MODEL), lambda i: (layer, seg(i), 0, 0)),
            pl.BlockSpec((1, D_MODEL), lambda i: (0, 0)),
            pl.BlockSpec((D_MODEL, nq + 2 * nk), lambda i: (0, 0)),
            tab_spec, tab_spec, tab_spec,
            pl.BlockSpec((1, LANES), lambda i: (0, 0)),
            pl.BlockSpec((1, LANES), lambda i: (0, 0)),
        ],
        out_specs=[row(nq), row(nk), row(nv)],
        out_shape=[jax.ShapeDtypeStruct((T_ALL, nq), bf16),
                   jax.ShapeDtypeStruct((T_ALL, nk), bf16),
                   jax.ShapeDtypeStruct((T_ALL, nv), bf16)],
        compiler_params=pltpu.CompilerParams(
            dimension_semantics=("arbitrary",), vmem_limit_bytes=VMEM_LIMIT),
        name=f"qkv{layer}",
    )(xs, mods, g, w_bf16, *tabs, qn, kn)


def _rope_tables(hd):
    quarter = hd // 4
    inv_freq = jnp.float32(ROPE_BASE) ** (-jnp.arange(quarter, dtype=f32) / quarter)
    t = jnp.arange(SEQ)
    ang_r = (t // GRID_W).astype(f32)[:, None] * inv_freq[None, :]
    ang_c = (t % GRID_W).astype(f32)[:, None] * inv_freq[None, :]
    z = jnp.zeros_like(ang_r)
    cos = jnp.concatenate([jnp.cos(ang_r)] * 2 + [jnp.cos(ang_c)] * 2, axis=-1)
    sa = jnp.concatenate([z, jnp.sin(ang_r), z, jnp.sin(ang_c)], axis=-1)
    sb = jnp.concatenate([-jnp.sin(ang_r), z, -jnp.sin(ang_c), z], axis=-1)
    rep = LANES // hd
    pad = lambda a, v: jnp.concatenate(
        [jnp.tile(a, (1, rep)), jnp.full((TMP, LANES), v, f32)], axis=0)
    return pad(cos, 1.0), pad(sa, 0.0), pad(sb, 0.0)


def _pair_operand(x, g):
    lane = lax.broadcasted_iota(jnp.int32, x.shape, 1)
    swapped = pltpu.roll(x, HD_A, 1)
    lo_src, hi_src = (x, swapped) if g == 0 else (swapped, x)
    lo = jnp.where(lane < HD_A, lo_src, 0.0)
    hi = jnp.where(lane >= HD_A, hi_src, 0.0)
    return jnp.concatenate([lo, hi], axis=0)


def _attend_pairs(q_ref, o_ref, sink_ref, kcat, vcat, mask):
    n = kcat.shape[0]
    pairs = HQ_A // HKV_A // 2
    rows = pairs * BLOCK
    lane = lax.broadcasted_iota(jnp.int32, (rows, LANES), 1)
    pair_of_row = lax.broadcasted_iota(jnp.int32, (rows, 1), 0) // BLOCK
    ind_row = lax.broadcasted_iota(jnp.int32, (2 * n, LANES), 0)
    ind_lane = lax.broadcasted_iota(jnp.int32, (2 * n, LANES), 1)
    ind = jnp.where(ind_lane == ind_row // n, 1.0, 0.0)
    if mask is not None:
        mask = jnp.concatenate([mask] * pairs, axis=0)
    for g in range(HKV_A):
        kp = _pair_operand(kcat, g).astype(bf16)
        vp = jnp.concatenate([_pair_operand(vcat, g), ind], axis=1).astype(bf16)
        cols = [(g * pairs + p) * LANES for p in range(pairs)]
        qs = jnp.concatenate([q_ref[:, c:c + LANES] for c in cols], axis=0)
        s = lax.dot_general(qs, kp, NT_DIMS, preferred_element_type=f32)
        es, ms, sks = [], [], []
        for hh in range(2):
            sh = s[:, hh * n:(hh + 1) * n]
            if mask is not None:
                sh = jnp.where(mask, sh, NEG)
            sk = jnp.zeros((rows, 1), f32)
            for p in range(pairs):
                sk = jnp.where(pair_of_row == p, sink_ref[cols[p] // HD_A + hh] * LOG2E, sk)
            m = jnp.maximum(jnp.max(sh, axis=-1, keepdims=True), sk)
            es.append(jnp.exp2(sh - m).astype(bf16))
            ms.append(m)
            sks.append(sk)
        oe = jnp.dot(jnp.concatenate(es, axis=1), vp, preferred_element_type=f32)
        inv = [1.0 / (oe[:, LANES + hh:LANES + hh + 1] + jnp.exp2(sks[hh] - ms[hh])) for hh in range(2)]
        o = oe[:, :LANES] * jnp.where(lane < HD_A, inv[0], inv[1])
        for p in range(pairs):
            o_ref[:, cols[p]:cols[p] + LANES] = o[p * BLOCK:(p + 1) * BLOCK].astype(bf16)


def _attn_a_kernel(sink_ref, q_ref, k_ref, v_ref, kc_ref, vc_ref, o_ref):
    j = pl.program_id(1)
    nblk = SEQ // BLOCK
    kc = kc_ref[...].astype(f32)
    vc = vc_ref[...].astype(f32)

    @pl.when(j < nblk)
    def _():
        wlen = 3 * BLOCK
        s0 = pl.multiple_of(jnp.clip((j - 1) * BLOCK, 0, SEQ - wlen), BLOCK)
        kcat = jnp.concatenate([k_ref[pl.ds(s0, wlen), :].astype(f32), kc], axis=0)
        vcat = jnp.concatenate([v_ref[pl.ds(s0, wlen), :].astype(f32), vc], axis=0)
        qpos = j * BLOCK + lax.broadcasted_iota(jnp.int32, (BLOCK, wlen + CTX_LEN), 0)
        col = lax.broadcasted_iota(jnp.int32, (BLOCK, wlen + CTX_LEN), 1)
        mask = (jnp.abs(qpos - (s0 + col)) <= WINDOW) | (col >= wlen)
        _attend_pairs(q_ref, o_ref, sink_ref, kcat, vcat, mask)

    @pl.when(j >= nblk)
    def _():
        _attend_pairs(q_ref, o_ref, sink_ref, kc, vc, None)


def _attn_a(q, k, v, sink):
    nblk = SEQ // BLOCK
    cblk = CTX_LEN // BLOCK
    nk = HKV_A * HD_A
    qrow = lambda b, j, s: (jnp.where(j < nblk, b * nblk + j, T_LAT // BLOCK + b * cblk + (j - nblk)), 0)
    lat_kv = pl.BlockSpec((SEQ, nk), lambda b, j, s: (b, 0))
    ctx_kv = pl.BlockSpec((CTX_LEN, nk), lambda b, j, s: (T_LAT // CTX_LEN + b, 0))
    return pl.pallas_call(
        _attn_a_kernel,
        grid_spec=pltpu.PrefetchScalarGridSpec(
            num_scalar_prefetch=1,
            grid=(BATCH, nblk + cblk),
            in_specs=[pl.BlockSpec((BLOCK, HQ_A * HD_A), qrow), lat_kv, lat_kv, ctx_kv, ctx_kv],
            out_specs=pl.BlockSpec((BLOCK, HQ_A * HD_A), qrow),
        ),
        out_shape=jax.ShapeDtypeStruct((T_ALL, HQ_A * HD_A), bf16),
        compiler_params=pltpu.CompilerParams(
            dimension_semantics=("arbitrary", "arbitrary"), vmem_limit_bytes=VMEM_LIMIT),
        name="attn_a",
    )(sink, q, k, v, k, v)


def _attn_b_kernel(q_ref, k_ref, v_ref, kc_ref, vc_ref, o_ref):
    rep = HQ_B // HKV_B
    for g in range(HKV_B):
        gs = slice(g * HD_B, (g + 1) * HD_B)
        vs = slice(2 * g * HD_B, 2 * (g + 1) * HD_B)
        qs = jnp.concatenate(
            [q_ref[:, (g * rep + r) * HD_B:(g * rep + r + 1) * HD_B] for r in range(rep)], axis=0)
        s1 = lax.dot_general(qs, k_ref[:, gs], NT_DIMS, preferred_element_type=f32)
        s2 = lax.dot_general(qs, kc_ref[:, gs], NT_DIMS, preferred_element_type=f32)
        m = jnp.maximum(jnp.max(s1, axis=-1, keepdims=True), jnp.max(s2, axis=-1, keepdims=True))
        e1 = jnp.exp2(s1 - m).astype(bf16)
        e2 = jnp.exp2(s2 - m).astype(bf16)
        oe = (jnp.dot(e1, v_ref[:, vs], preferred_element_type=f32)
              + jnp.dot(e2, vc_ref[:, vs], preferred_element_type=f32))
        o = oe[:, :HD_B] * (1.0 / oe[:, HD_B:HD_B + 1])
        for r in range(rep):
            o_ref[:, (g * rep + r) * HD_B:(g * rep + r + 1) * HD_B] = (
                o[r * BLOCK:(r + 1) * BLOCK].astype(bf16))


def _attn_b(q, k, v):
    nblk = SEQ // BLOCK
    nk = HKV_B * HD_B
    qrow = lambda b, j: (b * nblk + j, 0)
    lat = lambda n: pl.BlockSpec((SEQ, n), lambda b, j: (b, 0))
    ctx = lambda n: pl.BlockSpec((CTX_LEN, n), lambda b, j: (T_LAT // CTX_LEN + b, 0))
    return pl.pallas_call(
        _attn_b_kernel,
        grid=(BATCH, nblk),
        in_specs=[pl.BlockSpec((BLOCK, HQ_B * HD_B), qrow), lat(nk), lat(2 * nk), ctx(nk), ctx(2 * nk)],
        out_specs=pl.BlockSpec((BLOCK, HQ_B * HD_B), qrow),
        out_shape=jax.ShapeDtypeStruct((T_LAT, HQ_B * HD_B), bf16),
        compiler_params=pltpu.CompilerParams(
            dimension_semantics=("arbitrary", "arbitrary"), vmem_limit_bytes=VMEM_LIMIT),
        name="attn_b",
    )(q, k, v, k, v)


def _post_attn_kernel(o_ref, wo_ref, x_ref, mod_ref, g1_ref, g2_ref, wrt_ref, brt_ref,
                      xn_ref, f_ref, tit_ref, ti_ref, tw_ref, cnt_ref):
    a = jnp.dot(o_ref[...], wo_ref[...], preferred_element_type=f32)
    x = x_ref[...] + mod_ref[2:3, :] * _rms(a, g1_ref[...])
    xn_ref[...] = x
    f = _rms(x, g2_ref[...]) * (1.0 + mod_ref[4:5, :]) + mod_ref[3:4, :]
    f_ref[...] = f.astype(bf16)
    logits = lax.dot_general(wrt_ref[...], f, NT_DIMS, precision=lax.Precision.HIGHEST,
                             preferred_element_type=f32) + brt_ref[...]
    sub = lax.broadcasted_iota(jnp.int32, logits.shape, 0)
    rest = logits
    top_v, top_i = [], []
    for _ in range(TOP_K):
        m = jnp.max(rest, axis=0, keepdims=True)
        idx = jnp.min(jnp.where(rest == m, sub, N_EXPERTS), axis=0, keepdims=True)
        top_v.append(m)
        top_i.append(idx)
        rest = jnp.where(sub == idx, -jnp.inf, rest)
    es = [jnp.exp(v - top_v[0]) for v in top_v]
    inv = 1.0 / (es[0] + es[1] + es[2] + es[3])
    k8 = lax.broadcasted_iota(jnp.int32, (SLOT_ROWS, TMP), 0)
    tit = jnp.full((SLOT_ROWS, TMP), -1.0, f32)
    twt = jnp.zeros((SLOT_ROWS, TMP), f32)
    for k in range(TOP_K):
        tit = jnp.where(k8 == k, top_i[k].astype(f32), tit)
        twt = jnp.where(k8 == k, es[k] * inv, twt)
    tit_ref[...] = tit
    eye = (lax.broadcasted_iota(jnp.int32, (TM, TM), 0)
           == lax.broadcasted_iota(jnp.int32, (TM, TM), 1)).astype(f32)
    to_rows = lambda t: lax.dot_general(eye, t, NT_DIMS, precision=lax.Precision.HIGHEST,
                                        preferred_element_type=f32)
    lane = lax.broadcasted_iota(jnp.int32, (TM, N_EXPERTS), 1).astype(f32)
    for h in range(TMP // TM):
        cols = slice(h * TM, (h + 1) * TM)
        ti = to_rows(tit[:, cols])
        ti_ref[cols, :] = ti
        tw_ref[cols, :] = to_rows(twt[:, cols])
        cnt = jnp.zeros((TM, N_EXPERTS), f32)
        for k in range(TOP_K):
            cnt = cnt + (ti[:, k:k + 1] == lane).astype(f32)
        cnt_ref[h] = jnp.sum(cnt, axis=0, keepdims=True)


def _post_attn(o, wo_bf16, xs, mods, layer, g1, g2, wr, br, n_tiles):
    seg = functools.partial(_seg_of_tile, tiles_per_seq=SEQ // TMP, lat_tiles=T_LAT // TMP)
    rows = n_tiles * TM
    row = lambda n: pl.BlockSpec((TMP, n), lambda i: (i, 0))
    const = lambda a, b: pl.BlockSpec((a, b), lambda i: (0, 0))
    return pl.pallas_call(
        _post_attn_kernel,
        grid=(rows // TMP,),
        in_specs=[
            row(D_MODEL), const(D_MODEL, D_MODEL), row(D_MODEL),
            pl.BlockSpec((None, None, 6, D_MODEL), lambda i: (layer, seg(i), 0, 0)),
            const(1, D_MODEL), const(1, D_MODEL), const(N_EXPERTS, D_MODEL), const(N_EXPERTS, 1),
        ],
        out_specs=[row(D_MODEL), row(D_MODEL),
                   pl.BlockSpec((SLOT_ROWS, TMP), lambda i: (0, i)),
                   row(SLOT_ROWS), row(SLOT_ROWS),
                   pl.BlockSpec((TMP // TM, 1, N_EXPERTS), lambda i: (i, 0, 0))],
        out_shape=[jax.ShapeDtypeStruct((rows, D_MODEL), f32),
                   jax.ShapeDtypeStruct((rows, D_MODEL), bf16),
                   jax.ShapeDtypeStruct((SLOT_ROWS, rows), f32),
                   jax.ShapeDtypeStruct((rows, SLOT_ROWS), f32),
                   jax.ShapeDtypeStruct((rows, SLOT_ROWS), f32),
                   jax.ShapeDtypeStruct((n_tiles, 1, N_EXPERTS), f32)],
        compiler_params=pltpu.CompilerParams(
            dimension_semantics=("arbitrary",), vmem_limit_bytes=VMEM_LIMIT),
        name=f"post_attn{layer}",
    )(o, wo_bf16, xs, mods, g1, g2, wr.T, br.reshape(N_EXPERTS, 1))


def _ffn_tiles_max(n_tiles):
    rows = n_tiles * (TM * TOP_K + N_EXPERTS * (RUN_ALIGN - 1)) + N_EXPERTS * (TME - RUN_ALIGN)
    return -(-rows // TME)


def _route_tables(cnt, n_tiles_max):
    n = cnt.reshape(-1, N_EXPERTS).astype(jnp.int32)
    run = (n + RUN_ALIGN - 1) // RUN_ALIGN * RUN_ALIGN
    loff = jnp.cumsum(run, axis=1) - run
    tot = jnp.sum(run, axis=0)
    gsz = (tot + TME - 1) // TME * TME
    ends = jnp.cumsum(gsz)
    goff = (ends - gsz)[None, :] + jnp.cumsum(run, axis=0) - run
    c0 = jnp.arange(N_CHUNKS, dtype=jnp.int32) * RUN_ALIGN
    owner = jnp.sum(c0[None, :, None] >= (loff + run)[:, None, :], axis=2)
    shift = jnp.where(owner[:, :, None] == jnp.arange(N_EXPERTS)[None, None, :],
                      (goff - loff)[:, None, :], 0)
    tab = jnp.concatenate([jnp.sum(shift, axis=2) + c0[None, :],
                           jnp.sum(run, axis=1, keepdims=True) // RUN_ALIGN], axis=1)[:, None, :]
    n_active = ends[-1] // TME
    tail = jnp.concatenate([ends - gsz + tot, (gsz - tot) // RUN_ALIGN, n_active[None]])[None, :]
    tile = jnp.arange(n_tiles_max, dtype=jnp.int32)
    te = jnp.sum((jnp.minimum(tile, n_active - 1) * TME)[:, None] >= ends[None, :], axis=1)
    loff_f = loff.astype(f32)
    return (tab, tail, loff_f[:, None, :], loff_f[:, :, None], te.astype(jnp.int32),
            n_active.reshape(1).astype(jnp.int32))


def _local_chunk(c):
    if isinstance(c, int):
        return pl.ds(c * RUN_ALIGN, RUN_ALIGN)
    return pl.ds(pl.multiple_of(c * RUN_ALIGN, RUN_ALIGN), RUN_ALIGN)


def _global_chunk(ctab_ref, c):
    return pl.ds(pl.multiple_of(ctab_ref[0, c], RUN_ALIGN), RUN_ALIGN)


def _start_chunks(ctab_ref, copy):
    for c in range(MIN_CHUNKS):
        copy(c).start()
    lax.fori_loop(MIN_CHUNKS, ctab_ref[0, N_CHUNKS], lambda c, carry: (copy(c).start(), carry)[1], 0)


def _wait_chunks(ctab_ref, copy, bulk):
    bulk.wait()
    lax.fori_loop(MIN_CHUNKS, ctab_ref[0, N_CHUNKS], lambda c, carry: (copy(c).wait(), carry)[1], 0)


def _chunk(base, c):
    return pl.ds(pl.multiple_of(base + c * RUN_ALIGN, RUN_ALIGN), RUN_ALIGN)


def _dispatch_kernel(ctab_ref, ctab_prev_ref, tail_ref, f_ref, tit_ref, loffc_ref, xs_hbm,
                     lbuf, zbuf, sem, zsem):
    i = pl.program_id(0)
    tit = tit_ref[...]
    sub = lax.broadcasted_iota(jnp.int32, (N_EXPERTS, TM), 0).astype(f32)
    hit = [tit[k:k + 1, :] == sub for k in range(TOP_K)]
    cnt = sum(h.astype(f32) for h in hit)
    before = (lax.broadcasted_iota(jnp.int32, (TM, TM), 0)
              < lax.broadcasted_iota(jnp.int32, (TM, TM), 1)).astype(bf16)
    base = loffc_ref[...] + jnp.dot(cnt.astype(bf16), before, preferred_element_type=f32)
    rows = lax.broadcasted_iota(jnp.int32, (LOCAL_ROWS, TM), 0).astype(f32)
    onehot = jnp.zeros((LOCAL_ROWS, TM), f32)
    for k in range(TOP_K):
        lp = jnp.sum(jnp.where(hit[k], base, 0.0), axis=0, keepdims=True)
        onehot = onehot + (rows == lp).astype(f32)
    slot = i % 2
    lbuf[slot] = jnp.dot(onehot.astype(bf16), f_ref[...], preferred_element_type=f32)

    def chunk_copy(ctab, s):
        return lambda c: pltpu.make_async_copy(
            lbuf.at[s, _local_chunk(c)], xs_hbm.at[_global_chunk(ctab, c)], sem.at[s])

    def bulk_copy(s):
        rows = MIN_CHUNKS * RUN_ALIGN
        return pltpu.make_async_copy(lbuf.at[s, 0:rows], xs_hbm.at[0:rows], sem.at[s])

    _start_chunks(ctab_ref, chunk_copy(ctab_ref, slot))

    @pl.when(i > 0)
    def _():
        _wait_chunks(ctab_prev_ref, chunk_copy(ctab_prev_ref, 1 - slot), bulk_copy(1 - slot))

    tail_copy = lambda e, c: pltpu.make_async_copy(zbuf, xs_hbm.at[_chunk(tail_ref[0, e], c)], zsem)

    def for_each_tail(fn):
        for e in range(N_EXPERTS):
            def body(c, carry, e=e):
                fn(e, c)
                return carry
            lax.fori_loop(0, tail_ref[0, N_EXPERTS + e], body, 0)

    @pl.when(i == pl.num_programs(0) - 1)
    def _():
        zbuf[...] = jnp.zeros_like(zbuf)
        for_each_tail(lambda e, c: tail_copy(e, c).start())
        for_each_tail(lambda e, c: tail_copy(e, c).wait())
        _wait_chunks(ctab_ref, chunk_copy(ctab_ref, slot), bulk_copy(slot))
        lbuf[slot, 0:TME, :] = jnp.zeros((TME, D_MODEL), f32)
        n_active = tail_ref[0, 2 * N_EXPERTS]
        n_spare = xs_hbm.shape[0] // TME - n_active
        spare_copy = lambda j: pltpu.make_async_copy(
            lbuf.at[slot, 0:TME], xs_hbm.at[pl.ds(pl.multiple_of((n_active + j) * TME, TME), TME)], zsem)
        lax.fori_loop(0, n_spare, lambda j, carry: (spare_copy(j).start(), carry)[1], 0)
        lax.fori_loop(0, n_spare, lambda j, carry: (spare_copy(j).wait(), carry)[1], 0)


def _dispatch(f, tit, tab, tail, loffc, layer, n_tiles, n_tiles_max):
    smem = functools.partial(pl.BlockSpec, memory_space=pltpu.SMEM)
    return pl.pallas_call(
        _dispatch_kernel,
        grid=(n_tiles,),
        in_specs=[
            smem((None, 1, N_CHUNKS + 1), lambda i: (i, 0, 0)),
            smem((None, 1, N_CHUNKS + 1), lambda i: (jnp.maximum(i - 1, 0), 0, 0)),
            smem((1, 2 * N_EXPERTS + 1), lambda i: (0, 0)),
            pl.BlockSpec((TM, D_MODEL), lambda i: (i, 0)),
            pl.BlockSpec((SLOT_ROWS, TM), lambda i: (0, i)),
            pl.BlockSpec((None, N_EXPERTS, 1), lambda i: (i, 0, 0)),
        ],
        out_specs=pl.BlockSpec(memory_space=pl.ANY),
        out_shape=jax.ShapeDtypeStruct((n_tiles_max * TME, D_MODEL), f32),
        scratch_shapes=[pltpu.VMEM((2, LOCAL_ROWS, D_MODEL), f32), pltpu.VMEM((RUN_ALIGN, D_MODEL), f32),
                        pltpu.SemaphoreType.DMA((2,)), pltpu.SemaphoreType.DMA(())],
        compiler_params=pltpu.CompilerParams(
            dimension_semantics=("arbitrary",), vmem_limit_bytes=VMEM_LIMIT),
        name=f"dispatch{layer}",
    )(tab, tab, tail, f, tit, loffc)


def _moe_ffn_kernel(te_ref, na_ref, x_ref, w1_ref, b1_ref, w2_ref, b2_ref, y_ref, w1b, w2b):
    i = pl.program_id(0)
    na = na_ref[0]

    @pl.when(i >= na)
    def _():
        y_ref[...] = jnp.zeros_like(y_ref)

    @pl.when(i < na)
    def _():
        @pl.when((i == 0) | (te_ref[i] != te_ref[jnp.maximum(i - 1, 0)]))
        def _():
            w1b[...] = w1_ref[...].astype(bf16)
            w2b[...] = w2_ref[...].astype(bf16)

        u = jnp.dot(x_ref[...].astype(bf16), w1b[...], preferred_element_type=f32) + b1_ref[...]
        glu = jnp.minimum(u[:, :D_FF], SWIGLU_LIMIT)
        lin = jnp.clip(u[:, D_FF:], -SWIGLU_LIMIT, SWIGLU_LIMIT)
        act = glu * jax.nn.sigmoid(SWIGLU_ALPHA * glu) * (lin + 1.0)
        y_ref[...] = jnp.dot(act.astype(bf16), w2b[...], preferred_element_type=f32) + b2_ref[...]


def _moe_ffn(xs_sorted, te, na, w1, b1, w2, b2, layer, n_tiles_max):
    ex = lambda i, te, na: (layer, te[i], 0, 0)
    return pl.pallas_call(
        _moe_ffn_kernel,
        grid_spec=pltpu.PrefetchScalarGridSpec(
            num_scalar_prefetch=2,
            grid=(n_tiles_max,),
            in_specs=[
                pl.BlockSpec((TME, D_MODEL), lambda i, te, na: (jnp.maximum(jnp.minimum(i, na[0] - 1), 0), 0)),
                pl.BlockSpec((None, None, D_MODEL, 2 * D_FF), ex),
                pl.BlockSpec((None, None, 1, 2 * D_FF), ex),
                pl.BlockSpec((None, None, D_FF, D_MODEL), ex),
                pl.BlockSpec((None, None, 1, D_MODEL), ex),
            ],
            out_specs=pl.BlockSpec((TME, D_MODEL), lambda i, te, na: (i, 0)),
            scratch_shapes=[pltpu.VMEM((D_MODEL, 2 * D_FF), bf16), pltpu.VMEM((D_FF, D_MODEL), bf16)],
        ),
        out_shape=jax.ShapeDtypeStruct((n_tiles_max * TME, D_MODEL), f32),
        compiler_params=pltpu.CompilerParams(
            dimension_semantics=("arbitrary",), vmem_limit_bytes=VMEM_LIMIT),
        name=f"moe_ffn{layer}",
    )(te, na, xs_sorted, w1, b1.reshape(DEPTH, N_EXPERTS, 1, 2 * D_FF), w2,
      b2.reshape(DEPTH, N_EXPERTS, 1, D_MODEL))


def _combine_kernel(ctab_ref, ctab_next_ref, y_hbm, ti_ref, tw_ref, loffr_ref, x_ref, mod_ref, g_ref,
                    xn_ref, ybuf, sem):
    i = pl.program_id(0)
    slot = i % 2

    def chunk_copy(ctab, s):
        return lambda c: pltpu.make_async_copy(
            y_hbm.at[_global_chunk(ctab, c)], ybuf.at[s, _local_chunk(c)], sem.at[s])

    def bulk_copy(s):
        rows = MIN_CHUNKS * RUN_ALIGN
        return pltpu.make_async_copy(y_hbm.at[0:rows], ybuf.at[s, 0:rows], sem.at[s])

    @pl.when(i == 0)
    def _():
        ybuf[...] = jnp.zeros_like(ybuf)
        _start_chunks(ctab_ref, chunk_copy(ctab_ref, 0))

    @pl.when(i + 1 < pl.num_programs(0))
    def _():
        _start_chunks(ctab_next_ref, chunk_copy(ctab_next_ref, 1 - slot))

    _wait_chunks(ctab_ref, chunk_copy(ctab_ref, slot), bulk_copy(slot))

    ti = ti_ref[...]
    tw = tw_ref[...]
    lane = lax.broadcasted_iota(jnp.int32, (TM, N_EXPERTS), 1).astype(f32)
    hit = [ti[:, k:k + 1] == lane for k in range(TOP_K)]
    cnt = sum(h.astype(f32) for h in hit)
    before = (lax.broadcasted_iota(jnp.int32, (TM, TM), 1)
              < lax.broadcasted_iota(jnp.int32, (TM, TM), 0)).astype(bf16)
    base = loffr_ref[...] + jnp.dot(before, cnt.astype(bf16), preferred_element_type=f32)
    cols = lax.broadcasted_iota(jnp.int32, (TM, LOCAL_ROWS), 1).astype(f32)
    w = jnp.zeros((TM, LOCAL_ROWS), f32)
    for k in range(TOP_K):
        lp = jnp.sum(jnp.where(hit[k], base, 0.0), axis=1, keepdims=True)
        w = jnp.where(cols == lp, tw[:, k:k + 1], w)
    w_hi = w.astype(bf16)
    w_lo = (w - w_hi.astype(f32)).astype(bf16)
    yl = ybuf[slot].astype(bf16)
    acc = (jnp.dot(w_hi, yl, preferred_element_type=f32)
           + jnp.dot(w_lo, yl, preferred_element_type=f32))
    xn_ref[...] = x_ref[...] + mod_ref[5:6, :] * _rms(acc, g_ref[...])


def _combine(y, tab, ti, tw, loffr, xs, mods, layer, g, n_tiles):
    seg = functools.partial(_seg_of_tile, tiles_per_seq=TILES_PER_SEQ, lat_tiles=LAT_TILES)
    row = lambda n: pl.BlockSpec((TM, n), lambda i: (i, 0))
    smem = functools.partial(pl.BlockSpec, memory_space=pltpu.SMEM)
    return pl.pallas_call(
        _combine_kernel,
        grid=(n_tiles,),
        in_specs=[
            smem((None, 1, N_CHUNKS + 1), lambda i: (i, 0, 0)),
            smem((None, 1, N_CHUNKS + 1), lambda i: (jnp.minimum(i + 1, n_tiles - 1), 0, 0)),
            pl.BlockSpec(memory_space=pl.ANY),
            row(SLOT_ROWS), row(SLOT_ROWS),
            pl.BlockSpec((None, 1, N_EXPERTS), lambda i: (i, 0, 0)),
            row(D_MODEL),
            pl.BlockSpec((None, None, 6, D_MODEL), lambda i: (layer, seg(i), 0, 0)),
            pl.BlockSpec((1, D_MODEL), lambda i: (0, 0)),
        ],
        out_specs=row(D_MODEL),
        out_shape=jax.ShapeDtypeStruct((n_tiles * TM, D_MODEL), f32),
        scratch_shapes=[pltpu.VMEM((2, LOCAL_ROWS, D_MODEL), f32), pltpu.SemaphoreType.DMA((2,))],
        compiler_params=pltpu.CompilerParams(
            dimension_semantics=("arbitrary",), vmem_limit_bytes=VMEM_LIMIT),
        name=f"combine{layer}",
    )(tab, tab, y, ti, tw, loffr, xs, mods, g)


def _moe(f, tit, ti, tw, cnt, xs, mods, layer, g, w1, b1, w2, b2, n_tiles):
    n_tiles_max = _ffn_tiles_max(n_tiles)
    tab, tail, loffr, loffc, te, na = _route_tables(cnt, n_tiles_max)
    xs_sorted = _dispatch(f, tit, tab, tail, loffc, layer, n_tiles, n_tiles_max)
    y = _moe_ffn(xs_sorted, te, na, w1, b1, w2, b2, layer, n_tiles_max)
    return _combine(y, tab, ti, tw, loffr, xs, mods, layer, g, n_tiles)


def kernel(x, c, ctx, c_ctx, w_ada, b_ada, norm_g, a_w_qkv, a_w_o, a_sink, b_w_qkv, b_q_norm, b_k_norm,
           b_w_o, moe_w_router, moe_b_router, moe_w1, moe_b1, moe_w2, moe_b2):
    assert DEPTH == 2 and x.shape == (BATCH, SEQ, D_MODEL) and ctx.shape == (BATCH, CTX_LEN, D_MODEL)
    xs = jnp.concatenate([x.reshape(T_LAT, D_MODEL), ctx.reshape(T_CTX, D_MODEL)], axis=0)
    c_all = jnp.concatenate(
        [c, c_ctx[None, :], jnp.zeros((MOD_ROWS - BATCH - 1, D_MODEL), f32)], axis=0)
    mods = _adaln(c_all, w_ada, b_ada).reshape(DEPTH, MOD_ROWS, 6, D_MODEL)
    ones = jnp.ones((1, LANES), f32)
    g = lambda i, j: norm_g[i, j][None, :]

    q, k, v = _qkv(xs, mods, 0, g(0, 0), a_w_qkv[0].astype(bf16), _rope_tables(HD_A), ones, ones,
                   hq=HQ_A, hkv=HKV_A, hd=HD_A, qk_norm=False, v_ones=False, tm=TMP)
    o = _attn_a(q, k, v, a_sink[0])
    xs, *routed = _post_attn(o, a_w_o[0].astype(bf16), xs, mods, 0, g(0, 1), g(0, 2),
                             moe_w_router[0], moe_b_router[0], ALL_TILES)
    xs = _moe(*routed, xs, mods, 0, g(0, 3), moe_w1, moe_b1, moe_w2, moe_b2, ALL_TILES)

    q, k, v = _qkv(xs, mods, 1, g(1, 0), b_w_qkv[0].astype(bf16), _rope_tables(HD_B),
                   b_q_norm[0][None, :], b_k_norm[0][None, :],
                   hq=HQ_B, hkv=HKV_B, hd=HD_B, qk_norm=True, v_ones=True, tm=TM)
    o = _attn_b(q, k, v)
    xl, *routed = _post_attn(o, b_w_o[0].astype(bf16), xs, mods, 1, g(1, 1), g(1, 2),
                             moe_w_router[1], moe_b_router[1], LAT_TILES)
    xl = _moe(*routed, xl, mods, 1, g(1, 3), moe_w1, moe_b1, moe_w2, moe_b2, LAT_TILES)
    return xl.reshape(BATCH, SEQ, D_MODEL)
```

```python
import functools

import jax
import jax.numpy as jnp
from jax import lax
from jax.experimental import pallas as pl
from jax.experimental.pallas import tpu as pltpu

D_MODEL = 1024
BATCH = 8
SEQ = 2048
DEPTH = 2
GRID_W = 64
CTX_LEN = 256
BLOCK = 128
WINDOW = 128
ROPE_BASE = 10000.0
EPS = 1e-6
HQ_A, HKV_A, HD_A = 16, 2, 64
HQ_B, HKV_B, HD_B = 8, 2, 128
N_EXPERTS = 32
TOP_K = 4
D_FF = D_MODEL
SWIGLU_LIMIT = 7.0
SWIGLU_ALPHA = 1.702

T_LAT = BATCH * SEQ
T_CTX = BATCH * CTX_LEN
T_ALL = T_LAT + T_CTX
LANES = 128
TM = 256
LAT_TILES = T_LAT // TM
ALL_TILES = T_ALL // TM
TILES_PER_SEQ = SEQ // TM
MOD_ROWS = 16
CTX_MOD_ROW = BATCH
TMP = 512
BQ_B = 512
TME = 512
SLOT_ROWS = 8
RUN_ALIGN = 8
LOCAL_ROWS = -(-(TM * TOP_K + N_EXPERTS * (RUN_ALIGN - 1)) // LANES) * LANES
N_CHUNKS = LOCAL_ROWS // RUN_ALIGN
MIN_CHUNKS = TM * TOP_K // RUN_ALIGN
NEG = -1e30
LOG2E = 1.4426950408889634
NT_DIMS = (((1,), (1,)), ((), ()))
VMEM_LIMIT = 56 * 1024 * 1024

f32 = jnp.float32
bf16 = jnp.bfloat16


def _seg_of_tile(i, tiles_per_seq, lat_tiles):
    return jnp.where(i < lat_tiles, i // tiles_per_seq, CTX_MOD_ROW)


def _adaln_kernel(c_ref, w_ref, b_ref, o_ref):
    c = c_ref[...]
    s = c * jax.nn.sigmoid(c)
    o_ref[...] = jnp.dot(s, w_ref[...], precision=lax.Precision.HIGHEST,
                         preferred_element_type=f32) + b_ref[...]


def _adaln(c_all, w_ada, b_ada):
    tn = 1536
    return pl.pallas_call(
        _adaln_kernel,
        grid=(DEPTH, 6 * D_MODEL // tn),
        in_specs=[
            pl.BlockSpec((MOD_ROWS, D_MODEL), lambda l, j: (0, 0)),
            pl.BlockSpec((None, D_MODEL, tn), lambda l, j: (l, 0, j)),
            pl.BlockSpec((None, 1, tn), lambda l, j: (l, 0, j)),
        ],
        out_specs=pl.BlockSpec((None, MOD_ROWS, tn), lambda l, j: (l, 0, j)),
        out_shape=jax.ShapeDtypeStruct((DEPTH, MOD_ROWS, 6 * D_MODEL), f32),
        compiler_params=pltpu.CompilerParams(
            dimension_semantics=("arbitrary", "arbitrary"), vmem_limit_bytes=VMEM_LIMIT),
        name="adaln",
    )(c_all, w_ada, b_ada.reshape(DEPTH, 1, 6 * D_MODEL))


def _rms(x, g):
    return x * lax.rsqrt(jnp.mean(x * x, axis=-1, keepdims=True) + EPS) * g


def _qkv_kernel(x_ref, mod_ref, g_ref, w_ref, cos_ref, sa_ref, sb_ref, qn_ref, kn_ref,
                q_ref, k_ref, v_ref, *, nq, nk, hd, qk_norm, v_ones):
    h = _rms(x_ref[...], g_ref[...]) * (1.0 + mod_ref[1:2, :]) + mod_ref[0:1, :]
    qkv = jnp.dot(h.astype(bf16), w_ref[...], preferred_element_type=f32)
    cos, sa, sb = cos_ref[...], sa_ref[...], sb_ref[...]
    quarter = hd // 4
    scale = hd ** -0.5 * LOG2E

    def rope(c):
        return c * cos + pltpu.roll(c, quarter, 1) * sa + pltpu.roll(c, LANES - quarter, 1) * sb

    for j in range(nq // LANES):
        c = qkv[:, j * LANES:(j + 1) * LANES]
        if qk_norm:
            c = _rms(c, qn_ref[...])
        q_ref[:, j * LANES:(j + 1) * LANES] = (rope(c) * scale).astype(bf16)
    for j in range(nk // LANES):
        c = qkv[:, nq + j * LANES:nq + (j + 1) * LANES]
        if qk_norm:
            c = _rms(c, kn_ref[...])
        k_ref[:, j * LANES:(j + 1) * LANES] = rope(c).astype(bf16)
    if v_ones:
        for j in range(nk // LANES):
            v_ref[:, 2 * j * LANES:(2 * j + 1) * LANES] = (
                qkv[:, nq + nk + j * LANES:nq + nk + (j + 1) * LANES].astype(bf16))
            v_ref[:, (2 * j + 1) * LANES:(2 * j + 2) * LANES] = jnp.ones((x_ref.shape[0], LANES), bf16)
    else:
        v_ref[...] = qkv[:, nq + nk:].astype(bf16)


def _qkv(xs, mods, layer, g, w_bf16, tabs, qn, kn, *, hq, hkv, hd, qk_norm, v_ones, tm):
    nq, nk = hq * hd, hkv * hd
    nv = 2 * nk if v_ones else nk
    per_seq, lat_tiles = SEQ // tm, T_LAT // tm
    seg = functools.partial(_seg_of_tile, tiles_per_seq=per_seq, lat_tiles=lat_tiles)
    tab_idx = lambda i: (jnp.where(i < lat_tiles, i % per_seq, per_seq), 0)
    tab_spec = pl.BlockSpec((tm, LANES), tab_idx)
    row = lambda n: pl.BlockSpec((tm, n), lambda i: (i, 0))
    return pl.pallas_call(
        functools.partial(_qkv_kernel, nq=nq, nk=nk, hd=hd, qk_norm=qk_norm, v_ones=v_ones),
        grid=(T_ALL // tm,),
        in_specs=[
            row(D_MODEL),
            pl.BlockSpec((None, None, 6, D_MODEL), lambda i: (layer, seg(i), 0, 0)),
            pl.BlockSpec((1, D_MODEL), lambda i: (0, 0)),
            pl.BlockSpec((D_MODEL, nq + 2 * nk), lambda i: (0, 0)),
            tab_spec, tab_spec, tab_spec,
            pl.BlockSpec((1, LANES), lambda i: (0, 0)),
            pl.BlockSpec((1, LANES), lambda i: (0, 0)),
        ],
        out_specs=[row(nq), row(nk), row(nv)],
        out_shape=[jax.ShapeDtypeStruct((T_ALL, nq), bf16),
                   jax.ShapeDtypeStruct((T_ALL, nk), bf16),
                   jax.ShapeDtypeStruct((T_ALL, nv), bf16)],
        compiler_params=pltpu.CompilerParams(
            dimension_semantics=("arbitrary",), vmem_limit_bytes=VMEM_LIMIT),
        name=f"qkv{layer}",
    )(xs, mods, g, w_bf16, *tabs, qn, kn)


def _rope_tables(hd):
    quarter = hd // 4
    inv_freq = jnp.float32(ROPE_BASE) ** (-jnp.arange(quarter, dtype=f32) / quarter)
    t = jnp.arange(SEQ)
    ang_r = (t // GRID_W).astype(f32)[:, None] * inv_freq[None, :]
    ang_c = (t % GRID_W).astype(f32)[:, None] * inv_freq[None, :]
    z = jnp.zeros_like(ang_r)
    cos = jnp.concatenate([jnp.cos(ang_r)] * 2 + [jnp.cos(ang_c)] * 2, axis=-1)
    sa = jnp.concatenate([z, jnp.sin(ang_r), z, jnp.sin(ang_c)], axis=-1)
    sb = jnp.concatenate([-jnp.sin(ang_r), z, -jnp.sin(ang_c), z], axis=-1)
    rep = LANES // hd
    pad = lambda a, v: jnp.concatenate(
        [jnp.tile(a, (1, rep)), jnp.full((TMP, LANES), v, f32)], axis=0)
    return pad(cos, 1.0), pad(sa, 0.0), pad(sb, 0.0)


def _pair_operand(x, g):
    lane = lax.broadcasted_iota(jnp.int32, x.shape, 1)
    swapped = pltpu.roll(x, HD_A, 1)
    lo_src, hi_src = (x, swapped) if g == 0 else (swapped, x)
    lo = jnp.where(lane < HD_A, lo_src, 0.0)
    hi = jnp.where(lane >= HD_A, hi_src, 0.0)
    return jnp.concatenate([lo, hi], axis=0)


def _attend_pairs(q_ref, o_ref, sink_ref, kcat, vcat, mask):
    n = kcat.shape[0]
    pairs = HQ_A // HKV_A // 2
    rows = pairs * BLOCK
    lane = lax.broadcasted_iota(jnp.int32, (rows, LANES), 1)
    pair_of_row = lax.broadcasted_iota(jnp.int32, (rows, 1), 0) // BLOCK
    ind_row = lax.broadcasted_iota(jnp.int32, (2 * n, LANES), 0)
    ind_lane = lax.broadcasted_iota(jnp.int32, (2 * n, LANES), 1)
    ind = jnp.where(ind_lane == ind_row // n, 1.0, 0.0)
    if mask is not None:
        mask = jnp.concatenate([mask] * pairs, axis=0)
    for g in range(HKV_A):
        kp = _pair_operand(kcat, g).astype(bf16)
        vp = jnp.concatenate([_pair_operand(vcat, g), ind], axis=1).astype(bf16)
        cols = [(g * pairs + p) * LANES for p in range(pairs)]
        qs = jnp.concatenate([q_ref[:, c:c + LANES] for c in cols], axis=0)
        s = lax.dot_general(qs, kp, NT_DIMS, preferred_element_type=f32)
        es, ms, sks = [], [], []
        for hh in range(2):
            sh = s[:, hh * n:(hh + 1) * n]
            if mask is not None:
                sh = jnp.where(mask, sh, NEG)
            sk = jnp.zeros((rows, 1), f32)
            for p in range(pairs):
                sk = jnp.where(pair_of_row == p, sink_ref[cols[p] // HD_A + hh] * LOG2E, sk)
            m = jnp.maximum(jnp.max(sh, axis=-1, keepdims=True), sk)
            es.append(jnp.exp2(sh - m).astype(bf16))
            ms.append(m)
            sks.append(sk)
        oe = jnp.dot(jnp.concatenate(es, axis=1), vp, preferred_element_type=f32)
        inv = [1.0 / (oe[:, LANES + hh:LANES + hh + 1] + jnp.exp2(sks[hh] - ms[hh])) for hh in range(2)]
        o = oe[:, :LANES] * jnp.where(lane < HD_A, inv[0], inv[1])
        for p in range(pairs):
            o_ref[:, cols[p]:cols[p] + LANES] = o[p * BLOCK:(p + 1) * BLOCK].astype(bf16)


def _attn_a_kernel(sink_ref, q_ref, k_ref, v_ref, kc_ref, vc_ref, o_ref):
    j = pl.program_id(1)
    nblk = SEQ // BLOCK
    kc = kc_ref[...].astype(f32)
    vc = vc_ref[...].astype(f32)

    @pl.when(j < nblk)
    def _():
        wlen = 3 * BLOCK
        s0 = pl.multiple_of(jnp.clip((j - 1) * BLOCK, 0, SEQ - wlen), BLOCK)
        kcat = jnp.concatenate([k_ref[pl.ds(s0, wlen), :].astype(f32), kc], axis=0)
        vcat = jnp.concatenate([v_ref[pl.ds(s0, wlen), :].astype(f32), vc], axis=0)
        qpos = j * BLOCK + lax.broadcasted_iota(jnp.int32, (BLOCK, wlen + CTX_LEN), 0)
        col = lax.broadcasted_iota(jnp.int32, (BLOCK, wlen + CTX_LEN), 1)
        mask = (jnp.abs(qpos - (s0 + col)) <= WINDOW) | (col >= wlen)
        _attend_pairs(q_ref, o_ref, sink_ref, kcat, vcat, mask)

    @pl.when(j >= nblk)
    def _():
        _attend_pairs(q_ref, o_ref, sink_ref, kc, vc, None)


def _attn_a(q, k, v, sink):
    nblk = SEQ // BLOCK
    cblk = CTX_LEN // BLOCK
    nk = HKV_A * HD_A
    qrow = lambda b, j, s: (jnp.where(j < nblk, b * nblk + j, T_LAT // BLOCK + b * cblk + (j - nblk)), 0)
    lat_kv = pl.BlockSpec((SEQ, nk), lambda b, j, s: (b, 0))
    ctx_kv = pl.BlockSpec((CTX_LEN, nk), lambda b, j, s: (T_LAT // CTX_LEN + b, 0))
    return pl.pallas_call(
        _attn_a_kernel,
        grid_spec=pltpu.PrefetchScalarGridSpec(
            num_scalar_prefetch=1,
            grid=(BATCH, nblk + cblk),
            in_specs=[pl.BlockSpec((BLOCK, HQ_A * HD_A), qrow), lat_kv, lat_kv, ctx_kv, ctx_kv],
            out_specs=pl.BlockSpec((BLOCK, HQ_A * HD_A), qrow),
        ),
        out_shape=jax.ShapeDtypeStruct((T_ALL, HQ_A * HD_A), bf16),
        compiler_params=pltpu.CompilerParams(
            dimension_semantics=("arbitrary", "arbitrary"), vmem_limit_bytes=VMEM_LIMIT),
        name="attn_a",
    )(sink, q, k, v, k, v)


def _attn_b_kernel(q_ref, k_ref, v_ref, kc_ref, vc_ref, o_ref):
    rep = HQ_B // HKV_B
    for h in range(HQ_B):
        g = h // rep
        hs = slice(h * HD_B, (h + 1) * HD_B)
        gs = slice(g * HD_B, (g + 1) * HD_B)
        vs = slice(2 * g * HD_B, 2 * (g + 1) * HD_B)
        q = q_ref[:, hs]
        s1 = lax.dot_general(q, k_ref[:, gs], NT_DIMS, preferred_element_type=f32)
        s2 = lax.dot_general(q, kc_ref[:, gs], NT_DIMS, preferred_element_type=f32)
        m = jnp.maximum(jnp.max(s1, axis=-1, keepdims=True), jnp.max(s2, axis=-1, keepdims=True))
        e1 = jnp.exp2(s1 - m).astype(bf16)
        e2 = jnp.exp2(s2 - m).astype(bf16)
        oe = (jnp.dot(e1, v_ref[:, vs], preferred_element_type=f32)
              + jnp.dot(e2, vc_ref[:, vs], preferred_element_type=f32))
        o_ref[:, hs] = (oe[:, :HD_B] * (1.0 / oe[:, HD_B:HD_B + 1])).astype(bf16)


def _attn_b(q, k, v):
    nblk = SEQ // BQ_B
    nk = HKV_B * HD_B
    qrow = lambda b, j: (b * nblk + j, 0)
    lat = lambda n: pl.BlockSpec((SEQ, n), lambda b, j: (b, 0))
    ctx = lambda n: pl.BlockSpec((CTX_LEN, n), lambda b, j: (T_LAT // CTX_LEN + b, 0))
    return pl.pallas_call(
        _attn_b_kernel,
        grid=(BATCH, nblk),
        in_specs=[pl.BlockSpec((BQ_B, HQ_B * HD_B), qrow), lat(nk), lat(2 * nk), ctx(nk), ctx(2 * nk)],
        out_specs=pl.BlockSpec((BQ_B, HQ_B * HD_B), qrow),
        out_shape=jax.ShapeDtypeStruct((T_LAT, HQ_B * HD_B), bf16),
        compiler_params=pltpu.CompilerParams(
            dimension_semantics=("arbitrary", "arbitrary"), vmem_limit_bytes=VMEM_LIMIT),
        name="attn_b",
    )(q, k, v, k, v)


def _post_attn_kernel(o_ref, wo_ref, x_ref, mod_ref, g1_ref, g2_ref, wrt_ref, brt_ref,
                      xn_ref, f_ref, tit_ref, ti_ref, tw_ref, cnt_ref):
    a = jnp.dot(o_ref[...], wo_ref[...], preferred_element_type=f32)
    x = x_ref[...] + mod_ref[2:3, :] * _rms(a, g1_ref[...])
    xn_ref[...] = x
    f = _rms(x, g2_ref[...]) * (1.0 + mod_ref[4:5, :]) + mod_ref[3:4, :]
    f_ref[...] = f.astype(bf16)
    logits = lax.dot_general(wrt_ref[...], f, NT_DIMS, precision=lax.Precision.HIGHEST,
                             preferred_element_type=f32) + brt_ref[...]
    sub = lax.broadcasted_iota(jnp.int32, logits.shape, 0)
    rest = logits
    top_v, top_i = [], []
    for _ in range(TOP_K):
        m = jnp.max(rest, axis=0, keepdims=True)
        idx = jnp.min(jnp.where(rest == m, sub, N_EXPERTS), axis=0, keepdims=True)
        top_v.append(m)
        top_i.append(idx)
        rest = jnp.where(sub == idx, -jnp.inf, rest)
    es = [jnp.exp(v - top_v[0]) for v in top_v]
    inv = 1.0 / (es[0] + es[1] + es[2] + es[3])
    k8 = lax.broadcasted_iota(jnp.int32, (SLOT_ROWS, TMP), 0)
    tit = jnp.full((SLOT_ROWS, TMP), -1.0, f32)
    twt = jnp.zeros((SLOT_ROWS, TMP), f32)
    for k in range(TOP_K):
        tit = jnp.where(k8 == k, top_i[k].astype(f32), tit)
        twt = jnp.where(k8 == k, es[k] * inv, twt)
    tit_ref[...] = tit
    eye = (lax.broadcasted_iota(jnp.int32, (TM, TM), 0)
           == lax.broadcasted_iota(jnp.int32, (TM, TM), 1)).astype(f32)
    to_rows = lambda t: lax.dot_general(eye, t, NT_DIMS, precision=lax.Precision.HIGHEST,
                                        preferred_element_type=f32)
    lane = lax.broadcasted_iota(jnp.int32, (TM, N_EXPERTS), 1).astype(f32)
    for h in range(TMP // TM):
        cols = slice(h * TM, (h + 1) * TM)
        ti = to_rows(tit[:, cols])
        ti_ref[cols, :] = ti
        tw_ref[cols, :] = to_rows(twt[:, cols])
        cnt = jnp.zeros((TM, N_EXPERTS), f32)
        for k in range(TOP_K):
            cnt = cnt + (ti[:, k:k + 1] == lane).astype(f32)
        cnt_ref[h] = jnp.sum(cnt, axis=0, keepdims=True)


def _post_attn(o, wo_bf16, xs, mods, layer, g1, g2, wr, br, n_tiles):
    seg = functools.partial(_seg_of_tile, tiles_per_seq=SEQ // TMP, lat_tiles=T_LAT // TMP)
    rows = n_tiles * TM
    row = lambda n: pl.BlockSpec((TMP, n), lambda i: (i, 0))
    const = lambda a, b: pl.BlockSpec((a, b), lambda i: (0, 0))
    return pl.pallas_call(
        _post_attn_kernel,
        grid=(rows // TMP,),
        in_specs=[
            row(D_MODEL), const(D_MODEL, D_MODEL), row(D_MODEL),
            pl.BlockSpec((None, None, 6, D_MODEL), lambda i: (layer, seg(i), 0, 0)),
            const(1, D_MODEL), const(1, D_MODEL), const(N_EXPERTS, D_MODEL), const(N_EXPERTS, 1),
        ],
        out_specs=[row(D_MODEL), row(D_MODEL),
                   pl.BlockSpec((SLOT_ROWS, TMP), lambda i: (0, i)),
                   row(SLOT_ROWS), row(SLOT_ROWS),
                   pl.BlockSpec((TMP // TM, 1, N_EXPERTS), lambda i: (i, 0, 0))],
        out_shape=[jax.ShapeDtypeStruct((rows, D_MODEL), f32),
                   jax.ShapeDtypeStruct((rows, D_MODEL), bf16),
                   jax.ShapeDtypeStruct((SLOT_ROWS, rows), f32),
                   jax.ShapeDtypeStruct((rows, SLOT_ROWS), f32),
                   jax.ShapeDtypeStruct((rows, SLOT_ROWS), f32),
                   jax.ShapeDtypeStruct((n_tiles, 1, N_EXPERTS), f32)],
        compiler_params=pltpu.CompilerParams(
            dimension_semantics=("arbitrary",), vmem_limit_bytes=VMEM_LIMIT),
        name=f"post_attn{layer}",
    )(o, wo_bf16, xs, mods, g1, g2, wr.T, br.reshape(N_EXPERTS, 1))


def _ffn_tiles_max(n_tiles):
    rows = n_tiles * (TM * TOP_K + N_EXPERTS * (RUN_ALIGN - 1)) + N_EXPERTS * (TME - RUN_ALIGN)
    return -(-rows // TME)


def _route_tables(cnt, n_tiles_max):
    n = cnt.reshape(-1, N_EXPERTS).astype(jnp.int32)
    run = (n + RUN_ALIGN - 1) // RUN_ALIGN * RUN_ALIGN
    loff = jnp.cumsum(run, axis=1) - run
    tot = jnp.sum(run, axis=0)
    gsz = (tot + TME - 1) // TME * TME
    ends = jnp.cumsum(gsz)
    goff = (ends - gsz)[None, :] + jnp.cumsum(run, axis=0) - run
    c0 = jnp.arange(N_CHUNKS, dtype=jnp.int32) * RUN_ALIGN
    owner = jnp.sum(c0[None, :, None] >= (loff + run)[:, None, :], axis=2)
    shift = jnp.where(owner[:, :, None] == jnp.arange(N_EXPERTS)[None, None, :],
                      (goff - loff)[:, None, :], 0)
    tab = jnp.concatenate([jnp.sum(shift, axis=2) + c0[None, :],
                           jnp.sum(run, axis=1, keepdims=True) // RUN_ALIGN], axis=1)[:, None, :]
    n_active = ends[-1] // TME
    tail = jnp.concatenate([ends - gsz + tot, (gsz - tot) // RUN_ALIGN, n_active[None]])[None, :]
    tile = jnp.arange(n_tiles_max, dtype=jnp.int32)
    te = jnp.sum((jnp.minimum(tile, n_active - 1) * TME)[:, None] >= ends[None, :], axis=1)
    loff_f = loff.astype(f32)
    return (tab, tail, loff_f[:, None, :], loff_f[:, :, None], te.astype(jnp.int32),
            n_active.reshape(1).astype(jnp.int32))


def _local_chunk(c):
    if isinstance(c, int):
        return pl.ds(c * RUN_ALIGN, RUN_ALIGN)
    return pl.ds(pl.multiple_of(c * RUN_ALIGN, RUN_ALIGN), RUN_ALIGN)


def _global_chunk(ctab_ref, c):
    return pl.ds(pl.multiple_of(ctab_ref[0, c], RUN_ALIGN), RUN_ALIGN)


def _start_chunks(ctab_ref, copy):
    for c in range(MIN_CHUNKS):
        copy(c).start()
    lax.fori_loop(MIN_CHUNKS, ctab_ref[0, N_CHUNKS], lambda c, carry: (copy(c).start(), carry)[1], 0)


def _wait_chunks(ctab_ref, copy, bulk):
    bulk.wait()
    lax.fori_loop(MIN_CHUNKS, ctab_ref[0, N_CHUNKS], lambda c, carry: (copy(c).wait(), carry)[1], 0)


def _chunk(base, c):
    return pl.ds(pl.multiple_of(base + c * RUN_ALIGN, RUN_ALIGN), RUN_ALIGN)


def _dispatch_kernel(ctab_ref, ctab_prev_ref, tail_ref, f_ref, tit_ref, loffc_ref, xs_hbm,
                     lbuf, zbuf, sem, zsem):
    i = pl.program_id(0)
    tit = tit_ref[...]
    sub = lax.broadcasted_iota(jnp.int32, (N_EXPERTS, TM), 0).astype(f32)
    hit = [tit[k:k + 1, :] == sub for k in range(TOP_K)]
    cnt = sum(h.astype(f32) for h in hit)
    before = (lax.broadcasted_iota(jnp.int32, (TM, TM), 0)
              < lax.broadcasted_iota(jnp.int32, (TM, TM), 1)).astype(bf16)
    base = loffc_ref[...] + jnp.dot(cnt.astype(bf16), before, preferred_element_type=f32)
    rows = lax.broadcasted_iota(jnp.int32, (LOCAL_ROWS, TM), 0).astype(f32)
    onehot = jnp.zeros((LOCAL_ROWS, TM), f32)
    for k in range(TOP_K):
        lp = jnp.sum(jnp.where(hit[k], base, 0.0), axis=0, keepdims=True)
        onehot = onehot + (rows == lp).astype(f32)
    slot = i % 2
    lbuf[slot] = jnp.dot(onehot.astype(bf16), f_ref[...], preferred_element_type=f32)

    def chunk_copy(ctab, s):
        return lambda c: pltpu.make_async_copy(
            lbuf.at[s, _local_chunk(c)], xs_hbm.at[_global_chunk(ctab, c)], sem.at[s])

    def bulk_copy(s):
        rows = MIN_CHUNKS * RUN_ALIGN
        return pltpu.make_async_copy(lbuf.at[s, 0:rows], xs_hbm.at[0:rows], sem.at[s])

    _start_chunks(ctab_ref, chunk_copy(ctab_ref, slot))

    @pl.when(i > 0)
    def _():
        _wait_chunks(ctab_prev_ref, chunk_copy(ctab_prev_ref, 1 - slot), bulk_copy(1 - slot))

    tail_copy = lambda e, c: pltpu.make_async_copy(zbuf, xs_hbm.at[_chunk(tail_ref[0, e], c)], zsem)

    def for_each_tail(fn):
        for e in range(N_EXPERTS):
            def body(c, carry, e=e):
                fn(e, c)
                return carry
            lax.fori_loop(0, tail_ref[0, N_EXPERTS + e], body, 0)

    @pl.when(i == pl.num_programs(0) - 1)
    def _():
        zbuf[...] = jnp.zeros_like(zbuf)
        for_each_tail(lambda e, c: tail_copy(e, c).start())
        for_each_tail(lambda e, c: tail_copy(e, c).wait())
        _wait_chunks(ctab_ref, chunk_copy(ctab_ref, slot), bulk_copy(slot))
        lbuf[slot, 0:TME, :] = jnp.zeros((TME, D_MODEL), f32)
        n_active = tail_ref[0, 2 * N_EXPERTS]
        n_spare = xs_hbm.shape[0] // TME - n_active
        spare_copy = lambda j: pltpu.make_async_copy(
            lbuf.at[slot, 0:TME], xs_hbm.at[pl.ds(pl.multiple_of((n_active + j) * TME, TME), TME)], zsem)
        lax.fori_loop(0, n_spare, lambda j, carry: (spare_copy(j).start(), carry)[1], 0)
        lax.fori_loop(0, n_spare, lambda j, carry: (spare_copy(j).wait(), carry)[1], 0)


def _dispatch(f, tit, tab, tail, loffc, layer, n_tiles, n_tiles_max):
    smem = functools.partial(pl.BlockSpec, memory_space=pltpu.SMEM)
    return pl.pallas_call(
        _dispatch_kernel,
        grid=(n_tiles,),
        in_specs=[
            smem((None, 1, N_CHUNKS + 1), lambda i: (i, 0, 0)),
            smem((None, 1, N_CHUNKS + 1), lambda i: (jnp.maximum(i - 1, 0), 0, 0)),
            smem((1, 2 * N_EXPERTS + 1), lambda i: (0, 0)),
            pl.BlockSpec((TM, D_MODEL), lambda i: (i, 0)),
            pl.BlockSpec((SLOT_ROWS, TM), lambda i: (0, i)),
            pl.BlockSpec((None, N_EXPERTS, 1), lambda i: (i, 0, 0)),
        ],
        out_specs=pl.BlockSpec(memory_space=pl.ANY),
        out_shape=jax.ShapeDtypeStruct((n_tiles_max * TME, D_MODEL), f32),
        scratch_shapes=[pltpu.VMEM((2, LOCAL_ROWS, D_MODEL), f32), pltpu.VMEM((RUN_ALIGN, D_MODEL), f32),
                        pltpu.SemaphoreType.DMA((2,)), pltpu.SemaphoreType.DMA(())],
        compiler_params=pltpu.CompilerParams(
            dimension_semantics=("arbitrary",), vmem_limit_bytes=VMEM_LIMIT),
        name=f"dispatch{layer}",
    )(tab, tab, tail, f, tit, loffc)


def _moe_ffn_kernel(te_ref, na_ref, x_ref, w1_ref, b1_ref, w2_ref, b2_ref, y_ref, w1b, w2b):
    i = pl.program_id(0)
    na = na_ref[0]

    @pl.when(i >= na)
    def _():
        y_ref[...] = jnp.zeros_like(y_ref)

    @pl.when(i < na)
    def _():
        @pl.when((i == 0) | (te_ref[i] != te_ref[jnp.maximum(i - 1, 0)]))
        def _():
            w1b[...] = w1_ref[...].astype(bf16)
            w2b[...] = w2_ref[...].astype(bf16)

        u = jnp.dot(x_ref[...].astype(bf16), w1b[...], preferred_element_type=f32) + b1_ref[...]
        glu = jnp.minimum(u[:, :D_FF], SWIGLU_LIMIT)
        lin = jnp.clip(u[:, D_FF:], -SWIGLU_LIMIT, SWIGLU_LIMIT)
        act = glu * jax.nn.sigmoid(SWIGLU_ALPHA * glu) * (lin + 1.0)
        y_ref[...] = jnp.dot(act.astype(bf16), w2b[...], preferred_element_type=f32) + b2_ref[...]


def _moe_ffn(xs_sorted, te, na, w1, b1, w2, b2, layer, n_tiles_max):
    ex = lambda i, te, na: (layer, te[i], 0, 0)
    return pl.pallas_call(
        _moe_ffn_kernel,
        grid_spec=pltpu.PrefetchScalarGridSpec(
            num_scalar_prefetch=2,
            grid=(n_tiles_max,),
            in_specs=[
                pl.BlockSpec((TME, D_MODEL), lambda i, te, na: (jnp.maximum(jnp.minimum(i, na[0] - 1), 0), 0)),
                pl.BlockSpec((None, None, D_MODEL, 2 * D_FF), ex),
                pl.BlockSpec((None, None, 1, 2 * D_FF), ex),
                pl.BlockSpec((None, None, D_FF, D_MODEL), ex),
                pl.BlockSpec((None, None, 1, D_MODEL), ex),
            ],
            out_specs=pl.BlockSpec((TME, D_MODEL), lambda i, te, na: (i, 0)),
            scratch_shapes=[pltpu.VMEM((D_MODEL, 2 * D_FF), bf16), pltpu.VMEM((D_FF, D_MODEL), bf16)],
        ),
        out_shape=jax.ShapeDtypeStruct((n_tiles_max * TME, D_MODEL), f32),
        compiler_params=pltpu.CompilerParams(
            dimension_semantics=("arbitrary",), vmem_limit_bytes=VMEM_LIMIT),
        name=f"moe_ffn{layer}",
    )(te, na, xs_sorted, w1, b1.reshape(DEPTH, N_EXPERTS, 1, 2 * D_FF), w2,
      b2.reshape(DEPTH, N_EXPERTS, 1, D_MODEL))


def _combine_kernel(ctab_ref, ctab_next_ref, y_hbm, ti_ref, tw_ref, loffr_ref, x_ref, mod_ref, g_ref,
                    xn_ref, ybuf, sem):
    i = pl.program_id(0)
    slot = i % 2

    def chunk_copy(ctab, s):
        return lambda c: pltpu.make_async_copy(
            y_hbm.at[_global_chunk(ctab, c)], ybuf.at[s, _local_chunk(c)], sem.at[s])

    def bulk_copy(s):
        rows = MIN_CHUNKS * RUN_ALIGN
        return pltpu.make_async_copy(y_hbm.at[0:rows], ybuf.at[s, 0:rows], sem.at[s])

    @pl.when(i == 0)
    def _():
        ybuf[...] = jnp.zeros_like(ybuf)
        _start_chunks(ctab_ref, chunk_copy(ctab_ref, 0))

    @pl.when(i + 1 < pl.num_programs(0))
    def _():
        _start_chunks(ctab_next_ref, chunk_copy(ctab_next_ref, 1 - slot))

    _wait_chunks(ctab_ref, chunk_copy(ctab_ref, slot), bulk_copy(slot))

    ti = ti_ref[...]
    tw = tw_ref[...]
    lane = lax.broadcasted_iota(jnp.int32, (TM, N_EXPERTS), 1).astype(f32)
    hit = [ti[:, k:k + 1] == lane for k in range(TOP_K)]
    cnt = sum(h.astype(f32) for h in hit)
    before = (lax.broadcasted_iota(jnp.int32, (TM, TM), 1)
              < lax.broadcasted_iota(jnp.int32, (TM, TM), 0)).astype(bf16)
    base = loffr_ref[...] + jnp.dot(before, cnt.astype(bf16), preferred_element_type=f32)
    cols = lax.broadcasted_iota(jnp.int32, (TM, LOCAL_ROWS), 1).astype(f32)
    w = jnp.zeros((TM, LOCAL_ROWS), f32)
    for k in range(TOP_K):
        lp = jnp.sum(jnp.where(hit[k], base, 0.0), axis=1, keepdims=True)
        w = jnp.where(cols == lp, tw[:, k:k + 1], w)
    w_hi = w.astype(bf16)
    w_lo = (w - w_hi.astype(f32)).astype(bf16)
    yl = ybuf[slot].astype(bf16)
    acc = (jnp.dot(w_hi, yl, preferred_element_type=f32)
           + jnp.dot(w_lo, yl, preferred_element_type=f32))
    xn_ref[...] = x_ref[...] + mod_ref[5:6, :] * _rms(acc, g_ref[...])


def _combine(y, tab, ti, tw, loffr, xs, mods, layer, g, n_tiles):
    seg = functools.partial(_seg_of_tile, tiles_per_seq=TILES_PER_SEQ, lat_tiles=LAT_TILES)
    row = lambda n: pl.BlockSpec((TM, n), lambda i: (i, 0))
    smem = functools.partial(pl.BlockSpec, memory_space=pltpu.SMEM)
    return pl.pallas_call(
        _combine_kernel,
        grid=(n_tiles,),
        in_specs=[
            smem((None, 1, N_CHUNKS + 1), lambda i: (i, 0, 0)),
            smem((None, 1, N_CHUNKS + 1), lambda i: (jnp.minimum(i + 1, n_tiles - 1), 0, 0)),
            pl.BlockSpec(memory_space=pl.ANY),
            row(SLOT_ROWS), row(SLOT_ROWS),
            pl.BlockSpec((None, 1, N_EXPERTS), lambda i: (i, 0, 0)),
            row(D_MODEL),
            pl.BlockSpec((None, None, 6, D_MODEL), lambda i: (layer, seg(i), 0, 0)),
            pl.BlockSpec((1, D_MODEL), lambda i: (0, 0)),
        ],
        out_specs=row(D_MODEL),
        out_shape=jax.ShapeDtypeStruct((n_tiles * TM, D_MODEL), f32),
        scratch_shapes=[pltpu.VMEM((2, LOCAL_ROWS, D_MODEL), f32), pltpu.SemaphoreType.DMA((2,))],
        compiler_params=pltpu.CompilerParams(
            dimension_semantics=("arbitrary",), vmem_limit_bytes=VMEM_LIMIT),
        name=f"combine{layer}",
    )(tab, tab, y, ti, tw, loffr, xs, mods, g)


def _moe(f, tit, ti, tw, cnt, xs, mods, layer, g, w1, b1, w2, b2, n_tiles):
    n_tiles_max = _ffn_tiles_max(n_tiles)
    tab, tail, loffr, loffc, te, na = _route_tables(cnt, n_tiles_max)
    xs_sorted = _dispatch(f, tit, tab, tail, loffc, layer, n_tiles, n_tiles_max)
    y = _moe_ffn(xs_sorted, te, na, w1, b1, w2, b2, layer, n_tiles_max)
    return _combine(y, tab, ti, tw, loffr, xs, mods, layer, g, n_tiles)


def kernel(x, c, ctx, c_ctx, w_ada, b_ada, norm_g, a_w_qkv, a_w_o, a_sink, b_w_qkv, b_q_norm, b_k_norm,
           b_w_o, moe_w_router, moe_b_router, moe_w1, moe_b1, moe_w2, moe_b2):
    assert DEPTH == 2 and x.shape == (BATCH, SEQ, D_MODEL) and ctx.shape == (BATCH, CTX_LEN, D_MODEL)
    xs = jnp.concatenate([x.reshape(T_LAT, D_MODEL), ctx.reshape(T_CTX, D_MODEL)], axis=0)
    c_all = jnp.concatenate(
        [c, c_ctx[None, :], jnp.zeros((MOD_ROWS - BATCH - 1, D_MODEL), f32)], axis=0)
    mods = _adaln(c_all, w_ada, b_ada).reshape(DEPTH, MOD_ROWS, 6, D_MODEL)
    ones = jnp.ones((1, LANES), f32)
    g = lambda i, j: norm_g[i, j][None, :]

    q, k, v = _qkv(xs, mods, 0, g(0, 0), a_w_qkv[0].astype(bf16), _rope_tables(HD_A), ones, ones,
                   hq=HQ_A, hkv=HKV_A, hd=HD_A, qk_norm=False, v_ones=False, tm=TMP)
    o = _attn_a(q, k, v, a_sink[0])
    xs, *routed = _post_attn(o, a_w_o[0].astype(bf16), xs, mods, 0, g(0, 1), g(0, 2),
                             moe_w_router[0], moe_b_router[0], ALL_TILES)
    xs = _moe(*routed, xs, mods, 0, g(0, 3), moe_w1, moe_b1, moe_w2, moe_b2, ALL_TILES)

    q, k, v = _qkv(xs, mods, 1, g(1, 0), b_w_qkv[0].astype(bf16), _rope_tables(HD_B),
                   b_q_norm[0][None, :], b_k_norm[0][None, :],
                   hq=HQ_B, hkv=HKV_B, hd=HD_B, qk_norm=True, v_ones=True, tm=TM)
    o = _attn_b(q, k, v)
    xl, *routed = _post_attn(o, b_w_o[0].astype(bf16), xs, mods, 1, g(1, 1), g(1, 2),
                             moe_w_router[1], moe_b_router[1], LAT_TILES)
    xl = _moe(*routed, xl, mods, 1, g(1, 3), moe_w1, moe_b1, moe_w2, moe_b2, LAT_TILES)
    return xl.reshape(BATCH, SEQ, D_MODEL)
```

```python
import functools

import jax
import jax.numpy as jnp
from jax import lax
from jax.experimental import pallas as pl
from jax.experimental.pallas import tpu as pltpu

D_MODEL = 1024
BATCH = 8
SEQ = 2048
DEPTH = 2
GRID_W = 64
CTX_LEN = 256
BLOCK = 128
WINDOW = 128
ROPE_BASE = 10000.0
EPS = 1e-6
HQ_A, HKV_A, HD_A = 16, 2, 64
HQ_B, HKV_B, HD_B = 8, 2, 128
N_EXPERTS = 32
TOP_K = 4
D_FF = D_MODEL
SWIGLU_LIMIT = 7.0
SWIGLU_ALPHA = 1.702

T_LAT = BATCH * SEQ
T_CTX = BATCH * CTX_LEN
T_ALL = T_LAT + T_CTX
LANES = 128
TM = 256
LAT_TILES = T_LAT // TM
ALL_TILES = T_ALL // TM
TILES_PER_SEQ = SEQ // TM
MOD_ROWS = 16
CTX_MOD_ROW = BATCH
TMP = 512
SUB = 2
BQ_B = 512
TME = 512
SLOT_ROWS = 8
RUN_ALIGN = 8
LOCAL_ROWS = -(-(TM * TOP_K + N_EXPERTS * (RUN_ALIGN - 1)) // LANES) * LANES
N_CHUNKS = LOCAL_ROWS // RUN_ALIGN
MIN_CHUNKS = TM * TOP_K // RUN_ALIGN
NEG = -1e30
LOG2E = 1.4426950408889634
NT_DIMS = (((1,), (1,)), ((), ()))
VMEM_LIMIT = 56 * 1024 * 1024

f32 = jnp.float32
bf16 = jnp.bfloat16


def _seg_of_tile(i, tiles_per_seq, lat_tiles):
    return jnp.where(i < lat_tiles, i // tiles_per_seq, CTX_MOD_ROW)


def _adaln_kernel(c_ref, w_ref, b_ref, o_ref):
    c = c_ref[...]
    s = c * jax.nn.sigmoid(c)
    o_ref[...] = jnp.dot(s, w_ref[...], precision=lax.Precision.HIGHEST,
                         preferred_element_type=f32) + b_ref[...]


def _adaln(c_all, w_ada, b_ada):
    tn = 1536
    return pl.pallas_call(
        _adaln_kernel,
        grid=(DEPTH, 6 * D_MODEL // tn),
        in_specs=[
            pl.BlockSpec((MOD_ROWS, D_MODEL), lambda l, j: (0, 0)),
            pl.BlockSpec((None, D_MODEL, tn), lambda l, j: (l, 0, j)),
            pl.BlockSpec((None, 1, tn), lambda l, j: (l, 0, j)),
        ],
        out_specs=pl.BlockSpec((None, MOD_ROWS, tn), lambda l, j: (l, 0, j)),
        out_shape=jax.ShapeDtypeStruct((DEPTH, MOD_ROWS, 6 * D_MODEL), f32),
        compiler_params=pltpu.CompilerParams(
            dimension_semantics=("arbitrary", "arbitrary"), vmem_limit_bytes=VMEM_LIMIT),
        name="adaln",
    )(c_all, w_ada, b_ada.reshape(DEPTH, 1, 6 * D_MODEL))


def _rms(x, g):
    return x * lax.rsqrt(jnp.mean(x * x, axis=-1, keepdims=True) + EPS) * g


def _stream_specs(tm, ctx_base):
    lat_tiles = T_LAT // tm
    return [pl.BlockSpec((tm, D_MODEL), lambda i, *_: (jnp.minimum(i, lat_tiles - 1), 0)),
            pl.BlockSpec((tm, D_MODEL), lambda i, *_: (jnp.maximum(i, lat_tiles) - ctx_base, 0))]


def _stream_tile(x_ref, xc_ref):
    lat_tiles = T_LAT // x_ref.shape[0]
    return jnp.where(pl.program_id(0) < lat_tiles, x_ref[...], xc_ref[...])


def _qkv_kernel(x_ref, xc_ref, mod_ref, g_ref, w_ref, cos_ref, sa_ref, sb_ref, qn_ref, kn_ref,
                q_ref, k_ref, v_ref, *, nq, nk, hd, qk_norm, v_ones):
    h = _rms(_stream_tile(x_ref, xc_ref), g_ref[...]) * (1.0 + mod_ref[1:2, :]) + mod_ref[0:1, :]
    qkv = jnp.dot(h.astype(bf16), w_ref[...], preferred_element_type=f32)
    cos, sa, sb = cos_ref[...], sa_ref[...], sb_ref[...]
    quarter = hd // 4
    scale = hd ** -0.5 * LOG2E

    def rope(c):
        return c * cos + pltpu.roll(c, quarter, 1) * sa + pltpu.roll(c, LANES - quarter, 1) * sb

    for j in range(nq // LANES):
        c = qkv[:, j * LANES:(j + 1) * LANES]
        if qk_norm:
            c = _rms(c, qn_ref[...])
        q_ref[:, j * LANES:(j + 1) * LANES] = (rope(c) * scale).astype(bf16)
    for j in range(nk // LANES):
        c = qkv[:, nq + j * LANES:nq + (j + 1) * LANES]
        if qk_norm:
            c = _rms(c, kn_ref[...])
        k_ref[:, j * LANES:(j + 1) * LANES] = rope(c).astype(bf16)
    if v_ones:
        for j in range(nk // LANES):
            v_ref[:, 2 * j * LANES:(2 * j + 1) * LANES] = (
                qkv[:, nq + nk + j * LANES:nq + nk + (j + 1) * LANES].astype(bf16))
            v_ref[:, (2 * j + 1) * LANES:(2 * j + 2) * LANES] = jnp.ones((x_ref.shape[0], LANES), bf16)
    else:
        v_ref[...] = qkv[:, nq + nk:].astype(bf16)


def _qkv(x_lat, x_ctx, mods, layer, g, w_bf16, tabs, qn, kn, *, hq, hkv, hd, qk_norm, v_ones, tm):
    ctx_base = 0 if x_ctx is x_lat else T_LAT // tm
    nq, nk = hq * hd, hkv * hd
    nv = 2 * nk if v_ones else nk
    per_seq, lat_tiles = SEQ // tm, T_LAT // tm
    seg = functools.partial(_seg_of_tile, tiles_per_seq=per_seq, lat_tiles=lat_tiles)
    tab_idx = lambda i: (jnp.where(i < lat_tiles, i % per_seq, per_seq), 0)
    tab_spec = pl.BlockSpec((tm, LANES), tab_idx)
    row = lambda n: pl.BlockSpec((tm, n), lambda i: (i, 0))
    return pl.pallas_call(
        functools.partial(_qkv_kernel, nq=nq, nk=nk, hd=hd, qk_norm=qk_norm, v_ones=v_ones),
        grid=(T_ALL // tm,),
        in_specs=[
            *_stream_specs(tm, ctx_base),
            pl.BlockSpec((None, None, 6, D_MODEL), lambda i: (layer, seg(i), 0, 0)),
            pl.BlockSpec((1, D_MODEL), lambda i: (0, 0)),
            pl.BlockSpec((D_MODEL, nq + 2 * nk), lambda i: (0, 0)),
            tab_spec, tab_spec, tab_spec,
            pl.BlockSpec((1, LANES), lambda i: (0, 0)),
            pl.BlockSpec((1, LANES), lambda i: (0, 0)),
        ],
        out_specs=[row(nq), row(nk), row(nv)],
        out_shape=[jax.ShapeDtypeStruct((T_ALL, nq), bf16),
                   jax.ShapeDtypeStruct((T_ALL, nk), bf16),
                   jax.ShapeDtypeStruct((T_ALL, nv), bf16)],
        compiler_params=pltpu.CompilerParams(
            dimension_semantics=("arbitrary",), vmem_limit_bytes=VMEM_LIMIT),
        name=f"qkv{layer}",
    )(x_lat, x_ctx, mods, g, w_bf16, *tabs, qn, kn)


def _rope_tables(hd):
    quarter = hd // 4
    inv_freq = jnp.float32(ROPE_BASE) ** (-jnp.arange(quarter, dtype=f32) / quarter)
    t = jnp.arange(SEQ)
    ang_r = (t // GRID_W).astype(f32)[:, None] * inv_freq[None, :]
    ang_c = (t % GRID_W).astype(f32)[:, None] * inv_freq[None, :]
    z = jnp.zeros_like(ang_r)
    cos = jnp.concatenate([jnp.cos(ang_r)] * 2 + [jnp.cos(ang_c)] * 2, axis=-1)
    sa = jnp.concatenate([z, jnp.sin(ang_r), z, jnp.sin(ang_c)], axis=-1)
    sb = jnp.concatenate([-jnp.sin(ang_r), z, -jnp.sin(ang_c), z], axis=-1)
    rep = LANES // hd
    pad = lambda a, v: jnp.concatenate(
        [jnp.tile(a, (1, rep)), jnp.full((TMP, LANES), v, f32)], axis=0)
    return pad(cos, 1.0), pad(sa, 0.0), pad(sb, 0.0)


def _pair_operand(x, g):
    lane = lax.broadcasted_iota(jnp.int32, x.shape, 1)
    swapped = pltpu.roll(x, HD_A, 1)
    lo_src, hi_src = (x, swapped) if g == 0 else (swapped, x)
    lo = jnp.where(lane < HD_A, lo_src, 0.0)
    hi = jnp.where(lane >= HD_A, hi_src, 0.0)
    return jnp.concatenate([lo, hi], axis=0)


def _attend_pairs(q_ref, o_ref, sink_ref, kcat, vcat, mask):
    n = kcat.shape[0]
    pairs = HQ_A // HKV_A // 2
    rows = pairs * BLOCK
    lane = lax.broadcasted_iota(jnp.int32, (rows, LANES), 1)
    pair_of_row = lax.broadcasted_iota(jnp.int32, (rows, 1), 0) // BLOCK
    ind_row = lax.broadcasted_iota(jnp.int32, (2 * n, LANES), 0)
    ind_lane = lax.broadcasted_iota(jnp.int32, (2 * n, LANES), 1)
    ind = jnp.where(ind_lane == ind_row // n, 1.0, 0.0)
    if mask is not None:
        mask = jnp.concatenate([mask] * pairs, axis=0)
    for g in range(HKV_A):
        kp = _pair_operand(kcat, g).astype(bf16)
        vp = jnp.concatenate([_pair_operand(vcat, g), ind], axis=1).astype(bf16)
        cols = [(g * pairs + p) * LANES for p in range(pairs)]
        qs = jnp.concatenate([q_ref[:, c:c + LANES] for c in cols], axis=0)
        s = lax.dot_general(qs, kp, NT_DIMS, preferred_element_type=f32)
        es, ms, sks = [], [], []
        for hh in range(2):
            sh = s[:, hh * n:(hh + 1) * n]
            if mask is not None:
                sh = jnp.where(mask, sh, NEG)
            sk = jnp.zeros((rows, 1), f32)
            for p in range(pairs):
                sk = jnp.where(pair_of_row == p, sink_ref[cols[p] // HD_A + hh] * LOG2E, sk)
            m = jnp.maximum(jnp.max(sh, axis=-1, keepdims=True), sk)
            es.append(jnp.exp2(sh - m).astype(bf16))
            ms.append(m)
            sks.append(sk)
        oe = jnp.dot(jnp.concatenate(es, axis=1), vp, preferred_element_type=f32)
        inv = [1.0 / (oe[:, LANES + hh:LANES + hh + 1] + jnp.exp2(sks[hh] - ms[hh])) for hh in range(2)]
        o = oe[:, :LANES] * jnp.where(lane < HD_A, inv[0], inv[1])
        for p in range(pairs):
            o_ref[:, cols[p]:cols[p] + LANES] = o[p * BLOCK:(p + 1) * BLOCK].astype(bf16)


def _attn_a_kernel(sink_ref, q_ref, k_ref, v_ref, kc_ref, vc_ref, o_ref):
    j = pl.program_id(1)
    nblk = SEQ // BLOCK
    kc = kc_ref[...].astype(f32)
    vc = vc_ref[...].astype(f32)

    @pl.when(j < nblk)
    def _():
        wlen = 3 * BLOCK
        s0 = pl.multiple_of(jnp.clip((j - 1) * BLOCK, 0, SEQ - wlen), BLOCK)
        kcat = jnp.concatenate([k_ref[pl.ds(s0, wlen), :].astype(f32), kc], axis=0)
        vcat = jnp.concatenate([v_ref[pl.ds(s0, wlen), :].astype(f32), vc], axis=0)
        qpos = j * BLOCK + lax.broadcasted_iota(jnp.int32, (BLOCK, wlen + CTX_LEN), 0)
        col = lax.broadcasted_iota(jnp.int32, (BLOCK, wlen + CTX_LEN), 1)
        mask = (jnp.abs(qpos - (s0 + col)) <= WINDOW) | (col >= wlen)
        _attend_pairs(q_ref, o_ref, sink_ref, kcat, vcat, mask)

    @pl.when(j >= nblk)
    def _():
        _attend_pairs(q_ref, o_ref, sink_ref, kc, vc, None)


def _attn_a(q, k, v, sink):
    nblk = SEQ // BLOCK
    cblk = CTX_LEN // BLOCK
    nk = HKV_A * HD_A
    qrow = lambda b, j, s: (jnp.where(j < nblk, b * nblk + j, T_LAT // BLOCK + b * cblk + (j - nblk)), 0)
    lat_kv = pl.BlockSpec((SEQ, nk), lambda b, j, s: (b, 0))
    ctx_kv = pl.BlockSpec((CTX_LEN, nk), lambda b, j, s: (T_LAT // CTX_LEN + b, 0))
    return pl.pallas_call(
        _attn_a_kernel,
        grid_spec=pltpu.PrefetchScalarGridSpec(
            num_scalar_prefetch=1,
            grid=(BATCH, nblk + cblk),
            in_specs=[pl.BlockSpec((BLOCK, HQ_A * HD_A), qrow), lat_kv, lat_kv, ctx_kv, ctx_kv],
            out_specs=pl.BlockSpec((BLOCK, HQ_A * HD_A), qrow),
        ),
        out_shape=jax.ShapeDtypeStruct((T_ALL, HQ_A * HD_A), bf16),
        compiler_params=pltpu.CompilerParams(
            dimension_semantics=("arbitrary", "arbitrary"), vmem_limit_bytes=VMEM_LIMIT),
        name="attn_a",
    )(sink, q, k, v, k, v)


def _attn_b_kernel(q_ref, k_ref, v_ref, kc_ref, vc_ref, o_ref):
    rep = HQ_B // HKV_B
    for h in range(HQ_B):
        g = h // rep
        hs = slice(h * HD_B, (h + 1) * HD_B)
        gs = slice(g * HD_B, (g + 1) * HD_B)
        vs = slice(2 * g * HD_B, 2 * (g + 1) * HD_B)
        q = q_ref[:, hs]
        s1 = lax.dot_general(q, k_ref[:, gs], NT_DIMS, preferred_element_type=f32)
        s2 = lax.dot_general(q, kc_ref[:, gs], NT_DIMS, preferred_element_type=f32)
        m = jnp.maximum(jnp.max(s1, axis=-1, keepdims=True), jnp.max(s2, axis=-1, keepdims=True))
        e1 = jnp.exp2(s1 - m).astype(bf16)
        e2 = jnp.exp2(s2 - m).astype(bf16)
        oe = (jnp.dot(e1, v_ref[:, vs], preferred_element_type=f32)
              + jnp.dot(e2, vc_ref[:, vs], preferred_element_type=f32))
        o_ref[:, hs] = (oe[:, :HD_B] * (1.0 / oe[:, HD_B:HD_B + 1])).astype(bf16)


def _attn_b(q, k, v):
    nblk = SEQ // BQ_B
    nk = HKV_B * HD_B
    qrow = lambda b, j: (b * nblk + j, 0)
    lat = lambda n: pl.BlockSpec((SEQ, n), lambda b, j: (b, 0))
    ctx = lambda n: pl.BlockSpec((CTX_LEN, n), lambda b, j: (T_LAT // CTX_LEN + b, 0))
    return pl.pallas_call(
        _attn_b_kernel,
        grid=(BATCH, nblk),
        in_specs=[pl.BlockSpec((BQ_B, HQ_B * HD_B), qrow), lat(nk), lat(2 * nk), ctx(nk), ctx(2 * nk)],
        out_specs=pl.BlockSpec((BQ_B, HQ_B * HD_B), qrow),
        out_shape=jax.ShapeDtypeStruct((T_LAT, HQ_B * HD_B), bf16),
        compiler_params=pltpu.CompilerParams(
            dimension_semantics=("arbitrary", "arbitrary"), vmem_limit_bytes=VMEM_LIMIT),
        name="attn_b",
    )(q, k, v, k, v)


def _post_attn_kernel(o_ref, wo_ref, x_ref, xc_ref, mod_ref, g1_ref, g2_ref, wrt_ref, brt_ref,
                      xn_ref, f_ref, tit_ref, ti_ref, tw_ref, cnt_ref):
    a = jnp.dot(o_ref[...], wo_ref[...], preferred_element_type=f32)
    x = _stream_tile(x_ref, xc_ref) + mod_ref[2:3, :] * _rms(a, g1_ref[...])
    xn_ref[...] = x
    f = _rms(x, g2_ref[...]) * (1.0 + mod_ref[4:5, :]) + mod_ref[3:4, :]
    f_ref[...] = f.astype(bf16)
    logits = lax.dot_general(wrt_ref[...], f, NT_DIMS, precision=lax.Precision.HIGHEST,
                             preferred_element_type=f32) + brt_ref[...]
    sub = lax.broadcasted_iota(jnp.int32, logits.shape, 0)
    rest = logits
    top_v, top_i = [], []
    for _ in range(TOP_K):
        m = jnp.max(rest, axis=0, keepdims=True)
        idx = jnp.min(jnp.where(rest == m, sub, N_EXPERTS), axis=0, keepdims=True)
        top_v.append(m)
        top_i.append(idx)
        rest = jnp.where(sub == idx, -jnp.inf, rest)
    es = [jnp.exp(v - top_v[0]) for v in top_v]
    inv = 1.0 / (es[0] + es[1] + es[2] + es[3])
    k8 = lax.broadcasted_iota(jnp.int32, (SLOT_ROWS, TMP), 0)
    tit = jnp.full((SLOT_ROWS, TMP), -1.0, f32)
    twt = jnp.zeros((SLOT_ROWS, TMP), f32)
    for k in range(TOP_K):
        tit = jnp.where(k8 == k, top_i[k].astype(f32), tit)
        twt = jnp.where(k8 == k, es[k] * inv, twt)
    tit_ref[...] = tit
    eye = (lax.broadcasted_iota(jnp.int32, (TM, TM), 0)
           == lax.broadcasted_iota(jnp.int32, (TM, TM), 1)).astype(f32)
    to_rows = lambda t: lax.dot_general(eye, t, NT_DIMS, precision=lax.Precision.HIGHEST,
                                        preferred_element_type=f32)
    lane = lax.broadcasted_iota(jnp.int32, (TM, N_EXPERTS), 1).astype(f32)
    for h in range(TMP // TM):
        cols = slice(h * TM, (h + 1) * TM)
        ti = to_rows(tit[:, cols])
        ti_ref[cols, :] = ti
        tw_ref[cols, :] = to_rows(twt[:, cols])
        cnt = jnp.zeros((TM, N_EXPERTS), f32)
        for k in range(TOP_K):
            cnt = cnt + (ti[:, k:k + 1] == lane).astype(f32)
        cnt_ref[h] = jnp.sum(cnt, axis=0, keepdims=True)


def _post_attn(o, wo_bf16, x_lat, x_ctx, mods, layer, g1, g2, wr, br, n_tiles):
    ctx_base = 0 if x_ctx is x_lat else T_LAT // TMP
    seg = functools.partial(_seg_of_tile, tiles_per_seq=SEQ // TMP, lat_tiles=T_LAT // TMP)
    rows = n_tiles * TM
    row = lambda n: pl.BlockSpec((TMP, n), lambda i: (i, 0))
    const = lambda a, b: pl.BlockSpec((a, b), lambda i: (0, 0))
    return pl.pallas_call(
        _post_attn_kernel,
        grid=(rows // TMP,),
        in_specs=[
            row(D_MODEL), const(D_MODEL, D_MODEL), *_stream_specs(TMP, ctx_base),
            pl.BlockSpec((None, None, 6, D_MODEL), lambda i: (layer, seg(i), 0, 0)),
            const(1, D_MODEL), const(1, D_MODEL), const(N_EXPERTS, D_MODEL), const(N_EXPERTS, 1),
        ],
        out_specs=[row(D_MODEL), row(D_MODEL),
                   pl.BlockSpec((SLOT_ROWS, TMP), lambda i: (0, i)),
                   row(SLOT_ROWS), row(SLOT_ROWS),
                   pl.BlockSpec((TMP // TM, 1, N_EXPERTS), lambda i: (i, 0, 0))],
        out_shape=[jax.ShapeDtypeStruct((rows, D_MODEL), f32),
                   jax.ShapeDtypeStruct((rows, D_MODEL), bf16),
                   jax.ShapeDtypeStruct((SLOT_ROWS, rows), f32),
                   jax.ShapeDtypeStruct((rows, SLOT_ROWS), f32),
                   jax.ShapeDtypeStruct((rows, SLOT_ROWS), f32),
                   jax.ShapeDtypeStruct((n_tiles, 1, N_EXPERTS), f32)],
        compiler_params=pltpu.CompilerParams(
            dimension_semantics=("arbitrary",), vmem_limit_bytes=VMEM_LIMIT),
        name=f"post_attn{layer}",
    )(o, wo_bf16, x_lat, x_ctx, mods, g1, g2, wr.T, br.reshape(N_EXPERTS, 1))


def _ffn_tiles_max(n_tiles):
    rows = n_tiles * (TM * TOP_K + N_EXPERTS * (RUN_ALIGN - 1)) + N_EXPERTS * (TME - RUN_ALIGN)
    return -(-rows // TME)


def _route_tables(cnt, n_tiles_max):
    n = cnt.reshape(-1, N_EXPERTS).astype(jnp.int32)
    run = (n + RUN_ALIGN - 1) // RUN_ALIGN * RUN_ALIGN
    loff = jnp.cumsum(run, axis=1) - run
    tot = jnp.sum(run, axis=0)
    gsz = (tot + TME - 1) // TME * TME
    ends = jnp.cumsum(gsz)
    goff = (ends - gsz)[None, :] + jnp.cumsum(run, axis=0) - run
    c0 = jnp.arange(N_CHUNKS, dtype=jnp.int32) * RUN_ALIGN
    owner = jnp.sum(c0[None, :, None] >= (loff + run)[:, None, :], axis=2)
    shift = jnp.where(owner[:, :, None] == jnp.arange(N_EXPERTS)[None, None, :],
                      (goff - loff)[:, None, :], 0)
    tab = jnp.concatenate([jnp.sum(shift, axis=2) + c0[None, :],
                           jnp.sum(run, axis=1, keepdims=True) // RUN_ALIGN], axis=1)[:, None, :]
    n_active = ends[-1] // TME
    tail = jnp.concatenate([ends - gsz + tot, (gsz - tot) // RUN_ALIGN, n_active[None]])[None, :]
    tile = jnp.arange(n_tiles_max, dtype=jnp.int32)
    te = jnp.sum((jnp.minimum(tile, n_active - 1) * TME)[:, None] >= ends[None, :], axis=1)
    loff_f = loff.astype(f32)
    return (tab, tail, loff_f[:, None, :], loff_f[:, :, None], te.astype(jnp.int32),
            n_active.reshape(1).astype(jnp.int32))


def _local_chunk(c):
    if isinstance(c, int):
        return pl.ds(c * RUN_ALIGN, RUN_ALIGN)
    return pl.ds(pl.multiple_of(c * RUN_ALIGN, RUN_ALIGN), RUN_ALIGN)


def _global_chunk(ctab_ref, h, c):
    return pl.ds(pl.multiple_of(ctab_ref[h, 0, c], RUN_ALIGN), RUN_ALIGN)


def _start_chunks(ctab_ref, h, copy):
    for c in range(MIN_CHUNKS):
        copy(c).start()
    lax.fori_loop(MIN_CHUNKS, ctab_ref[h, 0, N_CHUNKS], lambda c, carry: (copy(c).start(), carry)[1], 0)


def _wait_chunks(ctab_ref, h, copy, bulk):
    bulk.wait()
    lax.fori_loop(MIN_CHUNKS, ctab_ref[h, 0, N_CHUNKS], lambda c, carry: (copy(c).wait(), carry)[1], 0)


def _chunk(base, c):
    return pl.ds(pl.multiple_of(base + c * RUN_ALIGN, RUN_ALIGN), RUN_ALIGN)


def _dispatch_kernel(ctab_ref, ctab_prev_ref, tail_ref, f_ref, tit_ref, loffc_ref, xs_hbm,
                     lbuf, zbuf, sem, zsem):
    i = pl.program_id(0)
    slot = i % 2
    sub = lax.broadcasted_iota(jnp.int32, (N_EXPERTS, TM), 0).astype(f32)
    before = (lax.broadcasted_iota(jnp.int32, (TM, TM), 0)
              < lax.broadcasted_iota(jnp.int32, (TM, TM), 1)).astype(bf16)
    rows = lax.broadcasted_iota(jnp.int32, (LOCAL_ROWS, TM), 0).astype(f32)

    def chunk_copy(ctab, h, b):
        return lambda c: pltpu.make_async_copy(
            lbuf.at[b, _local_chunk(c)], xs_hbm.at[_global_chunk(ctab, h, c)], sem.at[b])

    def bulk_copy(b):
        n = MIN_CHUNKS * RUN_ALIGN
        return pltpu.make_async_copy(lbuf.at[b, 0:n], xs_hbm.at[0:n], sem.at[b])

    def wait_tiles(ctab, s):
        for h in range(SUB):
            b = s * SUB + h
            _wait_chunks(ctab, h, chunk_copy(ctab, h, b), bulk_copy(b))

    for h in range(SUB):
        tit = tit_ref[:, h * TM:(h + 1) * TM]
        hit = [tit[k:k + 1, :] == sub for k in range(TOP_K)]
        cnt = sum(x.astype(f32) for x in hit)
        base = loffc_ref[h] + jnp.dot(cnt.astype(bf16), before, preferred_element_type=f32)
        onehot = jnp.zeros((LOCAL_ROWS, TM), f32)
        for k in range(TOP_K):
            lp = jnp.sum(jnp.where(hit[k], base, 0.0), axis=0, keepdims=True)
            onehot = onehot + (rows == lp).astype(f32)
        b = slot * SUB + h
        lbuf[b] = jnp.dot(onehot.astype(bf16), f_ref[h * TM:(h + 1) * TM, :], preferred_element_type=f32)
        _start_chunks(ctab_ref, h, chunk_copy(ctab_ref, h, b))

    @pl.when(i > 0)
    def _():
        wait_tiles(ctab_prev_ref, 1 - slot)

    tail_copy = lambda e, c: pltpu.make_async_copy(zbuf, xs_hbm.at[_chunk(tail_ref[0, e], c)], zsem)

    def for_each_tail(fn):
        for e in range(N_EXPERTS):
            def body(c, carry, e=e):
                fn(e, c)
                return carry
            lax.fori_loop(0, tail_ref[0, N_EXPERTS + e], body, 0)

    @pl.when(i == pl.num_programs(0) - 1)
    def _():
        zbuf[...] = jnp.zeros_like(zbuf)
        for_each_tail(lambda e, c: tail_copy(e, c).start())
        for_each_tail(lambda e, c: tail_copy(e, c).wait())
        wait_tiles(ctab_ref, slot)
        lbuf[0, 0:TME, :] = jnp.zeros((TME, D_MODEL), f32)
        n_active = tail_ref[0, 2 * N_EXPERTS]
        n_spare = xs_hbm.shape[0] // TME - n_active
        spare_copy = lambda j: pltpu.make_async_copy(
            lbuf.at[0, 0:TME], xs_hbm.at[pl.ds(pl.multiple_of((n_active + j) * TME, TME), TME)], zsem)
        lax.fori_loop(0, n_spare, lambda j, carry: (spare_copy(j).start(), carry)[1], 0)
        lax.fori_loop(0, n_spare, lambda j, carry: (spare_copy(j).wait(), carry)[1], 0)


def _dispatch(f, tit, tab, tail, loffc, layer, n_tiles, n_tiles_max):
    smem = functools.partial(pl.BlockSpec, memory_space=pltpu.SMEM)
    return pl.pallas_call(
        _dispatch_kernel,
        grid=(n_tiles // SUB,),
        in_specs=[
            smem((SUB, 1, N_CHUNKS + 1), lambda i: (i, 0, 0)),
            smem((SUB, 1, N_CHUNKS + 1), lambda i: (jnp.maximum(i - 1, 0), 0, 0)),
            smem((1, 2 * N_EXPERTS + 1), lambda i: (0, 0)),
            pl.BlockSpec((SUB * TM, D_MODEL), lambda i: (i, 0)),
            pl.BlockSpec((SLOT_ROWS, SUB * TM), lambda i: (0, i)),
            pl.BlockSpec((SUB, N_EXPERTS, 1), lambda i: (i, 0, 0)),
        ],
        out_specs=pl.BlockSpec(memory_space=pl.ANY),
        out_shape=jax.ShapeDtypeStruct((n_tiles_max * TME, D_MODEL), f32),
        scratch_shapes=[pltpu.VMEM((2 * SUB, LOCAL_ROWS, D_MODEL), f32),
                        pltpu.VMEM((RUN_ALIGN, D_MODEL), f32),
                        pltpu.SemaphoreType.DMA((2 * SUB,)), pltpu.SemaphoreType.DMA(())],
        compiler_params=pltpu.CompilerParams(
            dimension_semantics=("arbitrary",), vmem_limit_bytes=VMEM_LIMIT),
        name=f"dispatch{layer}",
    )(tab, tab, tail, f, tit, loffc)


def _moe_ffn_kernel(te_ref, na_ref, x_ref, w1_ref, b1_ref, w2_ref, b2_ref, y_ref, w1b, w2b):
    i = pl.program_id(0)
    na = na_ref[0]

    @pl.when(i >= na)
    def _():
        y_ref[...] = jnp.zeros_like(y_ref)

    @pl.when(i < na)
    def _():
        @pl.when((i == 0) | (te_ref[i] != te_ref[jnp.maximum(i - 1, 0)]))
        def _():
            w1b[...] = w1_ref[...].astype(bf16)
            w2b[...] = w2_ref[...].astype(bf16)

        u = jnp.dot(x_ref[...].astype(bf16), w1b[...], preferred_element_type=f32) + b1_ref[...]
        glu = jnp.minimum(u[:, :D_FF], SWIGLU_LIMIT)
        lin = jnp.clip(u[:, D_FF:], -SWIGLU_LIMIT, SWIGLU_LIMIT)
        act = glu * jax.nn.sigmoid(SWIGLU_ALPHA * glu) * (lin + 1.0)
        y_ref[...] = jnp.dot(act.astype(bf16), w2b[...], preferred_element_type=f32) + b2_ref[...]


def _moe_ffn(xs_sorted, te, na, w1, b1, w2, b2, layer, n_tiles_max):
    ex = lambda i, te, na: (layer, te[i], 0, 0)
    return pl.pallas_call(
        _moe_ffn_kernel,
        grid_spec=pltpu.PrefetchScalarGridSpec(
            num_scalar_prefetch=2,
            grid=(n_tiles_max,),
            in_specs=[
                pl.BlockSpec((TME, D_MODEL), lambda i, te, na: (jnp.maximum(jnp.minimum(i, na[0] - 1), 0), 0)),
                pl.BlockSpec((None, None, D_MODEL, 2 * D_FF), ex),
                pl.BlockSpec((None, None, 1, 2 * D_FF), ex),
                pl.BlockSpec((None, None, D_FF, D_MODEL), ex),
                pl.BlockSpec((None, None, 1, D_MODEL), ex),
            ],
            out_specs=pl.BlockSpec((TME, D_MODEL), lambda i, te, na: (i, 0)),
            scratch_shapes=[pltpu.VMEM((D_MODEL, 2 * D_FF), bf16), pltpu.VMEM((D_FF, D_MODEL), bf16)],
        ),
        out_shape=jax.ShapeDtypeStruct((n_tiles_max * TME, D_MODEL), f32),
        compiler_params=pltpu.CompilerParams(
            dimension_semantics=("arbitrary",), vmem_limit_bytes=VMEM_LIMIT),
        name=f"moe_ffn{layer}",
    )(te, na, xs_sorted, w1, b1.reshape(DEPTH, N_EXPERTS, 1, 2 * D_FF), w2,
      b2.reshape(DEPTH, N_EXPERTS, 1, D_MODEL))


def _combine_kernel(ctab_ref, ctab_next_ref, y_hbm, ti_ref, tw_ref, loffr_ref, x_ref, mod_ref, g_ref,
                    xn_ref, ybuf, sem):
    i = pl.program_id(0)
    slot = i % 2

    def chunk_copy(ctab, h, b):
        return lambda c: pltpu.make_async_copy(
            y_hbm.at[_global_chunk(ctab, h, c)], ybuf.at[b, _local_chunk(c)], sem.at[b])

    def bulk_copy(b):
        n = MIN_CHUNKS * RUN_ALIGN
        return pltpu.make_async_copy(y_hbm.at[0:n], ybuf.at[b, 0:n], sem.at[b])

    def start_tiles(ctab, s):
        for h in range(SUB):
            _start_chunks(ctab, h, chunk_copy(ctab, h, s * SUB + h))

    @pl.when(i == 0)
    def _():
        ybuf[...] = jnp.zeros_like(ybuf)
        start_tiles(ctab_ref, 0)

    @pl.when(i + 1 < pl.num_programs(0))
    def _():
        start_tiles(ctab_next_ref, 1 - slot)

    lane = lax.broadcasted_iota(jnp.int32, (TM, N_EXPERTS), 1).astype(f32)
    before = (lax.broadcasted_iota(jnp.int32, (TM, TM), 1)
              < lax.broadcasted_iota(jnp.int32, (TM, TM), 0)).astype(bf16)
    cols = lax.broadcasted_iota(jnp.int32, (TM, LOCAL_ROWS), 1).astype(f32)
    for h in range(SUB):
        tok = slice(h * TM, (h + 1) * TM)
        b = slot * SUB + h
        ti = ti_ref[tok, :]
        tw = tw_ref[tok, :]
        hit = [ti[:, k:k + 1] == lane for k in range(TOP_K)]
        cnt = sum(x.astype(f32) for x in hit)
        base = loffr_ref[h] + jnp.dot(before, cnt.astype(bf16), preferred_element_type=f32)
        w = jnp.zeros((TM, LOCAL_ROWS), f32)
        for k in range(TOP_K):
            lp = jnp.sum(jnp.where(hit[k], base, 0.0), axis=1, keepdims=True)
            w = jnp.where(cols == lp, tw[:, k:k + 1], w)
        w_hi = w.astype(bf16)
        w_lo = (w - w_hi.astype(f32)).astype(bf16)
        _wait_chunks(ctab_ref, h, chunk_copy(ctab_ref, h, b), bulk_copy(b))
        yl = ybuf[b].astype(bf16)
        acc = (jnp.dot(w_hi, yl, preferred_element_type=f32)
               + jnp.dot(w_lo, yl, preferred_element_type=f32))
        xn_ref[tok, :] = x_ref[tok, :] + mod_ref[5:6, :] * _rms(acc, g_ref[...])


def _combine(y, tab, ti, tw, loffr, xs, mods, layer, g, n_tiles):
    rows = SUB * TM
    n_steps = n_tiles // SUB
    seg = functools.partial(_seg_of_tile, tiles_per_seq=SEQ // rows, lat_tiles=T_LAT // rows)
    row = lambda n: pl.BlockSpec((rows, n), lambda i: (i, 0))
    smem = functools.partial(pl.BlockSpec, memory_space=pltpu.SMEM)
    return pl.pallas_call(
        _combine_kernel,
        grid=(n_steps,),
        in_specs=[
            smem((SUB, 1, N_CHUNKS + 1), lambda i: (i, 0, 0)),
            smem((SUB, 1, N_CHUNKS + 1), lambda i: (jnp.minimum(i + 1, n_steps - 1), 0, 0)),
            pl.BlockSpec(memory_space=pl.ANY),
            row(SLOT_ROWS), row(SLOT_ROWS),
            pl.BlockSpec((SUB, 1, N_EXPERTS), lambda i: (i, 0, 0)),
            row(D_MODEL),
            pl.BlockSpec((None, None, 6, D_MODEL), lambda i: (layer, seg(i), 0, 0)),
            pl.BlockSpec((1, D_MODEL), lambda i: (0, 0)),
        ],
        out_specs=row(D_MODEL),
        out_shape=jax.ShapeDtypeStruct((n_tiles * TM, D_MODEL), f32),
        scratch_shapes=[pltpu.VMEM((2 * SUB, LOCAL_ROWS, D_MODEL), f32),
                        pltpu.SemaphoreType.DMA((2 * SUB,))],
        compiler_params=pltpu.CompilerParams(
            dimension_semantics=("arbitrary",), vmem_limit_bytes=VMEM_LIMIT),
        name=f"combine{layer}",
    )(tab, tab, y, ti, tw, loffr, xs, mods, g)


def _moe(f, tit, ti, tw, cnt, xs, mods, layer, g, w1, b1, w2, b2, n_tiles):
    n_tiles_max = _ffn_tiles_max(n_tiles)
    tab, tail, loffr, loffc, te, na = _route_tables(cnt, n_tiles_max)
    xs_sorted = _dispatch(f, tit, tab, tail, loffc, layer, n_tiles, n_tiles_max)
    y = _moe_ffn(xs_sorted, te, na, w1, b1, w2, b2, layer, n_tiles_max)
    return _combine(y, tab, ti, tw, loffr, xs, mods, layer, g, n_tiles)


def kernel(x, c, ctx, c_ctx, w_ada, b_ada, norm_g, a_w_qkv, a_w_o, a_sink, b_w_qkv, b_q_norm, b_k_norm,
           b_w_o, moe_w_router, moe_b_router, moe_w1, moe_b1, moe_w2, moe_b2):
    assert DEPTH == 2 and x.shape == (BATCH, SEQ, D_MODEL) and ctx.shape == (BATCH, CTX_LEN, D_MODEL)
    x_lat, x_ctx = x.reshape(T_LAT, D_MODEL), ctx.reshape(T_CTX, D_MODEL)
    c_all = jnp.concatenate(
        [c, c_ctx[None, :], jnp.zeros((MOD_ROWS - BATCH - 1, D_MODEL), f32)], axis=0)
    mods = _adaln(c_all, w_ada, b_ada).reshape(DEPTH, MOD_ROWS, 6, D_MODEL)
    ones = jnp.ones((1, LANES), f32)
    g = lambda i, j: norm_g[i, j][None, :]

    q, k, v = _qkv(x_lat, x_ctx, mods, 0, g(0, 0), a_w_qkv[0].astype(bf16), _rope_tables(HD_A), ones, ones,
                   hq=HQ_A, hkv=HKV_A, hd=HD_A, qk_norm=False, v_ones=False, tm=TMP)
    o = _attn_a(q, k, v, a_sink[0])
    xs, *routed = _post_attn(o, a_w_o[0].astype(bf16), x_lat, x_ctx, mods, 0, g(0, 1), g(0, 2),
                             moe_w_router[0], moe_b_router[0], ALL_TILES)
    xs = _moe(*routed, xs, mods, 0, g(0, 3), moe_w1, moe_b1, moe_w2, moe_b2, ALL_TILES)

    q, k, v = _qkv(xs, xs, mods, 1, g(1, 0), b_w_qkv[0].astype(bf16), _rope_tables(HD_B),
                   b_q_norm[0][None, :], b_k_norm[0][None, :],
                   hq=HQ_B, hkv=HKV_B, hd=HD_B, qk_norm=True, v_ones=True, tm=TM)
    o = _attn_b(q, k, v)
    xl, *routed = _post_attn(o, b_w_o[0].astype(bf16), xs, xs, mods, 1, g(1, 1), g(1, 2),
                             moe_w_router[1], moe_b_router[1], LAT_TILES)
    xl = _moe(*routed, xl, mods, 1, g(1, 3), moe_w1, moe_b1, moe_w2, moe_b2, LAT_TILES)
    return xl.reshape(BATCH, SEQ, D_MODEL)
```

```python
import functools

import jax
import jax.numpy as jnp
from jax import lax
from jax.experimental import pallas as pl
from jax.experimental.pallas import tpu as pltpu

D_MODEL = 1024
BATCH = 8
SEQ = 2048
DEPTH = 2
GRID_W = 64
CTX_LEN = 256
BLOCK = 128
WINDOW = 128
ROPE_BASE = 10000.0
EPS = 1e-6
HQ_A, HKV_A, HD_A = 16, 2, 64
HQ_B, HKV_B, HD_B = 8, 2, 128
N_EXPERTS = 32
TOP_K = 4
D_FF = D_MODEL
SWIGLU_LIMIT = 7.0
SWIGLU_ALPHA = 1.702

T_LAT = BATCH * SEQ
T_CTX = BATCH * CTX_LEN
T_ALL = T_LAT + T_CTX
LANES = 128
TM = 256
LAT_TILES = T_LAT // TM
ALL_TILES = T_ALL // TM
TILES_PER_SEQ = SEQ // TM
MOD_ROWS = 16
CTX_MOD_ROW = BATCH
TMP = 512
PAIR_STACK = 2
SUB = 2
BQ_B = 512
TME = 512
SLOT_ROWS = 8
RUN_ALIGN = 8
LOCAL_ROWS = -(-(TM * TOP_K + N_EXPERTS * (RUN_ALIGN - 1)) // LANES) * LANES
N_CHUNKS = LOCAL_ROWS // RUN_ALIGN
MIN_CHUNKS = TM * TOP_K // RUN_ALIGN
NEG = -1e30
LOG2E = 1.4426950408889634
NT_DIMS = (((1,), (1,)), ((), ()))
VMEM_LIMIT = 56 * 1024 * 1024

f32 = jnp.float32
bf16 = jnp.bfloat16


def _seg_of_tile(i, tiles_per_seq, lat_tiles):
    return jnp.where(i < lat_tiles, i // tiles_per_seq, CTX_MOD_ROW)


def _adaln_kernel(c_ref, w_ref, b_ref, o_ref):
    c = c_ref[...]
    s = c * jax.nn.sigmoid(c)
    o_ref[...] = jnp.dot(s, w_ref[...], precision=lax.Precision.HIGHEST,
                         preferred_element_type=f32) + b_ref[...]


def _adaln(c_all, w_ada, b_ada):
    tn = 1536
    return pl.pallas_call(
        _adaln_kernel,
        grid=(DEPTH, 6 * D_MODEL // tn),
        in_specs=[
            pl.BlockSpec((MOD_ROWS, D_MODEL), lambda l, j: (0, 0)),
            pl.BlockSpec((None, D_MODEL, tn), lambda l, j: (l, 0, j)),
            pl.BlockSpec((None, 1, tn), lambda l, j: (l, 0, j)),
        ],
        out_specs=pl.BlockSpec((None, MOD_ROWS, tn), lambda l, j: (l, 0, j)),
        out_shape=jax.ShapeDtypeStruct((DEPTH, MOD_ROWS, 6 * D_MODEL), f32),
        compiler_params=pltpu.CompilerParams(
            dimension_semantics=("arbitrary", "arbitrary"), vmem_limit_bytes=VMEM_LIMIT),
        name="adaln",
    )(c_all, w_ada, b_ada.reshape(DEPTH, 1, 6 * D_MODEL))


def _rms(x, g):
    return x * lax.rsqrt(jnp.mean(x * x, axis=-1, keepdims=True) + EPS) * g


def _stream_specs(tm, ctx_base):
    lat_tiles = T_LAT // tm
    return [pl.BlockSpec((tm, D_MODEL), lambda i, *_: (jnp.minimum(i, lat_tiles - 1), 0)),
            pl.BlockSpec((tm, D_MODEL), lambda i, *_: (jnp.maximum(i, lat_tiles) - ctx_base, 0))]


def _stream_tile(x_ref, xc_ref):
    lat_tiles = T_LAT // x_ref.shape[0]
    return jnp.where(pl.program_id(0) < lat_tiles, x_ref[...], xc_ref[...])


def _qkv_kernel(x_ref, xc_ref, mod_ref, g_ref, w_ref, cos_ref, sa_ref, sb_ref, qn_ref, kn_ref,
                q_ref, k_ref, v_ref, *, nq, nk, hd, qk_norm, v_ones):
    h = _rms(_stream_tile(x_ref, xc_ref), g_ref[...]) * (1.0 + mod_ref[1:2, :]) + mod_ref[0:1, :]
    qkv = jnp.dot(h.astype(bf16), w_ref[...], preferred_element_type=f32)
    cos, sa, sb = cos_ref[...], sa_ref[...], sb_ref[...]
    quarter = hd // 4
    scale = hd ** -0.5 * LOG2E

    def rope(c):
        return c * cos + pltpu.roll(c, quarter, 1) * sa + pltpu.roll(c, LANES - quarter, 1) * sb

    for j in range(nq // LANES):
        c = qkv[:, j * LANES:(j + 1) * LANES]
        if qk_norm:
            c = _rms(c, qn_ref[...])
        q_ref[:, j * LANES:(j + 1) * LANES] = (rope(c) * scale).astype(bf16)
    for j in range(nk // LANES):
        c = qkv[:, nq + j * LANES:nq + (j + 1) * LANES]
        if qk_norm:
            c = _rms(c, kn_ref[...])
        k_ref[:, j * LANES:(j + 1) * LANES] = rope(c).astype(bf16)
    if v_ones:
        for j in range(nk // LANES):
            v_ref[:, 2 * j * LANES:(2 * j + 1) * LANES] = (
                qkv[:, nq + nk + j * LANES:nq + nk + (j + 1) * LANES].astype(bf16))
            v_ref[:, (2 * j + 1) * LANES:(2 * j + 2) * LANES] = jnp.ones((x_ref.shape[0], LANES), bf16)
    else:
        v_ref[...] = qkv[:, nq + nk:].astype(bf16)


def _qkv(x_lat, x_ctx, mods, layer, g, w_bf16, tabs, qn, kn, *, hq, hkv, hd, qk_norm, v_ones, tm):
    ctx_base = 0 if x_ctx is x_lat else T_LAT // tm
    nq, nk = hq * hd, hkv * hd
    nv = 2 * nk if v_ones else nk
    per_seq, lat_tiles = SEQ // tm, T_LAT // tm
    seg = functools.partial(_seg_of_tile, tiles_per_seq=per_seq, lat_tiles=lat_tiles)
    tab_idx = lambda i: (jnp.where(i < lat_tiles, i % per_seq, per_seq), 0)
    tab_spec = pl.BlockSpec((tm, LANES), tab_idx)
    row = lambda n: pl.BlockSpec((tm, n), lambda i: (i, 0))
    return pl.pallas_call(
        functools.partial(_qkv_kernel, nq=nq, nk=nk, hd=hd, qk_norm=qk_norm, v_ones=v_ones),
        grid=(T_ALL // tm,),
        in_specs=[
            *_stream_specs(tm, ctx_base),
            pl.BlockSpec((None, None, 6, D_MODEL), lambda i: (layer, seg(i), 0, 0)),
            pl.BlockSpec((1, D_MODEL), lambda i: (0, 0)),
            pl.BlockSpec((D_MODEL, nq + 2 * nk), lambda i: (0, 0)),
            tab_spec, tab_spec, tab_spec,
            pl.BlockSpec((1, LANES), lambda i: (0, 0)),
            pl.BlockSpec((1, LANES), lambda i: (0, 0)),
        ],
        out_specs=[row(nq), row(nk), row(nv)],
        out_shape=[jax.ShapeDtypeStruct((T_ALL, nq), bf16),
                   jax.ShapeDtypeStruct((T_ALL, nk), bf16),
                   jax.ShapeDtypeStruct((T_ALL, nv), bf16)],
        compiler_params=pltpu.CompilerParams(
            dimension_semantics=("arbitrary",), vmem_limit_bytes=VMEM_LIMIT),
        name=f"qkv{layer}",
    )(x_lat, x_ctx, mods, g, w_bf16, *tabs, qn, kn)


def _rope_tables(hd):
    quarter = hd // 4
    inv_freq = jnp.float32(ROPE_BASE) ** (-jnp.arange(quarter, dtype=f32) / quarter)
    t = jnp.arange(SEQ)
    ang_r = (t // GRID_W).astype(f32)[:, None] * inv_freq[None, :]
    ang_c = (t % GRID_W).astype(f32)[:, None] * inv_freq[None, :]
    z = jnp.zeros_like(ang_r)
    cos = jnp.concatenate([jnp.cos(ang_r)] * 2 + [jnp.cos(ang_c)] * 2, axis=-1)
    sa = jnp.concatenate([z, jnp.sin(ang_r), z, jnp.sin(ang_c)], axis=-1)
    sb = jnp.concatenate([-jnp.sin(ang_r), z, -jnp.sin(ang_c), z], axis=-1)
    rep = LANES // hd
    pad = lambda a, v: jnp.concatenate(
        [jnp.tile(a, (1, rep)), jnp.full((TMP, LANES), v, f32)], axis=0)
    return pad(cos, 1.0), pad(sa, 0.0), pad(sb, 0.0)


def _pair_operand(x, g):
    lane = lax.broadcasted_iota(jnp.int32, x.shape, 1)
    swapped = pltpu.roll(x, HD_A, 1)
    lo_src, hi_src = (x, swapped) if g == 0 else (swapped, x)
    lo = jnp.where(lane < HD_A, lo_src, 0.0)
    hi = jnp.where(lane >= HD_A, hi_src, 0.0)
    return jnp.concatenate([lo, hi], axis=0)


def _attend_pairs(q_ref, o_ref, sink_ref, kcat, vcat, mask):
    n = kcat.shape[0]
    pairs = PAIR_STACK
    rows = pairs * BLOCK
    lane = lax.broadcasted_iota(jnp.int32, (rows, LANES), 1)
    pair_of_row = lax.broadcasted_iota(jnp.int32, (rows, 1), 0) // BLOCK
    ind_row = lax.broadcasted_iota(jnp.int32, (2 * n, LANES), 0)
    ind_lane = lax.broadcasted_iota(jnp.int32, (2 * n, LANES), 1)
    ind = jnp.where(ind_lane == ind_row // n, 1.0, 0.0)
    if mask is not None:
        mask = jnp.concatenate([mask] * pairs, axis=0)
    operands = [(_pair_operand(kcat, g).astype(bf16),
                 jnp.concatenate([_pair_operand(vcat, g), ind], axis=1).astype(bf16))
                for g in range(HKV_A)]
    pairs_per_group = HQ_A // HKV_A // 2
    for c0 in range(0, HQ_A // 2, pairs):
        kp, vp = operands[c0 // pairs_per_group]
        cols = [(c0 + p) * LANES for p in range(pairs)]
        qs = jnp.concatenate([q_ref[:, c:c + LANES] for c in cols], axis=0)
        s = lax.dot_general(qs, kp, NT_DIMS, preferred_element_type=f32)
        es, ms, sks = [], [], []
        for hh in range(2):
            sh = s[:, hh * n:(hh + 1) * n]
            if mask is not None:
                sh = jnp.where(mask, sh, NEG)
            sk = jnp.zeros((rows, 1), f32)
            for p in range(pairs):
                sk = jnp.where(pair_of_row == p, sink_ref[cols[p] // HD_A + hh] * LOG2E, sk)
            m = jnp.maximum(jnp.max(sh, axis=-1, keepdims=True), sk)
            es.append(jnp.exp2(sh - m).astype(bf16))
            ms.append(m)
            sks.append(sk)
        oe = jnp.dot(jnp.concatenate(es, axis=1), vp, preferred_element_type=f32)
        inv = [1.0 / (oe[:, LANES + hh:LANES + hh + 1] + jnp.exp2(sks[hh] - ms[hh])) for hh in range(2)]
        o = oe[:, :LANES] * jnp.where(lane < HD_A, inv[0], inv[1])
        for p in range(pairs):
            o_ref[:, cols[p]:cols[p] + LANES] = o[p * BLOCK:(p + 1) * BLOCK].astype(bf16)


def _attn_a_kernel(sink_ref, q_ref, k_ref, v_ref, kc_ref, vc_ref, o_ref):
    j = pl.program_id(1)
    nblk = SEQ // BLOCK
    kc = kc_ref[...].astype(f32)
    vc = vc_ref[...].astype(f32)

    @pl.when(j < nblk)
    def _():
        wlen = 3 * BLOCK
        s0 = pl.multiple_of(jnp.clip((j - 1) * BLOCK, 0, SEQ - wlen), BLOCK)
        kcat = jnp.concatenate([k_ref[pl.ds(s0, wlen), :].astype(f32), kc], axis=0)
        vcat = jnp.concatenate([v_ref[pl.ds(s0, wlen), :].astype(f32), vc], axis=0)
        qpos = j * BLOCK + lax.broadcasted_iota(jnp.int32, (BLOCK, wlen + CTX_LEN), 0)
        col = lax.broadcasted_iota(jnp.int32, (BLOCK, wlen + CTX_LEN), 1)
        mask = (jnp.abs(qpos - (s0 + col)) <= WINDOW) | (col >= wlen)
        _attend_pairs(q_ref, o_ref, sink_ref, kcat, vcat, mask)

    @pl.when(j >= nblk)
    def _():
        _attend_pairs(q_ref, o_ref, sink_ref, kc, vc, None)


def _attn_a(q, k, v, sink):
    nblk = SEQ // BLOCK
    cblk = CTX_LEN // BLOCK
    nk = HKV_A * HD_A
    qrow = lambda b, j, s: (jnp.where(j < nblk, b * nblk + j, T_LAT // BLOCK + b * cblk + (j - nblk)), 0)
    lat_kv = pl.BlockSpec((SEQ, nk), lambda b, j, s: (b, 0))
    ctx_kv = pl.BlockSpec((CTX_LEN, nk), lambda b, j, s: (T_LAT // CTX_LEN + b, 0))
    return pl.pallas_call(
        _attn_a_kernel,
        grid_spec=pltpu.PrefetchScalarGridSpec(
            num_scalar_prefetch=1,
            grid=(BATCH, nblk + cblk),
            in_specs=[pl.BlockSpec((BLOCK, HQ_A * HD_A), qrow), lat_kv, lat_kv, ctx_kv, ctx_kv],
            out_specs=pl.BlockSpec((BLOCK, HQ_A * HD_A), qrow),
        ),
        out_shape=jax.ShapeDtypeStruct((T_ALL, HQ_A * HD_A), bf16),
        compiler_params=pltpu.CompilerParams(
            dimension_semantics=("arbitrary", "arbitrary"), vmem_limit_bytes=VMEM_LIMIT),
        name="attn_a",
    )(sink, q, k, v, k, v)


def _attn_b_kernel(q_ref, k_ref, v_ref, kc_ref, vc_ref, o_ref):
    rep = HQ_B // HKV_B
    for h in range(HQ_B):
        g = h // rep
        hs = slice(h * HD_B, (h + 1) * HD_B)
        gs = slice(g * HD_B, (g + 1) * HD_B)
        vs = slice(2 * g * HD_B, 2 * (g + 1) * HD_B)
        q = q_ref[:, hs]
        s1 = lax.dot_general(q, k_ref[:, gs], NT_DIMS, preferred_element_type=f32)
        s2 = lax.dot_general(q, kc_ref[:, gs], NT_DIMS, preferred_element_type=f32)
        m = jnp.maximum(jnp.max(s1, axis=-1, keepdims=True), jnp.max(s2, axis=-1, keepdims=True))
        e1 = jnp.exp2(s1 - m).astype(bf16)
        e2 = jnp.exp2(s2 - m).astype(bf16)
        oe = (jnp.dot(e1, v_ref[:, vs], preferred_element_type=f32)
              + jnp.dot(e2, vc_ref[:, vs], preferred_element_type=f32))
        o_ref[:, hs] = (oe[:, :HD_B] * (1.0 / oe[:, HD_B:HD_B + 1])).astype(bf16)


def _attn_b(q, k, v):
    nblk = SEQ // BQ_B
    nk = HKV_B * HD_B
    qrow = lambda b, j: (b * nblk + j, 0)
    lat = lambda n: pl.BlockSpec((SEQ, n), lambda b, j: (b, 0))
    ctx = lambda n: pl.BlockSpec((CTX_LEN, n), lambda b, j: (T_LAT // CTX_LEN + b, 0))
    return pl.pallas_call(
        _attn_b_kernel,
        grid=(BATCH, nblk),
        in_specs=[pl.BlockSpec((BQ_B, HQ_B * HD_B), qrow), lat(nk), lat(2 * nk), ctx(nk), ctx(2 * nk)],
        out_specs=pl.BlockSpec((BQ_B, HQ_B * HD_B), qrow),
        out_shape=jax.ShapeDtypeStruct((T_LAT, HQ_B * HD_B), bf16),
        compiler_params=pltpu.CompilerParams(
            dimension_semantics=("arbitrary", "arbitrary"), vmem_limit_bytes=VMEM_LIMIT),
        name="attn_b",
    )(q, k, v, k, v)


def _post_attn_kernel(o_ref, wo_ref, x_ref, xc_ref, mod_ref, g1_ref, g2_ref, wrt_ref, brt_ref,
                      xn_ref, f_ref, tit_ref, ti_ref, tw_ref, cnt_ref):
    x_in = _stream_tile(x_ref, xc_ref)
    halves = []
    for h in range(TMP // TM):
        r = slice(h * TM, (h + 1) * TM)
        a = jnp.dot(o_ref[r, :], wo_ref[...], preferred_element_type=f32)
        x = x_in[r] + mod_ref[2:3, :] * _rms(a, g1_ref[...])
        xn_ref[r, :] = x
        fh = _rms(x, g2_ref[...]) * (1.0 + mod_ref[4:5, :]) + mod_ref[3:4, :]
        f_ref[r, :] = fh.astype(bf16)
        halves.append(fh)
    f = jnp.concatenate(halves, axis=0)
    logits = lax.dot_general(wrt_ref[...], f, NT_DIMS, precision=lax.Precision.HIGHEST,
                             preferred_element_type=f32) + brt_ref[...]
    sub = lax.broadcasted_iota(jnp.int32, logits.shape, 0)
    rest = logits
    top_v, top_i = [], []
    for _ in range(TOP_K):
        m = jnp.max(rest, axis=0, keepdims=True)
        idx = jnp.min(jnp.where(rest == m, sub, N_EXPERTS), axis=0, keepdims=True)
        top_v.append(m)
        top_i.append(idx)
        rest = jnp.where(sub == idx, -jnp.inf, rest)
    es = [jnp.exp(v - top_v[0]) for v in top_v]
    inv = 1.0 / (es[0] + es[1] + es[2] + es[3])
    k8 = lax.broadcasted_iota(jnp.int32, (SLOT_ROWS, TMP), 0)
    tit = jnp.full((SLOT_ROWS, TMP), -1.0, f32)
    twt = jnp.zeros((SLOT_ROWS, TMP), f32)
    for k in range(TOP_K):
        tit = jnp.where(k8 == k, top_i[k].astype(f32), tit)
        twt = jnp.where(k8 == k, es[k] * inv, twt)
    tit_ref[...] = tit
    eye = (lax.broadcasted_iota(jnp.int32, (TM, TM), 0)
           == lax.broadcasted_iota(jnp.int32, (TM, TM), 1)).astype(f32)
    to_rows = lambda t: lax.dot_general(eye, t, NT_DIMS, precision=lax.Precision.HIGHEST,
                                        preferred_element_type=f32)
    lane = lax.broadcasted_iota(jnp.int32, (TM, N_EXPERTS), 1).astype(f32)
    for h in range(TMP // TM):
        cols = slice(h * TM, (h + 1) * TM)
        ti = to_rows(tit[:, cols])
        ti_ref[cols, :] = ti
        tw_ref[cols, :] = to_rows(twt[:, cols])
        cnt = jnp.zeros((TM, N_EXPERTS), f32)
        for k in range(TOP_K):
            cnt = cnt + (ti[:, k:k + 1] == lane).astype(f32)
        cnt_ref[h] = jnp.sum(cnt, axis=0, keepdims=True)


def _post_attn(o, wo_bf16, x_lat, x_ctx, mods, layer, g1, g2, wr, br, n_tiles):
    ctx_base = 0 if x_ctx is x_lat else T_LAT // TMP
    seg = functools.partial(_seg_of_tile, tiles_per_seq=SEQ // TMP, lat_tiles=T_LAT // TMP)
    rows = n_tiles * TM
    row = lambda n: pl.BlockSpec((TMP, n), lambda i: (i, 0))
    const = lambda a, b: pl.BlockSpec((a, b), lambda i: (0, 0))
    return pl.pallas_call(
        _post_attn_kernel,
        grid=(rows // TMP,),
        in_specs=[
            row(D_MODEL), const(D_MODEL, D_MODEL), *_stream_specs(TMP, ctx_base),
            pl.BlockSpec((None, None, 6, D_MODEL), lambda i: (layer, seg(i), 0, 0)),
            const(1, D_MODEL), const(1, D_MODEL), const(N_EXPERTS, D_MODEL), const(N_EXPERTS, 1),
        ],
        out_specs=[row(D_MODEL), row(D_MODEL),
                   pl.BlockSpec((SLOT_ROWS, TMP), lambda i: (0, i)),
                   row(SLOT_ROWS), row(SLOT_ROWS),
                   pl.BlockSpec((TMP // TM, 1, N_EXPERTS), lambda i: (i, 0, 0))],
        out_shape=[jax.ShapeDtypeStruct((rows, D_MODEL), f32),
                   jax.ShapeDtypeStruct((rows, D_MODEL), bf16),
                   jax.ShapeDtypeStruct((SLOT_ROWS, rows), f32),
                   jax.ShapeDtypeStruct((rows, SLOT_ROWS), f32),
                   jax.ShapeDtypeStruct((rows, SLOT_ROWS), f32),
                   jax.ShapeDtypeStruct((n_tiles, 1, N_EXPERTS), f32)],
        compiler_params=pltpu.CompilerParams(
            dimension_semantics=("arbitrary",), vmem_limit_bytes=VMEM_LIMIT),
        name=f"post_attn{layer}",
    )(o, wo_bf16, x_lat, x_ctx, mods, g1, g2, wr.T, br.reshape(N_EXPERTS, 1))


def _ffn_tiles_max(n_tiles):
    rows = n_tiles * (TM * TOP_K + N_EXPERTS * (RUN_ALIGN - 1)) + N_EXPERTS * (TME - RUN_ALIGN)
    return -(-rows // TME)


def _route_tables(cnt, n_tiles_max):
    n = cnt.reshape(-1, N_EXPERTS).astype(jnp.int32)
    run = (n + RUN_ALIGN - 1) // RUN_ALIGN * RUN_ALIGN
    loff = jnp.cumsum(run, axis=1) - run
    tot = jnp.sum(run, axis=0)
    gsz = (tot + TME - 1) // TME * TME
    ends = jnp.cumsum(gsz)
    goff = (ends - gsz)[None, :] + jnp.cumsum(run, axis=0) - run
    c0 = jnp.arange(N_CHUNKS, dtype=jnp.int32) * RUN_ALIGN
    owner = jnp.sum(c0[None, :, None] >= (loff + run)[:, None, :], axis=2)
    shift = jnp.where(owner[:, :, None] == jnp.arange(N_EXPERTS)[None, None, :],
                      (goff - loff)[:, None, :], 0)
    tab = jnp.concatenate([jnp.sum(shift, axis=2) + c0[None, :],
                           jnp.sum(run, axis=1, keepdims=True) // RUN_ALIGN], axis=1)[:, None, :]
    n_active = ends[-1] // TME
    tail = jnp.concatenate([ends - gsz + tot, (gsz - tot) // RUN_ALIGN, n_active[None]])[None, :]
    tile = jnp.arange(n_tiles_max, dtype=jnp.int32)
    te = jnp.sum((jnp.minimum(tile, n_active - 1) * TME)[:, None] >= ends[None, :], axis=1)
    of_tile = te[:, None] == jnp.arange(N_EXPERTS)[None, :]
    used = jnp.sum(jnp.where(of_tile, (ends - gsz + tot)[None, :], 0), axis=1) - tile * TME
    half = (used <= TME // 2) & (tile < n_active)
    loff_f = loff.astype(f32)
    return (tab, tail, loff_f[:, None, :], loff_f[:, :, None], te.astype(jnp.int32),
            half.astype(jnp.int32), n_active.reshape(1).astype(jnp.int32))


def _local_chunk(c):
    if isinstance(c, int):
        return pl.ds(c * RUN_ALIGN, RUN_ALIGN)
    return pl.ds(pl.multiple_of(c * RUN_ALIGN, RUN_ALIGN), RUN_ALIGN)


def _global_chunk(ctab_ref, h, c):
    return pl.ds(pl.multiple_of(ctab_ref[h, 0, c], RUN_ALIGN), RUN_ALIGN)


def _start_chunks(ctab_ref, h, copy):
    for c in range(MIN_CHUNKS):
        copy(c).start()
    lax.fori_loop(MIN_CHUNKS, ctab_ref[h, 0, N_CHUNKS], lambda c, carry: (copy(c).start(), carry)[1], 0)


def _wait_chunks(ctab_ref, h, copy, bulk):
    bulk.wait()
    lax.fori_loop(MIN_CHUNKS, ctab_ref[h, 0, N_CHUNKS], lambda c, carry: (copy(c).wait(), carry)[1], 0)


def _chunk(base, c):
    return pl.ds(pl.multiple_of(base + c * RUN_ALIGN, RUN_ALIGN), RUN_ALIGN)


def _dispatch_kernel(ctab_ref, ctab_prev_ref, tail_ref, f_ref, tit_ref, loffc_ref, xs_hbm,
                     lbuf, zbuf, sem, zsem):
    i = pl.program_id(0)
    slot = i % 2
    sub = lax.broadcasted_iota(jnp.int32, (N_EXPERTS, TM), 0).astype(f32)
    before = (lax.broadcasted_iota(jnp.int32, (TM, TM), 0)
              < lax.broadcasted_iota(jnp.int32, (TM, TM), 1)).astype(bf16)
    rows = lax.broadcasted_iota(jnp.int32, (LOCAL_ROWS, TM), 0).astype(f32)

    def chunk_copy(ctab, h, b):
        return lambda c: pltpu.make_async_copy(
            lbuf.at[b, _local_chunk(c)], xs_hbm.at[_global_chunk(ctab, h, c)], sem.at[b])

    def bulk_copy(b):
        n = MIN_CHUNKS * RUN_ALIGN
        return pltpu.make_async_copy(lbuf.at[b, 0:n], xs_hbm.at[0:n], sem.at[b])

    def wait_tiles(ctab, s):
        for h in range(SUB):
            b = s * SUB + h
            _wait_chunks(ctab, h, chunk_copy(ctab, h, b), bulk_copy(b))

    for h in range(SUB):
        tit = tit_ref[:, h * TM:(h + 1) * TM]
        hit = [tit[k:k + 1, :] == sub for k in range(TOP_K)]
        cnt = sum(x.astype(f32) for x in hit)
        base = loffc_ref[h] + jnp.dot(cnt.astype(bf16), before, preferred_element_type=f32)
        onehot = jnp.zeros((LOCAL_ROWS, TM), f32)
        for k in range(TOP_K):
            lp = jnp.sum(jnp.where(hit[k], base, 0.0), axis=0, keepdims=True)
            onehot = onehot + (rows == lp).astype(f32)
        b = slot * SUB + h
        lbuf[b] = jnp.dot(onehot.astype(bf16), f_ref[h * TM:(h + 1) * TM, :], preferred_element_type=f32)
        _start_chunks(ctab_ref, h, chunk_copy(ctab_ref, h, b))

    @pl.when(i > 0)
    def _():
        wait_tiles(ctab_prev_ref, 1 - slot)

    tail_copy = lambda e, c: pltpu.make_async_copy(zbuf, xs_hbm.at[_chunk(tail_ref[0, e], c)], zsem)

    def for_each_tail(fn):
        for e in range(N_EXPERTS):
            def body(c, carry, e=e):
                fn(e, c)
                return carry
            lax.fori_loop(0, tail_ref[0, N_EXPERTS + e], body, 0)

    @pl.when(i == pl.num_programs(0) - 1)
    def _():
        zbuf[...] = jnp.zeros_like(zbuf)
        for_each_tail(lambda e, c: tail_copy(e, c).start())
        for_each_tail(lambda e, c: tail_copy(e, c).wait())
        wait_tiles(ctab_ref, slot)
        lbuf[0, 0:TME, :] = jnp.zeros((TME, D_MODEL), f32)
        n_active = tail_ref[0, 2 * N_EXPERTS]
        n_spare = xs_hbm.shape[0] // TME - n_active
        spare_copy = lambda j: pltpu.make_async_copy(
            lbuf.at[0, 0:TME], xs_hbm.at[pl.ds(pl.multiple_of((n_active + j) * TME, TME), TME)], zsem)
        lax.fori_loop(0, n_spare, lambda j, carry: (spare_copy(j).start(), carry)[1], 0)
        lax.fori_loop(0, n_spare, lambda j, carry: (spare_copy(j).wait(), carry)[1], 0)


def _dispatch(f, tit, tab, tail, loffc, layer, n_tiles, n_tiles_max):
    smem = functools.partial(pl.BlockSpec, memory_space=pltpu.SMEM)
    return pl.pallas_call(
        _dispatch_kernel,
        grid=(n_tiles // SUB,),
        in_specs=[
            smem((SUB, 1, N_CHUNKS + 1), lambda i: (i, 0, 0)),
            smem((SUB, 1, N_CHUNKS + 1), lambda i: (jnp.maximum(i - 1, 0), 0, 0)),
            smem((1, 2 * N_EXPERTS + 1), lambda i: (0, 0)),
            pl.BlockSpec((SUB * TM, D_MODEL), lambda i: (i, 0)),
            pl.BlockSpec((SLOT_ROWS, SUB * TM), lambda i: (0, i)),
            pl.BlockSpec((SUB, N_EXPERTS, 1), lambda i: (i, 0, 0)),
        ],
        out_specs=pl.BlockSpec(memory_space=pl.ANY),
        out_shape=jax.ShapeDtypeStruct((n_tiles_max * TME, D_MODEL), f32),
        scratch_shapes=[pltpu.VMEM((2 * SUB, LOCAL_ROWS, D_MODEL), f32),
                        pltpu.VMEM((RUN_ALIGN, D_MODEL), f32),
                        pltpu.SemaphoreType.DMA((2 * SUB,)), pltpu.SemaphoreType.DMA(())],
        compiler_params=pltpu.CompilerParams(
            dimension_semantics=("arbitrary",), vmem_limit_bytes=VMEM_LIMIT),
        name=f"dispatch{layer}",
    )(tab, tab, tail, f, tit, loffc)


def _moe_ffn_kernel(te_ref, half_ref, na_ref, x_ref, w1_ref, b1_ref, w2_ref, b2_ref, y_ref, w1b, w2b):
    i = pl.program_id(0)
    na = na_ref[0]

    def ffn(rows):
        u = jnp.dot(x_ref[rows, :].astype(bf16), w1b[...], preferred_element_type=f32) + b1_ref[...]
        glu = jnp.minimum(u[:, :D_FF], SWIGLU_LIMIT)
        lin = jnp.clip(u[:, D_FF:], -SWIGLU_LIMIT, SWIGLU_LIMIT)
        act = glu * jax.nn.sigmoid(SWIGLU_ALPHA * glu) * (lin + 1.0)
        y_ref[rows, :] = jnp.dot(act.astype(bf16), w2b[...], preferred_element_type=f32) + b2_ref[...]

    @pl.when(i >= na)
    def _():
        y_ref[...] = jnp.zeros_like(y_ref)

    @pl.when(i < na)
    def _():
        @pl.when((i == 0) | (te_ref[i] != te_ref[jnp.maximum(i - 1, 0)]))
        def _():
            w1b[...] = w1_ref[...].astype(bf16)
            w2b[...] = w2_ref[...].astype(bf16)

        @pl.when(half_ref[i] == 0)
        def _():
            ffn(slice(0, TME))

        @pl.when(half_ref[i] != 0)
        def _():
            ffn(slice(0, TME // 2))
            y_ref[TME // 2:, :] = jnp.zeros((TME // 2, D_MODEL), f32)


def _moe_ffn(xs_sorted, te, half, na, w1, b1, w2, b2, layer, n_tiles_max):
    ex = lambda i, te, half, na: (layer, te[i], 0, 0)
    return pl.pallas_call(
        _moe_ffn_kernel,
        grid_spec=pltpu.PrefetchScalarGridSpec(
            num_scalar_prefetch=3,
            grid=(n_tiles_max,),
            in_specs=[
                pl.BlockSpec((TME, D_MODEL),
                             lambda i, te, half, na: (jnp.maximum(jnp.minimum(i, na[0] - 1), 0), 0)),
                pl.BlockSpec((None, None, D_MODEL, 2 * D_FF), ex),
                pl.BlockSpec((None, None, 1, 2 * D_FF), ex),
                pl.BlockSpec((None, None, D_FF, D_MODEL), ex),
                pl.BlockSpec((None, None, 1, D_MODEL), ex),
            ],
            out_specs=pl.BlockSpec((TME, D_MODEL), lambda i, te, half, na: (i, 0)),
            scratch_shapes=[pltpu.VMEM((D_MODEL, 2 * D_FF), bf16), pltpu.VMEM((D_FF, D_MODEL), bf16)],
        ),
        out_shape=jax.ShapeDtypeStruct((n_tiles_max * TME, D_MODEL), f32),
        compiler_params=pltpu.CompilerParams(
            dimension_semantics=("arbitrary",), vmem_limit_bytes=VMEM_LIMIT),
        name=f"moe_ffn{layer}",
    )(te, half, na, xs_sorted, w1, b1.reshape(DEPTH, N_EXPERTS, 1, 2 * D_FF), w2,
      b2.reshape(DEPTH, N_EXPERTS, 1, D_MODEL))


def _combine_kernel(ctab_ref, ctab_next_ref, y_hbm, ti_ref, tw_ref, loffr_ref, x_ref, mod_ref, g_ref,
                    xn_ref, ybuf, sem):
    i = pl.program_id(0)
    slot = i % 2

    def chunk_copy(ctab, h, b):
        return lambda c: pltpu.make_async_copy(
            y_hbm.at[_global_chunk(ctab, h, c)], ybuf.at[b, _local_chunk(c)], sem.at[b])

    def bulk_copy(b):
        n = MIN_CHUNKS * RUN_ALIGN
        return pltpu.make_async_copy(y_hbm.at[0:n], ybuf.at[b, 0:n], sem.at[b])

    def start_tiles(ctab, s):
        for h in range(SUB):
            _start_chunks(ctab, h, chunk_copy(ctab, h, s * SUB + h))

    @pl.when(i == 0)
    def _():
        ybuf[...] = jnp.zeros_like(ybuf)
        start_tiles(ctab_ref, 0)

    @pl.when(i + 1 < pl.num_programs(0))
    def _():
        start_tiles(ctab_next_ref, 1 - slot)

    lane = lax.broadcasted_iota(jnp.int32, (TM, N_EXPERTS), 1).astype(f32)
    before = (lax.broadcasted_iota(jnp.int32, (TM, TM), 1)
              < lax.broadcasted_iota(jnp.int32, (TM, TM), 0)).astype(bf16)
    cols = lax.broadcasted_iota(jnp.int32, (TM, LOCAL_ROWS), 1).astype(f32)
    for h in range(SUB):
        tok = slice(h * TM, (h + 1) * TM)
        b = slot * SUB + h
        ti = ti_ref[tok, :]
        tw = tw_ref[tok, :]
        hit = [ti[:, k:k + 1] == lane for k in range(TOP_K)]
        cnt = sum(x.astype(f32) for x in hit)
        base = loffr_ref[h] + jnp.dot(before, cnt.astype(bf16), preferred_element_type=f32)
        w = jnp.zeros((TM, LOCAL_ROWS), f32)
        for k in range(TOP_K):
            lp = jnp.sum(jnp.where(hit[k], base, 0.0), axis=1, keepdims=True)
            w = jnp.where(cols == lp, tw[:, k:k + 1], w)
        w_hi = w.astype(bf16)
        w_lo = (w - w_hi.astype(f32)).astype(bf16)
        _wait_chunks(ctab_ref, h, chunk_copy(ctab_ref, h, b), bulk_copy(b))
        yl = ybuf[b].astype(bf16)
        acc = (jnp.dot(w_hi, yl, preferred_element_type=f32)
               + jnp.dot(w_lo, yl, preferred_element_type=f32))
        xn_ref[tok, :] = x_ref[tok, :] + mod_ref[5:6, :] * _rms(acc, g_ref[...])


def _combine(y, tab, ti, tw, loffr, xs, mods, layer, g, n_tiles):
    rows = SUB * TM
    n_steps = n_tiles // SUB
    seg = functools.partial(_seg_of_tile, tiles_per_seq=SEQ // rows, lat_tiles=T_LAT // rows)
    row = lambda n: pl.BlockSpec((rows, n), lambda i: (i, 0))
    smem = functools.partial(pl.BlockSpec, memory_space=pltpu.SMEM)
    return pl.pallas_call(
        _combine_kernel,
        grid=(n_steps,),
        in_specs=[
            smem((SUB, 1, N_CHUNKS + 1), lambda i: (i, 0, 0)),
            smem((SUB, 1, N_CHUNKS + 1), lambda i: (jnp.minimum(i + 1, n_steps - 1), 0, 0)),
            pl.BlockSpec(memory_space=pl.ANY),
            row(SLOT_ROWS), row(SLOT_ROWS),
            pl.BlockSpec((SUB, 1, N_EXPERTS), lambda i: (i, 0, 0)),
            row(D_MODEL),
            pl.BlockSpec((None, None, 6, D_MODEL), lambda i: (layer, seg(i), 0, 0)),
            pl.BlockSpec((1, D_MODEL), lambda i: (0, 0)),
        ],
        out_specs=row(D_MODEL),
        out_shape=jax.ShapeDtypeStruct((n_tiles * TM, D_MODEL), f32),
        scratch_shapes=[pltpu.VMEM((2 * SUB, LOCAL_ROWS, D_MODEL), f32),
                        pltpu.SemaphoreType.DMA((2 * SUB,))],
        compiler_params=pltpu.CompilerParams(
            dimension_semantics=("arbitrary",), vmem_limit_bytes=VMEM_LIMIT),
        name=f"combine{layer}",
    )(tab, tab, y, ti, tw, loffr, xs, mods, g)


def _moe(f, tit, ti, tw, cnt, xs, mods, layer, g, w1, b1, w2, b2, n_tiles):
    n_tiles_max = _ffn_tiles_max(n_tiles)
    tab, tail, loffr, loffc, te, half, na = _route_tables(cnt, n_tiles_max)
    xs_sorted = _dispatch(f, tit, tab, tail, loffc, layer, n_tiles, n_tiles_max)
    y = _moe_ffn(xs_sorted, te, half, na, w1, b1, w2, b2, layer, n_tiles_max)
    return _combine(y, tab, ti, tw, loffr, xs, mods, layer, g, n_tiles)


def kernel(x, c, ctx, c_ctx, w_ada, b_ada, norm_g, a_w_qkv, a_w_o, a_sink, b_w_qkv, b_q_norm, b_k_norm,
           b_w_o, moe_w_router, moe_b_router, moe_w1, moe_b1, moe_w2, moe_b2):
    assert DEPTH == 2 and x.shape == (BATCH, SEQ, D_MODEL) and ctx.shape == (BATCH, CTX_LEN, D_MODEL)
    x_lat, x_ctx = x.reshape(T_LAT, D_MODEL), ctx.reshape(T_CTX, D_MODEL)
    c_all = jnp.concatenate(
        [c, c_ctx[None, :], jnp.zeros((MOD_ROWS - BATCH - 1, D_MODEL), f32)], axis=0)
    mods = _adaln(c_all, w_ada, b_ada).reshape(DEPTH, MOD_ROWS, 6, D_MODEL)
    ones = jnp.ones((1, LANES), f32)
    g = lambda i, j: norm_g[i, j][None, :]

    q, k, v = _qkv(x_lat, x_ctx, mods, 0, g(0, 0), a_w_qkv[0].astype(bf16), _rope_tables(HD_A), ones, ones,
                   hq=HQ_A, hkv=HKV_A, hd=HD_A, qk_norm=False, v_ones=False, tm=TMP)
    o = _attn_a(q, k, v, a_sink[0])
    xs, *routed = _post_attn(o, a_w_o[0].astype(bf16), x_lat, x_ctx, mods, 0, g(0, 1), g(0, 2),
                             moe_w_router[0], moe_b_router[0], ALL_TILES)
    xs = _moe(*routed, xs, mods, 0, g(0, 3), moe_w1, moe_b1, moe_w2, moe_b2, ALL_TILES)

    q, k, v = _qkv(xs, xs, mods, 1, g(1, 0), b_w_qkv[0].astype(bf16), _rope_tables(HD_B),
                   b_q_norm[0][None, :], b_k_norm[0][None, :],
                   hq=HQ_B, hkv=HKV_B, hd=HD_B, qk_norm=True, v_ones=True, tm=TM)
    o = _attn_b(q, k, v)
    xl, *routed = _post_attn(o, b_w_o[0].astype(bf16), xs, xs, mods, 1, g(1, 1), g(1, 2),
                             moe_w_router[1], moe_b_router[1], LAT_TILES)
    xl = _moe(*routed, xl, mods, 1, g(1, 3), moe_w1, moe_b1, moe_w2, moe_b2, LAT_TILES)
    return xl.reshape(BATCH, SEQ, D_MODEL)
```

```python
import functools

import jax
import jax.numpy as jnp
from jax import lax
from jax.experimental import pallas as pl
from jax.experimental.pallas import tpu as pltpu

D_MODEL = 1024
BATCH = 8
SEQ = 2048
DEPTH = 2
GRID_W = 64
CTX_LEN = 256
BLOCK = 128
WINDOW = 128
ROPE_BASE = 10000.0
EPS = 1e-6
HQ_A, HKV_A, HD_A = 16, 2, 64
HQ_B, HKV_B, HD_B = 8, 2, 128
N_EXPERTS = 32
TOP_K = 4
D_FF = D_MODEL
SWIGLU_LIMIT = 7.0
SWIGLU_ALPHA = 1.702

T_LAT = BATCH * SEQ
T_CTX = BATCH * CTX_LEN
T_ALL = T_LAT + T_CTX
LANES = 128
TM = 256
LAT_TILES = T_LAT // TM
ALL_TILES = T_ALL // TM
TILES_PER_SEQ = SEQ // TM
MOD_ROWS = 16
CTX_MOD_ROW = BATCH
TMP = 512
PAIR_STACK = 2
SUB = 2
BQ_B = 512
TME = 512
SLOT_ROWS = 8
RUN_ALIGN = 8
LOCAL_ROWS = -(-(TM * TOP_K + N_EXPERTS * (RUN_ALIGN - 1)) // LANES) * LANES
N_CHUNKS = LOCAL_ROWS // RUN_ALIGN
MIN_CHUNKS = TM * TOP_K // RUN_ALIGN
NEG = -1e30
LOG2E = 1.4426950408889634
NT_DIMS = (((1,), (1,)), ((), ()))
VMEM_LIMIT = 56 * 1024 * 1024

f32 = jnp.float32
bf16 = jnp.bfloat16


def _seg_of_tile(i, tiles_per_seq, lat_tiles):
    return jnp.where(i < lat_tiles, i // tiles_per_seq, CTX_MOD_ROW)


def _adaln_kernel(c_ref, w_ref, b_ref, o_ref):
    c = c_ref[...]
    s = c * jax.nn.sigmoid(c)
    o_ref[...] = jnp.dot(s, w_ref[...], precision=lax.Precision.HIGHEST,
                         preferred_element_type=f32) + b_ref[...]


def _adaln(c_all, w_ada, b_ada):
    tn = 1536
    return pl.pallas_call(
        _adaln_kernel,
        grid=(DEPTH, 6 * D_MODEL // tn),
        in_specs=[
            pl.BlockSpec((MOD_ROWS, D_MODEL), lambda l, j: (0, 0)),
            pl.BlockSpec((None, D_MODEL, tn), lambda l, j: (l, 0, j)),
            pl.BlockSpec((None, 1, tn), lambda l, j: (l, 0, j)),
        ],
        out_specs=pl.BlockSpec((None, MOD_ROWS, tn), lambda l, j: (l, 0, j)),
        out_shape=jax.ShapeDtypeStruct((DEPTH, MOD_ROWS, 6 * D_MODEL), f32),
        compiler_params=pltpu.CompilerParams(
            dimension_semantics=("arbitrary", "arbitrary"), vmem_limit_bytes=VMEM_LIMIT),
        name="adaln",
    )(c_all, w_ada, b_ada.reshape(DEPTH, 1, 6 * D_MODEL))


def _rms(x, g):
    return x * lax.rsqrt(jnp.mean(x * x, axis=-1, keepdims=True) + EPS) * g


def _stream_specs(tm, ctx_base):
    lat_tiles = T_LAT // tm
    return [pl.BlockSpec((tm, D_MODEL), lambda i, *_: (jnp.minimum(i, lat_tiles - 1), 0)),
            pl.BlockSpec((tm, D_MODEL), lambda i, *_: (jnp.maximum(i, lat_tiles) - ctx_base, 0))]


def _stream_tile(x_ref, xc_ref):
    lat_tiles = T_LAT // x_ref.shape[0]
    return jnp.where(pl.program_id(0) < lat_tiles, x_ref[...], xc_ref[...])


def _qkv_kernel(x_ref, xc_ref, mod_ref, g_ref, w_ref, cos_ref, sa_ref, sb_ref, qn_ref, kn_ref,
                q_ref, k_ref, v_ref, *, nq, nk, hd, qk_norm, v_ones, chain):
    x_in = _stream_tile(x_ref, xc_ref)
    quarter = hd // 4
    scale = hd ** -0.5 * LOG2E
    for r0 in range(0, x_ref.shape[0], chain):
        r = slice(r0, r0 + chain)
        h = _rms(x_in[r], g_ref[...]) * (1.0 + mod_ref[1:2, :]) + mod_ref[0:1, :]
        qkv = jnp.dot(h.astype(bf16), w_ref[...], preferred_element_type=f32)
        cos, sa, sb = cos_ref[r, :], sa_ref[r, :], sb_ref[r, :]

        def rope(c):
            return c * cos + pltpu.roll(c, quarter, 1) * sa + pltpu.roll(c, LANES - quarter, 1) * sb

        for j in range(nq // LANES):
            c = qkv[:, j * LANES:(j + 1) * LANES]
            if qk_norm:
                c = _rms(c, qn_ref[...])
            q_ref[r, j * LANES:(j + 1) * LANES] = (rope(c) * scale).astype(bf16)
        for j in range(nk // LANES):
            c = qkv[:, nq + j * LANES:nq + (j + 1) * LANES]
            if qk_norm:
                c = _rms(c, kn_ref[...])
            k_ref[r, j * LANES:(j + 1) * LANES] = rope(c).astype(bf16)
        if v_ones:
            for j in range(nk // LANES):
                v_ref[r, 2 * j * LANES:(2 * j + 1) * LANES] = (
                    qkv[:, nq + nk + j * LANES:nq + nk + (j + 1) * LANES].astype(bf16))
                v_ref[r, (2 * j + 1) * LANES:(2 * j + 2) * LANES] = jnp.ones((chain, LANES), bf16)
        else:
            v_ref[r, :] = qkv[:, nq + nk:].astype(bf16)


def _qkv(x_lat, x_ctx, mods, layer, g, w_bf16, tabs, qn, kn, *, hq, hkv, hd, qk_norm, v_ones, tm, chain):
    ctx_base = 0 if x_ctx is x_lat else T_LAT // tm
    nq, nk = hq * hd, hkv * hd
    nv = 2 * nk if v_ones else nk
    per_seq, lat_tiles = SEQ // tm, T_LAT // tm
    seg = functools.partial(_seg_of_tile, tiles_per_seq=per_seq, lat_tiles=lat_tiles)
    tab_idx = lambda i: (jnp.where(i < lat_tiles, i % per_seq, per_seq), 0)
    tab_spec = pl.BlockSpec((tm, LANES), tab_idx)
    row = lambda n: pl.BlockSpec((tm, n), lambda i: (i, 0))
    return pl.pallas_call(
        functools.partial(_qkv_kernel, nq=nq, nk=nk, hd=hd, qk_norm=qk_norm, v_ones=v_ones, chain=chain),
        grid=(T_ALL // tm,),
        in_specs=[
            *_stream_specs(tm, ctx_base),
            pl.BlockSpec((None, None, 6, D_MODEL), lambda i: (layer, seg(i), 0, 0)),
            pl.BlockSpec((1, D_MODEL), lambda i: (0, 0)),
            pl.BlockSpec((D_MODEL, nq + 2 * nk), lambda i: (0, 0)),
            tab_spec, tab_spec, tab_spec,
            pl.BlockSpec((1, LANES), lambda i: (0, 0)),
            pl.BlockSpec((1, LANES), lambda i: (0, 0)),
        ],
        out_specs=[row(nq), row(nk), row(nv)],
        out_shape=[jax.ShapeDtypeStruct((T_ALL, nq), bf16),
                   jax.ShapeDtypeStruct((T_ALL, nk), bf16),
                   jax.ShapeDtypeStruct((T_ALL, nv), bf16)],
        compiler_params=pltpu.CompilerParams(
            dimension_semantics=("arbitrary",), vmem_limit_bytes=VMEM_LIMIT),
        name=f"qkv{layer}",
    )(x_lat, x_ctx, mods, g, w_bf16, *tabs, qn, kn)


def _rope_tables(hd):
    quarter = hd // 4
    inv_freq = jnp.float32(ROPE_BASE) ** (-jnp.arange(quarter, dtype=f32) / quarter)
    t = jnp.arange(SEQ)
    ang_r = (t // GRID_W).astype(f32)[:, None] * inv_freq[None, :]
    ang_c = (t % GRID_W).astype(f32)[:, None] * inv_freq[None, :]
    z = jnp.zeros_like(ang_r)
    cos = jnp.concatenate([jnp.cos(ang_r)] * 2 + [jnp.cos(ang_c)] * 2, axis=-1)
    sa = jnp.concatenate([z, jnp.sin(ang_r), z, jnp.sin(ang_c)], axis=-1)
    sb = jnp.concatenate([-jnp.sin(ang_r), z, -jnp.sin(ang_c), z], axis=-1)
    rep = LANES // hd
    pad = lambda a, v: jnp.concatenate(
        [jnp.tile(a, (1, rep)), jnp.full((TMP, LANES), v, f32)], axis=0)
    return pad(cos, 1.0), pad(sa, 0.0), pad(sb, 0.0)


def _pair_operand(x, g):
    lane = lax.broadcasted_iota(jnp.int32, x.shape, 1)
    swapped = pltpu.roll(x, HD_A, 1)
    lo_src, hi_src = (x, swapped) if g == 0 else (swapped, x)
    lo = jnp.where(lane < HD_A, lo_src, 0.0)
    hi = jnp.where(lane >= HD_A, hi_src, 0.0)
    return jnp.concatenate([lo, hi], axis=0)


def _attend_pairs(q_ref, o_ref, sink_ref, kcat, vcat, mask):
    n = kcat.shape[0]
    pairs = PAIR_STACK
    rows = pairs * BLOCK
    lane = lax.broadcasted_iota(jnp.int32, (rows, LANES), 1)
    pair_of_row = lax.broadcasted_iota(jnp.int32, (rows, 1), 0) // BLOCK
    ind_row = lax.broadcasted_iota(jnp.int32, (2 * n, LANES), 0)
    ind_lane = lax.broadcasted_iota(jnp.int32, (2 * n, LANES), 1)
    ind = jnp.where(ind_lane == ind_row // n, 1.0, 0.0)
    if mask is not None:
        mask = jnp.concatenate([mask] * pairs, axis=0)
    operands = [(_pair_operand(kcat, g).astype(bf16),
                 jnp.concatenate([_pair_operand(vcat, g), ind], axis=1).astype(bf16))
                for g in range(HKV_A)]
    pairs_per_group = HQ_A // HKV_A // 2
    for c0 in range(0, HQ_A // 2, pairs):
        kp, vp = operands[c0 // pairs_per_group]
        cols = [(c0 + p) * LANES for p in range(pairs)]
        qs = jnp.concatenate([q_ref[:, c:c + LANES] for c in cols], axis=0)
        s = lax.dot_general(qs, kp, NT_DIMS, preferred_element_type=f32)
        es, ms, sks = [], [], []
        for hh in range(2):
            sh = s[:, hh * n:(hh + 1) * n]
            if mask is not None:
                sh = jnp.where(mask, sh, NEG)
            sk = jnp.zeros((rows, 1), f32)
            for p in range(pairs):
                sk = jnp.where(pair_of_row == p, sink_ref[cols[p] // HD_A + hh] * LOG2E, sk)
            m = jnp.maximum(jnp.max(sh, axis=-1, keepdims=True), sk)
            es.append(jnp.exp2(sh - m).astype(bf16))
            ms.append(m)
            sks.append(sk)
        oe = jnp.dot(jnp.concatenate(es, axis=1), vp, preferred_element_type=f32)
        inv = [1.0 / (oe[:, LANES + hh:LANES + hh + 1] + jnp.exp2(sks[hh] - ms[hh])) for hh in range(2)]
        o = oe[:, :LANES] * jnp.where(lane < HD_A, inv[0], inv[1])
        for p in range(pairs):
            o_ref[:, cols[p]:cols[p] + LANES] = o[p * BLOCK:(p + 1) * BLOCK].astype(bf16)


def _attn_a_kernel(sink_ref, q_ref, k_ref, v_ref, kc_ref, vc_ref, o_ref):
    j = pl.program_id(1)
    nblk = SEQ // BLOCK
    kc = kc_ref[...].astype(f32)
    vc = vc_ref[...].astype(f32)

    @pl.when(j < nblk)
    def _():
        wlen = 3 * BLOCK
        s0 = pl.multiple_of(jnp.clip((j - 1) * BLOCK, 0, SEQ - wlen), BLOCK)
        kcat = jnp.concatenate([k_ref[pl.ds(s0, wlen), :].astype(f32), kc], axis=0)
        vcat = jnp.concatenate([v_ref[pl.ds(s0, wlen), :].astype(f32), vc], axis=0)
        qpos = j * BLOCK + lax.broadcasted_iota(jnp.int32, (BLOCK, wlen + CTX_LEN), 0)
        col = lax.broadcasted_iota(jnp.int32, (BLOCK, wlen + CTX_LEN), 1)
        mask = (jnp.abs(qpos - (s0 + col)) <= WINDOW) | (col >= wlen)
        _attend_pairs(q_ref, o_ref, sink_ref, kcat, vcat, mask)

    @pl.when(j >= nblk)
    def _():
        _attend_pairs(q_ref, o_ref, sink_ref, kc, vc, None)


def _attn_a(q, k, v, sink):
    nblk = SEQ // BLOCK
    cblk = CTX_LEN // BLOCK
    nk = HKV_A * HD_A
    qrow = lambda b, j, s: (jnp.where(j < nblk, b * nblk + j, T_LAT // BLOCK + b * cblk + (j - nblk)), 0)
    lat_kv = pl.BlockSpec((SEQ, nk), lambda b, j, s: (b, 0))
    ctx_kv = pl.BlockSpec((CTX_LEN, nk), lambda b, j, s: (T_LAT // CTX_LEN + b, 0))
    return pl.pallas_call(
        _attn_a_kernel,
        grid_spec=pltpu.PrefetchScalarGridSpec(
            num_scalar_prefetch=1,
            grid=(BATCH, nblk + cblk),
            in_specs=[pl.BlockSpec((BLOCK, HQ_A * HD_A), qrow), lat_kv, lat_kv, ctx_kv, ctx_kv],
            out_specs=pl.BlockSpec((BLOCK, HQ_A * HD_A), qrow),
        ),
        out_shape=jax.ShapeDtypeStruct((T_ALL, HQ_A * HD_A), bf16),
        compiler_params=pltpu.CompilerParams(
            dimension_semantics=("arbitrary", "arbitrary"), vmem_limit_bytes=VMEM_LIMIT),
        name="attn_a",
    )(sink, q, k, v, k, v)


def _attn_b_kernel(q_ref, k_ref, v_ref, kc_ref, vc_ref, o_ref):
    rep = HQ_B // HKV_B
    for h in range(HQ_B):
        g = h // rep
        hs = slice(h * HD_B, (h + 1) * HD_B)
        gs = slice(g * HD_B, (g + 1) * HD_B)
        vs = slice(2 * g * HD_B, 2 * (g + 1) * HD_B)
        q = q_ref[:, hs]
        s1 = lax.dot_general(q, k_ref[:, gs], NT_DIMS, preferred_element_type=f32)
        s2 = lax.dot_general(q, kc_ref[:, gs], NT_DIMS, preferred_element_type=f32)
        m = jnp.maximum(jnp.max(s1, axis=-1, keepdims=True), jnp.max(s2, axis=-1, keepdims=True))
        e1 = jnp.exp2(s1 - m).astype(bf16)
        e2 = jnp.exp2(s2 - m).astype(bf16)
        oe = (jnp.dot(e1, v_ref[:, vs], preferred_element_type=f32)
              + jnp.dot(e2, vc_ref[:, vs], preferred_element_type=f32))
        o_ref[:, hs] = (oe[:, :HD_B] * (1.0 / oe[:, HD_B:HD_B + 1])).astype(bf16)


def _attn_b(q, k, v):
    nblk = SEQ // BQ_B
    nk = HKV_B * HD_B
    qrow = lambda b, j: (b * nblk + j, 0)
    lat = lambda n: pl.BlockSpec((SEQ, n), lambda b, j: (b, 0))
    ctx = lambda n: pl.BlockSpec((CTX_LEN, n), lambda b, j: (T_LAT // CTX_LEN + b, 0))
    return pl.pallas_call(
        _attn_b_kernel,
        grid=(BATCH, nblk),
        in_specs=[pl.BlockSpec((BQ_B, HQ_B * HD_B), qrow), lat(nk), lat(2 * nk), ctx(nk), ctx(2 * nk)],
        out_specs=pl.BlockSpec((BQ_B, HQ_B * HD_B), qrow),
        out_shape=jax.ShapeDtypeStruct((T_LAT, HQ_B * HD_B), bf16),
        compiler_params=pltpu.CompilerParams(
            dimension_semantics=("arbitrary", "arbitrary"), vmem_limit_bytes=VMEM_LIMIT),
        name="attn_b",
    )(q, k, v, k, v)


def _post_attn_kernel(o_ref, wo_ref, x_ref, xc_ref, mod_ref, g1_ref, g2_ref, wrt_ref, brt_ref,
                      xn_ref, f_ref, tit_ref, ti_ref, tw_ref, cnt_ref):
    x_in = _stream_tile(x_ref, xc_ref)
    halves = []
    for h in range(TMP // TM):
        r = slice(h * TM, (h + 1) * TM)
        a = jnp.dot(o_ref[r, :], wo_ref[...], preferred_element_type=f32)
        x = x_in[r] + mod_ref[2:3, :] * _rms(a, g1_ref[...])
        xn_ref[r, :] = x
        fh = _rms(x, g2_ref[...]) * (1.0 + mod_ref[4:5, :]) + mod_ref[3:4, :]
        f_ref[r, :] = fh.astype(bf16)
        halves.append(fh)
    f = jnp.concatenate(halves, axis=0)
    logits = lax.dot_general(wrt_ref[...], f, NT_DIMS, precision=lax.Precision.HIGHEST,
                             preferred_element_type=f32) + brt_ref[...]
    sub = lax.broadcasted_iota(jnp.int32, logits.shape, 0)
    rest = logits
    top_v, top_i = [], []
    for _ in range(TOP_K):
        m = jnp.max(rest, axis=0, keepdims=True)
        idx = jnp.min(jnp.where(rest == m, sub, N_EXPERTS), axis=0, keepdims=True)
        top_v.append(m)
        top_i.append(idx)
        rest = jnp.where(sub == idx, -jnp.inf, rest)
    es = [jnp.exp(v - top_v[0]) for v in top_v]
    inv = 1.0 / (es[0] + es[1] + es[2] + es[3])
    k8 = lax.broadcasted_iota(jnp.int32, (SLOT_ROWS, TMP), 0)
    tit = jnp.full((SLOT_ROWS, TMP), -1.0, f32)
    twt = jnp.zeros((SLOT_ROWS, TMP), f32)
    for k in range(TOP_K):
        tit = jnp.where(k8 == k, top_i[k].astype(f32), tit)
        twt = jnp.where(k8 == k, es[k] * inv, twt)
    tit_ref[...] = tit
    eye = (lax.broadcasted_iota(jnp.int32, (TM, TM), 0)
           == lax.broadcasted_iota(jnp.int32, (TM, TM), 1)).astype(f32)
    to_rows = lambda t: lax.dot_general(eye, t, NT_DIMS, precision=lax.Precision.HIGHEST,
                                        preferred_element_type=f32)
    lane = lax.broadcasted_iota(jnp.int32, (TM, N_EXPERTS), 1).astype(f32)
    for h in range(TMP // TM):
        cols = slice(h * TM, (h + 1) * TM)
        ti = to_rows(tit[:, cols])
        ti_ref[cols, :] = ti
        tw_ref[cols, :] = to_rows(twt[:, cols])
        cnt = jnp.zeros((TM, N_EXPERTS), f32)
        for k in range(TOP_K):
            cnt = cnt + (ti[:, k:k + 1] == lane).astype(f32)
        cnt_ref[h] = jnp.sum(cnt, axis=0, keepdims=True)


def _post_attn(o, wo_bf16, x_lat, x_ctx, mods, layer, g1, g2, wr, br, n_tiles):
    ctx_base = 0 if x_ctx is x_lat else T_LAT // TMP
    seg = functools.partial(_seg_of_tile, tiles_per_seq=SEQ // TMP, lat_tiles=T_LAT // TMP)
    rows = n_tiles * TM
    row = lambda n: pl.BlockSpec((TMP, n), lambda i: (i, 0))
    const = lambda a, b: pl.BlockSpec((a, b), lambda i: (0, 0))
    return pl.pallas_call(
        _post_attn_kernel,
        grid=(rows // TMP,),
        in_specs=[
            row(D_MODEL), const(D_MODEL, D_MODEL), *_stream_specs(TMP, ctx_base),
            pl.BlockSpec((None, None, 6, D_MODEL), lambda i: (layer, seg(i), 0, 0)),
            const(1, D_MODEL), const(1, D_MODEL), const(N_EXPERTS, D_MODEL), const(N_EXPERTS, 1),
        ],
        out_specs=[row(D_MODEL), row(D_MODEL),
                   pl.BlockSpec((SLOT_ROWS, TMP), lambda i: (0, i)),
                   row(SLOT_ROWS), row(SLOT_ROWS),
                   pl.BlockSpec((TMP // TM, 1, N_EXPERTS), lambda i: (i, 0, 0))],
        out_shape=[jax.ShapeDtypeStruct((rows, D_MODEL), f32),
                   jax.ShapeDtypeStruct((rows, D_MODEL), bf16),
                   jax.ShapeDtypeStruct((SLOT_ROWS, rows), f32),
                   jax.ShapeDtypeStruct((rows, SLOT_ROWS), f32),
                   jax.ShapeDtypeStruct((rows, SLOT_ROWS), f32),
                   jax.ShapeDtypeStruct((n_tiles, 1, N_EXPERTS), f32)],
        compiler_params=pltpu.CompilerParams(
            dimension_semantics=("arbitrary",), vmem_limit_bytes=VMEM_LIMIT),
        name=f"post_attn{layer}",
    )(o, wo_bf16, x_lat, x_ctx, mods, g1, g2, wr.T, br.reshape(N_EXPERTS, 1))


def _ffn_tiles_max(n_tiles):
    rows = n_tiles * (TM * TOP_K + N_EXPERTS * (RUN_ALIGN - 1)) + N_EXPERTS * (TME - RUN_ALIGN)
    return -(-rows // TME)


def _route_tables(cnt, n_tiles_max):
    n = cnt.reshape(-1, N_EXPERTS).astype(jnp.int32)
    run = (n + RUN_ALIGN - 1) // RUN_ALIGN * RUN_ALIGN
    loff = jnp.cumsum(run, axis=1) - run
    tot = jnp.sum(run, axis=0)
    gsz = (tot + TME - 1) // TME * TME
    ends = jnp.cumsum(gsz)
    goff = (ends - gsz)[None, :] + jnp.cumsum(run, axis=0) - run
    c0 = jnp.arange(N_CHUNKS, dtype=jnp.int32) * RUN_ALIGN
    owner = jnp.sum(c0[None, :, None] >= (loff + run)[:, None, :], axis=2)
    shift = jnp.where(owner[:, :, None] == jnp.arange(N_EXPERTS)[None, None, :],
                      (goff - loff)[:, None, :], 0)
    tab = jnp.concatenate([jnp.sum(shift, axis=2) + c0[None, :],
                           jnp.sum(run, axis=1, keepdims=True) // RUN_ALIGN], axis=1)[:, None, :]
    n_active = ends[-1] // TME
    tail = jnp.concatenate([ends - gsz + tot, (gsz - tot) // RUN_ALIGN, n_active[None]])[None, :]
    tile = jnp.arange(n_tiles_max, dtype=jnp.int32)
    te = jnp.sum((jnp.minimum(tile, n_active - 1) * TME)[:, None] >= ends[None, :], axis=1)
    of_tile = te[:, None] == jnp.arange(N_EXPERTS)[None, :]
    used = jnp.sum(jnp.where(of_tile, (ends - gsz + tot)[None, :], 0), axis=1) - tile * TME
    half = (used <= TME // 2) & (tile < n_active)
    ids = jnp.arange(N_EXPERTS, dtype=jnp.int32)
    nonempty = gsz > 0
    later = jnp.where((ids[None, :] > ids[:, None]) & nonempty[None, :], ids[None, :], N_EXPERTS)
    nxt_e = jnp.min(later, axis=1)
    nxt_e = jnp.where(nxt_e == N_EXPERTS, -1, nxt_e)
    grp = jnp.sum(jnp.where(of_tile, (jnp.cumsum(nonempty) - 1)[None, :], 0), axis=1)
    nxt = jnp.sum(jnp.where(of_tile, nxt_e[None, :], 0), axis=1)
    loff_f = loff.astype(f32)
    i32 = lambda a: a.astype(jnp.int32)
    return (tab, tail, loff_f[:, None, :], loff_f[:, :, None], i32(te), i32(half), i32(grp), i32(nxt),
            i32(n_active).reshape(1))


def _local_chunk(c):
    if isinstance(c, int):
        return pl.ds(c * RUN_ALIGN, RUN_ALIGN)
    return pl.ds(pl.multiple_of(c * RUN_ALIGN, RUN_ALIGN), RUN_ALIGN)


def _global_chunk(ctab_ref, h, c):
    return pl.ds(pl.multiple_of(ctab_ref[h, 0, c], RUN_ALIGN), RUN_ALIGN)


def _start_chunks(ctab_ref, h, copy):
    for c in range(MIN_CHUNKS):
        copy(c).start()
    lax.fori_loop(MIN_CHUNKS, ctab_ref[h, 0, N_CHUNKS], lambda c, carry: (copy(c).start(), carry)[1], 0)


def _wait_chunks(ctab_ref, h, copy, bulk):
    bulk.wait()
    lax.fori_loop(MIN_CHUNKS, ctab_ref[h, 0, N_CHUNKS], lambda c, carry: (copy(c).wait(), carry)[1], 0)


def _chunk(base, c):
    return pl.ds(pl.multiple_of(base + c * RUN_ALIGN, RUN_ALIGN), RUN_ALIGN)


def _dispatch_kernel(ctab_ref, ctab_prev_ref, tail_ref, f_ref, tit_ref, loffc_ref, xs_hbm,
                     lbuf, zbuf, sem, zsem):
    i = pl.program_id(0)
    slot = i % 2
    sub = lax.broadcasted_iota(jnp.int32, (N_EXPERTS, TM), 0).astype(f32)
    before = (lax.broadcasted_iota(jnp.int32, (TM, TM), 0)
              < lax.broadcasted_iota(jnp.int32, (TM, TM), 1)).astype(bf16)
    rows = lax.broadcasted_iota(jnp.int32, (LOCAL_ROWS, TM), 0).astype(f32)

    def chunk_copy(ctab, h, b):
        return lambda c: pltpu.make_async_copy(
            lbuf.at[b, _local_chunk(c)], xs_hbm.at[_global_chunk(ctab, h, c)], sem.at[b])

    def bulk_copy(b):
        n = MIN_CHUNKS * RUN_ALIGN
        return pltpu.make_async_copy(lbuf.at[b, 0:n], xs_hbm.at[0:n], sem.at[b])

    def wait_tiles(ctab, s):
        for h in range(SUB):
            b = s * SUB + h
            _wait_chunks(ctab, h, chunk_copy(ctab, h, b), bulk_copy(b))

    for h in range(SUB):
        tit = tit_ref[:, h * TM:(h + 1) * TM]
        hit = [tit[k:k + 1, :] == sub for k in range(TOP_K)]
        cnt = sum(x.astype(f32) for x in hit)
        base = loffc_ref[h] + jnp.dot(cnt.astype(bf16), before, preferred_element_type=f32)
        onehot = jnp.zeros((LOCAL_ROWS, TM), f32)
        for k in range(TOP_K):
            lp = jnp.sum(jnp.where(hit[k], base, 0.0), axis=0, keepdims=True)
            onehot = onehot + (rows == lp).astype(f32)
        b = slot * SUB + h
        lbuf[b] = jnp.dot(onehot.astype(bf16), f_ref[h * TM:(h + 1) * TM, :], preferred_element_type=f32)
        _start_chunks(ctab_ref, h, chunk_copy(ctab_ref, h, b))

    @pl.when(i > 0)
    def _():
        wait_tiles(ctab_prev_ref, 1 - slot)

    tail_copy = lambda e, c: pltpu.make_async_copy(zbuf, xs_hbm.at[_chunk(tail_ref[0, e], c)], zsem)

    def for_each_tail(fn):
        for e in range(N_EXPERTS):
            def body(c, carry, e=e):
                fn(e, c)
                return carry
            lax.fori_loop(0, tail_ref[0, N_EXPERTS + e], body, 0)

    @pl.when(i == pl.num_programs(0) - 1)
    def _():
        zbuf[...] = jnp.zeros_like(zbuf)
        for_each_tail(lambda e, c: tail_copy(e, c).start())
        for_each_tail(lambda e, c: tail_copy(e, c).wait())
        wait_tiles(ctab_ref, slot)
        lbuf[0, 0:TME, :] = jnp.zeros((TME, D_MODEL), f32)
        n_active = tail_ref[0, 2 * N_EXPERTS]
        n_spare = xs_hbm.shape[0] // TME - n_active
        spare_copy = lambda j: pltpu.make_async_copy(
            lbuf.at[0, 0:TME], xs_hbm.at[pl.ds(pl.multiple_of((n_active + j) * TME, TME), TME)], zsem)
        lax.fori_loop(0, n_spare, lambda j, carry: (spare_copy(j).start(), carry)[1], 0)
        lax.fori_loop(0, n_spare, lambda j, carry: (spare_copy(j).wait(), carry)[1], 0)


def _dispatch(f, tit, tab, tail, loffc, layer, n_tiles, n_tiles_max):
    smem = functools.partial(pl.BlockSpec, memory_space=pltpu.SMEM)
    return pl.pallas_call(
        _dispatch_kernel,
        grid=(n_tiles // SUB,),
        in_specs=[
            smem((SUB, 1, N_CHUNKS + 1), lambda i: (i, 0, 0)),
            smem((SUB, 1, N_CHUNKS + 1), lambda i: (jnp.maximum(i - 1, 0), 0, 0)),
            smem((1, 2 * N_EXPERTS + 1), lambda i: (0, 0)),
            pl.BlockSpec((SUB * TM, D_MODEL), lambda i: (i, 0)),
            pl.BlockSpec((SLOT_ROWS, SUB * TM), lambda i: (0, i)),
            pl.BlockSpec((SUB, N_EXPERTS, 1), lambda i: (i, 0, 0)),
        ],
        out_specs=pl.BlockSpec(memory_space=pl.ANY),
        out_shape=jax.ShapeDtypeStruct((n_tiles_max * TME, D_MODEL), f32),
        scratch_shapes=[pltpu.VMEM((2 * SUB, LOCAL_ROWS, D_MODEL), f32),
                        pltpu.VMEM((RUN_ALIGN, D_MODEL), f32),
                        pltpu.SemaphoreType.DMA((2 * SUB,)), pltpu.SemaphoreType.DMA(())],
        compiler_params=pltpu.CompilerParams(
            dimension_semantics=("arbitrary",), vmem_limit_bytes=VMEM_LIMIT),
        name=f"dispatch{layer}",
    )(tab, tab, tail, f, tit, loffc)


def _moe_ffn_kernel(te_ref, half_ref, grp_ref, nxt_ref, na_ref, x_ref, w1_hbm, b1_ref, w2_hbm, b2_ref,
                    y_ref, w1f, w2f, w1b, w2b, sem, *, layer):
    i = pl.program_id(0)
    na = na_ref[0]

    def weight_copies(expert, s):
        return (pltpu.make_async_copy(w1_hbm.at[layer, expert], w1f.at[s], sem.at[0, s]),
                pltpu.make_async_copy(w2_hbm.at[layer, expert], w2f.at[s], sem.at[1, s]))

    def ffn(rows):
        u = jnp.dot(x_ref[rows, :].astype(bf16), w1b[...], preferred_element_type=f32) + b1_ref[...]
        glu = jnp.minimum(u[:, :D_FF], SWIGLU_LIMIT)
        lin = jnp.clip(u[:, D_FF:], -SWIGLU_LIMIT, SWIGLU_LIMIT)
        act = glu * jax.nn.sigmoid(SWIGLU_ALPHA * glu) * (lin + 1.0)
        y_ref[rows, :] = jnp.dot(act.astype(bf16), w2b[...], preferred_element_type=f32) + b2_ref[...]

    @pl.when(i >= na)
    def _():
        y_ref[...] = jnp.zeros_like(y_ref)

    @pl.when(i == 0)
    def _():
        for cp in weight_copies(te_ref[0], 0):
            cp.start()

    @pl.when(i < na)
    def _():
        @pl.when((i == 0) | (te_ref[i] != te_ref[jnp.maximum(i - 1, 0)]))
        def _():
            s = grp_ref[i] % 2
            for cp in weight_copies(te_ref[i], s):
                cp.wait()
            w1b[...] = w1f[s].astype(bf16)
            w2b[...] = w2f[s].astype(bf16)

            @pl.when(nxt_ref[i] >= 0)
            def _():
                for cp in weight_copies(nxt_ref[i], 1 - s):
                    cp.start()

        @pl.when(half_ref[i] == 0)
        def _():
            ffn(slice(0, TME))

        @pl.when(half_ref[i] != 0)
        def _():
            ffn(slice(0, TME // 2))
            y_ref[TME // 2:, :] = jnp.zeros((TME // 2, D_MODEL), f32)


def _moe_ffn(xs_sorted, te, half, grp, nxt, na, w1, b1, w2, b2, layer, n_tiles_max):
    ex = lambda i, te, *_: (layer, te[i], 0, 0)
    return pl.pallas_call(
        functools.partial(_moe_ffn_kernel, layer=layer),
        grid_spec=pltpu.PrefetchScalarGridSpec(
            num_scalar_prefetch=5,
            grid=(n_tiles_max,),
            in_specs=[
                pl.BlockSpec((TME, D_MODEL),
                             lambda i, te, half, grp, nxt, na: (jnp.maximum(jnp.minimum(i, na[0] - 1), 0), 0)),
                pl.BlockSpec(memory_space=pl.ANY),
                pl.BlockSpec((None, None, 1, 2 * D_FF), ex),
                pl.BlockSpec(memory_space=pl.ANY),
                pl.BlockSpec((None, None, 1, D_MODEL), ex),
            ],
            out_specs=pl.BlockSpec((TME, D_MODEL), lambda i, *_: (i, 0)),
            scratch_shapes=[pltpu.VMEM((2, D_MODEL, 2 * D_FF), f32), pltpu.VMEM((2, D_FF, D_MODEL), f32),
                            pltpu.VMEM((D_MODEL, 2 * D_FF), bf16), pltpu.VMEM((D_FF, D_MODEL), bf16),
                            pltpu.SemaphoreType.DMA((2, 2))],
        ),
        out_shape=jax.ShapeDtypeStruct((n_tiles_max * TME, D_MODEL), f32),
        compiler_params=pltpu.CompilerParams(
            dimension_semantics=("arbitrary",), vmem_limit_bytes=VMEM_LIMIT),
        name=f"moe_ffn{layer}",
    )(te, half, grp, nxt, na, xs_sorted, w1, b1.reshape(DEPTH, N_EXPERTS, 1, 2 * D_FF), w2,
      b2.reshape(DEPTH, N_EXPERTS, 1, D_MODEL))


def _combine_kernel(ctab_ref, ctab_next_ref, y_hbm, ti_ref, tw_ref, loffr_ref, x_ref, mod_ref, g_ref,
                    xn_ref, ybuf, sem):
    i = pl.program_id(0)
    slot = i % 2

    def chunk_copy(ctab, h, b):
        return lambda c: pltpu.make_async_copy(
            y_hbm.at[_global_chunk(ctab, h, c)], ybuf.at[b, _local_chunk(c)], sem.at[b])

    def bulk_copy(b):
        n = MIN_CHUNKS * RUN_ALIGN
        return pltpu.make_async_copy(y_hbm.at[0:n], ybuf.at[b, 0:n], sem.at[b])

    def start_tiles(ctab, s):
        for h in range(SUB):
            _start_chunks(ctab, h, chunk_copy(ctab, h, s * SUB + h))

    @pl.when(i == 0)
    def _():
        ybuf[...] = jnp.zeros_like(ybuf)
        start_tiles(ctab_ref, 0)

    @pl.when(i + 1 < pl.num_programs(0))
    def _():
        start_tiles(ctab_next_ref, 1 - slot)

    lane = lax.broadcasted_iota(jnp.int32, (TM, N_EXPERTS), 1).astype(f32)
    before = (lax.broadcasted_iota(jnp.int32, (TM, TM), 1)
              < lax.broadcasted_iota(jnp.int32, (TM, TM), 0)).astype(bf16)
    cols = lax.broadcasted_iota(jnp.int32, (TM, LOCAL_ROWS), 1).astype(f32)
    for h in range(SUB):
        tok = slice(h * TM, (h + 1) * TM)
        b = slot * SUB + h
        ti = ti_ref[tok, :]
        tw = tw_ref[tok, :]
        hit = [ti[:, k:k + 1] == lane for k in range(TOP_K)]
        cnt = sum(x.astype(f32) for x in hit)
        base = loffr_ref[h] + jnp.dot(before, cnt.astype(bf16), preferred_element_type=f32)
        w = jnp.zeros((TM, LOCAL_ROWS), f32)
        for k in range(TOP_K):
            lp = jnp.sum(jnp.where(hit[k], base, 0.0), axis=1, keepdims=True)
            w = jnp.where(cols == lp, tw[:, k:k + 1], w)
        w_hi = w.astype(bf16)
        w_lo = (w - w_hi.astype(f32)).astype(bf16)
        _wait_chunks(ctab_ref, h, chunk_copy(ctab_ref, h, b), bulk_copy(b))
        yl = ybuf[b].astype(bf16)
        acc = (jnp.dot(w_hi, yl, preferred_element_type=f32)
               + jnp.dot(w_lo, yl, preferred_element_type=f32))
        xn_ref[tok, :] = x_ref[tok, :] + mod_ref[5:6, :] * _rms(acc, g_ref[...])


def _combine(y, tab, ti, tw, loffr, xs, mods, layer, g, n_tiles):
    rows = SUB * TM
    n_steps = n_tiles // SUB
    seg = functools.partial(_seg_of_tile, tiles_per_seq=SEQ // rows, lat_tiles=T_LAT // rows)
    row = lambda n: pl.BlockSpec((rows, n), lambda i: (i, 0))
    smem = functools.partial(pl.BlockSpec, memory_space=pltpu.SMEM)
    return pl.pallas_call(
        _combine_kernel,
        grid=(n_steps,),
        in_specs=[
            smem((SUB, 1, N_CHUNKS + 1), lambda i: (i, 0, 0)),
            smem((SUB, 1, N_CHUNKS + 1), lambda i: (jnp.minimum(i + 1, n_steps - 1), 0, 0)),
            pl.BlockSpec(memory_space=pl.ANY),
            row(SLOT_ROWS), row(SLOT_ROWS),
            pl.BlockSpec((SUB, 1, N_EXPERTS), lambda i: (i, 0, 0)),
            row(D_MODEL),
            pl.BlockSpec((None, None, 6, D_MODEL), lambda i: (layer, seg(i), 0, 0)),
            pl.BlockSpec((1, D_MODEL), lambda i: (0, 0)),
        ],
        out_specs=row(D_MODEL),
        out_shape=jax.ShapeDtypeStruct((n_tiles * TM, D_MODEL), f32),
        scratch_shapes=[pltpu.VMEM((2 * SUB, LOCAL_ROWS, D_MODEL), f32),
                        pltpu.SemaphoreType.DMA((2 * SUB,))],
        compiler_params=pltpu.CompilerParams(
            dimension_semantics=("arbitrary",), vmem_limit_bytes=VMEM_LIMIT),
        name=f"combine{layer}",
    )(tab, tab, y, ti, tw, loffr, xs, mods, g)


def _moe(f, tit, ti, tw, cnt, xs, mods, layer, g, w1, b1, w2, b2, n_tiles):
    n_tiles_max = _ffn_tiles_max(n_tiles)
    tab, tail, loffr, loffc, te, half, grp, nxt, na = _route_tables(cnt, n_tiles_max)
    xs_sorted = _dispatch(f, tit, tab, tail, loffc, layer, n_tiles, n_tiles_max)
    y = _moe_ffn(xs_sorted, te, half, grp, nxt, na, w1, b1, w2, b2, layer, n_tiles_max)
    return _combine(y, tab, ti, tw, loffr, xs, mods, layer, g, n_tiles)


def kernel(x, c, ctx, c_ctx, w_ada, b_ada, norm_g, a_w_qkv, a_w_o, a_sink, b_w_qkv, b_q_norm, b_k_norm,
           b_w_o, moe_w_router, moe_b_router, moe_w1, moe_b1, moe_w2, moe_b2):
    assert DEPTH == 2 and x.shape == (BATCH, SEQ, D_MODEL) and ctx.shape == (BATCH, CTX_LEN, D_MODEL)
    x_lat, x_ctx = x.reshape(T_LAT, D_MODEL), ctx.reshape(T_CTX, D_MODEL)
    c_all = jnp.concatenate(
        [c, c_ctx[None, :], jnp.zeros((MOD_ROWS - BATCH - 1, D_MODEL), f32)], axis=0)
    mods = _adaln(c_all, w_ada, b_ada).reshape(DEPTH, MOD_ROWS, 6, D_MODEL)
    ones = jnp.ones((1, LANES), f32)
    g = lambda i, j: norm_g[i, j][None, :]

    q, k, v = _qkv(x_lat, x_ctx, mods, 0, g(0, 0), a_w_qkv[0].astype(bf16), _rope_tables(HD_A), ones, ones,
                   hq=HQ_A, hkv=HKV_A, hd=HD_A, qk_norm=False, v_ones=False, tm=TMP, chain=TM)
    o = _attn_a(q, k, v, a_sink[0])
    xs, *routed = _post_attn(o, a_w_o[0].astype(bf16), x_lat, x_ctx, mods, 0, g(0, 1), g(0, 2),
                             moe_w_router[0], moe_b_router[0], ALL_TILES)
    xs = _moe(*routed, xs, mods, 0, g(0, 3), moe_w1, moe_b1, moe_w2, moe_b2, ALL_TILES)

    q, k, v = _qkv(xs, xs, mods, 1, g(1, 0), b_w_qkv[0].astype(bf16), _rope_tables(HD_B),
                   b_q_norm[0][None, :], b_k_norm[0][None, :],
                   hq=HQ_B, hkv=HKV_B, hd=HD_B, qk_norm=True, v_ones=True, tm=TM, chain=TM)
    o = _attn_b(q, k, v)
    xl, *routed = _post_attn(o, b_w_o[0].astype(bf16), xs, xs, mods, 1, g(1, 1), g(1, 2),
                             moe_w_router[1], moe_b_router[1], LAT_TILES)
    xl = _moe(*routed, xl, mods, 1, g(1, 3), moe_w1, moe_b1, moe_w2, moe_b2, LAT_TILES)
    return xl.reshape(BATCH, SEQ, D_MODEL)
```

```python
import functools

import jax
import jax.numpy as jnp
from jax import lax
from jax.experimental import pallas as pl
from jax.experimental.pallas import tpu as pltpu

D_MODEL = 1024
BATCH = 8
SEQ = 2048
DEPTH = 2
GRID_W = 64
CTX_LEN = 256
BLOCK = 128
WINDOW = 128
ROPE_BASE = 10000.0
EPS = 1e-6
HQ_A, HKV_A, HD_A = 16, 2, 64
HQ_B, HKV_B, HD_B = 8, 2, 128
N_EXPERTS = 32
TOP_K = 4
D_FF = D_MODEL
SWIGLU_LIMIT = 7.0
SWIGLU_ALPHA = 1.702

T_LAT = BATCH * SEQ
T_CTX = BATCH * CTX_LEN
T_ALL = T_LAT + T_CTX
LANES = 128
TM = 256
LAT_TILES = T_LAT // TM
ALL_TILES = T_ALL // TM
TILES_PER_SEQ = SEQ // TM
MOD_ROWS = 16
CTX_MOD_ROW = BATCH
TMP = 512
PAIR_STACK = 2
QB_A = 2
SUB = 2
BQ_B = 512
TME = 512
SLOT_ROWS = 8
RUN_ALIGN = 8
LOCAL_ROWS = -(-(TM * TOP_K + N_EXPERTS * (RUN_ALIGN - 1)) // LANES) * LANES
N_CHUNKS = LOCAL_ROWS // RUN_ALIGN
MIN_CHUNKS = TM * TOP_K // RUN_ALIGN
NEG = -1e30
LOG2E = 1.4426950408889634
NT_DIMS = (((1,), (1,)), ((), ()))
VMEM_LIMIT = 56 * 1024 * 1024

f32 = jnp.float32
bf16 = jnp.bfloat16


def _seg_of_tile(i, tiles_per_seq, lat_tiles):
    return jnp.where(i < lat_tiles, i // tiles_per_seq, CTX_MOD_ROW)


def _adaln_kernel(c_ref, w_ref, b_ref, o_ref):
    c = c_ref[...]
    s = c * jax.nn.sigmoid(c)
    o_ref[...] = jnp.dot(s, w_ref[...], precision=lax.Precision.HIGHEST,
                         preferred_element_type=f32) + b_ref[...]


def _adaln(c_all, w_ada, b_ada):
    tn = 1536
    return pl.pallas_call(
        _adaln_kernel,
        grid=(DEPTH, 6 * D_MODEL // tn),
        in_specs=[
            pl.BlockSpec((MOD_ROWS, D_MODEL), lambda l, j: (0, 0)),
            pl.BlockSpec((None, D_MODEL, tn), lambda l, j: (l, 0, j)),
            pl.BlockSpec((None, 1, tn), lambda l, j: (l, 0, j)),
        ],
        out_specs=pl.BlockSpec((None, MOD_ROWS, tn), lambda l, j: (l, 0, j)),
        out_shape=jax.ShapeDtypeStruct((DEPTH, MOD_ROWS, 6 * D_MODEL), f32),
        compiler_params=pltpu.CompilerParams(
            dimension_semantics=("arbitrary", "arbitrary"), vmem_limit_bytes=VMEM_LIMIT),
        name="adaln",
    )(c_all, w_ada, b_ada.reshape(DEPTH, 1, 6 * D_MODEL))


def _rms(x, g):
    return x * lax.rsqrt(jnp.mean(x * x, axis=-1, keepdims=True) + EPS) * g


def _stream_specs(tm, ctx_base):
    lat_tiles = T_LAT // tm
    return [pl.BlockSpec((tm, D_MODEL), lambda i, *_: (jnp.minimum(i, lat_tiles - 1), 0)),
            pl.BlockSpec((tm, D_MODEL), lambda i, *_: (jnp.maximum(i, lat_tiles) - ctx_base, 0))]


def _stream_tile(x_ref, xc_ref):
    lat_tiles = T_LAT // x_ref.shape[0]
    return jnp.where(pl.program_id(0) < lat_tiles, x_ref[...], xc_ref[...])


def _qkv_kernel(x_ref, xc_ref, mod_ref, g_ref, w_ref, cos_ref, sa_ref, sb_ref, qn_ref, kn_ref,
                q_ref, k_ref, v_ref, *, nq, nk, hd, qk_norm, v_ones, chain):
    x_in = _stream_tile(x_ref, xc_ref)
    quarter = hd // 4
    scale = hd ** -0.5 * LOG2E
    for r0 in range(0, x_ref.shape[0], chain):
        r = slice(r0, r0 + chain)
        h = _rms(x_in[r], g_ref[...]) * (1.0 + mod_ref[1:2, :]) + mod_ref[0:1, :]
        qkv = jnp.dot(h.astype(bf16), w_ref[...], preferred_element_type=f32)
        cos, sa, sb = cos_ref[r, :], sa_ref[r, :], sb_ref[r, :]

        def rope(c):
            return c * cos + pltpu.roll(c, quarter, 1) * sa + pltpu.roll(c, LANES - quarter, 1) * sb

        for j in range(nq // LANES):
            c = qkv[:, j * LANES:(j + 1) * LANES]
            if qk_norm:
                c = _rms(c, qn_ref[...])
            q_ref[r, j * LANES:(j + 1) * LANES] = (rope(c) * scale).astype(bf16)
        for j in range(nk // LANES):
            c = qkv[:, nq + j * LANES:nq + (j + 1) * LANES]
            if qk_norm:
                c = _rms(c, kn_ref[...])
            k_ref[r, j * LANES:(j + 1) * LANES] = rope(c).astype(bf16)
        if v_ones:
            for j in range(nk // LANES):
                v_ref[r, 2 * j * LANES:(2 * j + 1) * LANES] = (
                    qkv[:, nq + nk + j * LANES:nq + nk + (j + 1) * LANES].astype(bf16))
                v_ref[r, (2 * j + 1) * LANES:(2 * j + 2) * LANES] = jnp.ones((chain, LANES), bf16)
        else:
            v_ref[r, :] = qkv[:, nq + nk:].astype(bf16)


def _qkv(x_lat, x_ctx, mods, layer, g, w_bf16, tabs, qn, kn, *, hq, hkv, hd, qk_norm, v_ones, tm, chain):
    ctx_base = 0 if x_ctx is x_lat else T_LAT // tm
    nq, nk = hq * hd, hkv * hd
    nv = 2 * nk if v_ones else nk
    per_seq, lat_tiles = SEQ // tm, T_LAT // tm
    seg = functools.partial(_seg_of_tile, tiles_per_seq=per_seq, lat_tiles=lat_tiles)
    tab_idx = lambda i: (jnp.where(i < lat_tiles, i % per_seq, per_seq), 0)
    tab_spec = pl.BlockSpec((tm, LANES), tab_idx)
    row = lambda n: pl.BlockSpec((tm, n), lambda i: (i, 0))
    return pl.pallas_call(
        functools.partial(_qkv_kernel, nq=nq, nk=nk, hd=hd, qk_norm=qk_norm, v_ones=v_ones, chain=chain),
        grid=(T_ALL // tm,),
        in_specs=[
            *_stream_specs(tm, ctx_base),
            pl.BlockSpec((None, None, 6, D_MODEL), lambda i: (layer, seg(i), 0, 0)),
            pl.BlockSpec((1, D_MODEL), lambda i: (0, 0)),
            pl.BlockSpec((D_MODEL, nq + 2 * nk), lambda i: (0, 0)),
            tab_spec, tab_spec, tab_spec,
            pl.BlockSpec((1, LANES), lambda i: (0, 0)),
            pl.BlockSpec((1, LANES), lambda i: (0, 0)),
        ],
        out_specs=[row(nq), row(nk), row(nv)],
        out_shape=[jax.ShapeDtypeStruct((T_ALL, nq), bf16),
                   jax.ShapeDtypeStruct((T_ALL, nk), bf16),
                   jax.ShapeDtypeStruct((T_ALL, nv), bf16)],
        compiler_params=pltpu.CompilerParams(
            dimension_semantics=("arbitrary",), vmem_limit_bytes=VMEM_LIMIT),
        name=f"qkv{layer}",
    )(x_lat, x_ctx, mods, g, w_bf16, *tabs, qn, kn)


def _rope_tables(hd):
    quarter = hd // 4
    inv_freq = jnp.float32(ROPE_BASE) ** (-jnp.arange(quarter, dtype=f32) / quarter)
    t = jnp.arange(SEQ)
    ang_r = (t // GRID_W).astype(f32)[:, None] * inv_freq[None, :]
    ang_c = (t % GRID_W).astype(f32)[:, None] * inv_freq[None, :]
    z = jnp.zeros_like(ang_r)
    cos = jnp.concatenate([jnp.cos(ang_r)] * 2 + [jnp.cos(ang_c)] * 2, axis=-1)
    sa = jnp.concatenate([z, jnp.sin(ang_r), z, jnp.sin(ang_c)], axis=-1)
    sb = jnp.concatenate([-jnp.sin(ang_r), z, -jnp.sin(ang_c), z], axis=-1)
    rep = LANES // hd
    pad = lambda a, v: jnp.concatenate(
        [jnp.tile(a, (1, rep)), jnp.full((TMP, LANES), v, f32)], axis=0)
    return pad(cos, 1.0), pad(sa, 0.0), pad(sb, 0.0)


def _pair_operand(x, g):
    lane = lax.broadcasted_iota(jnp.int32, x.shape, 1)
    swapped = pltpu.roll(x, HD_A, 1)
    lo_src, hi_src = (x, swapped) if g == 0 else (swapped, x)
    lo = jnp.where(lane < HD_A, lo_src, 0.0)
    hi = jnp.where(lane >= HD_A, hi_src, 0.0)
    return jnp.concatenate([lo, hi], axis=0)


def _attend_pairs(q_ref, o_ref, qrows, sink_ref, kcat, vcat, mask):
    n = kcat.shape[0]
    pairs = PAIR_STACK
    rows = pairs * BLOCK
    lane = lax.broadcasted_iota(jnp.int32, (rows, LANES), 1)
    pair_of_row = lax.broadcasted_iota(jnp.int32, (rows, 1), 0) // BLOCK
    ind_row = lax.broadcasted_iota(jnp.int32, (2 * n, LANES), 0)
    ind_lane = lax.broadcasted_iota(jnp.int32, (2 * n, LANES), 1)
    ind = jnp.where(ind_lane == ind_row // n, 1.0, 0.0)
    if mask is not None:
        mask = jnp.concatenate([mask] * pairs, axis=0)
    operands = [(_pair_operand(kcat, g).astype(bf16),
                 jnp.concatenate([_pair_operand(vcat, g), ind], axis=1).astype(bf16))
                for g in range(HKV_A)]
    pairs_per_group = HQ_A // HKV_A // 2
    for c0 in range(0, HQ_A // 2, pairs):
        kp, vp = operands[c0 // pairs_per_group]
        cols = [(c0 + p) * LANES for p in range(pairs)]
        qs = jnp.concatenate([q_ref[qrows, c:c + LANES] for c in cols], axis=0)
        s = lax.dot_general(qs, kp, NT_DIMS, preferred_element_type=f32)
        es, ms, sks = [], [], []
        for hh in range(2):
            sh = s[:, hh * n:(hh + 1) * n]
            if mask is not None:
                sh = jnp.where(mask, sh, NEG)
            sk = jnp.zeros((rows, 1), f32)
            for p in range(pairs):
                sk = jnp.where(pair_of_row == p, sink_ref[cols[p] // HD_A + hh] * LOG2E, sk)
            m = jnp.maximum(jnp.max(sh, axis=-1, keepdims=True), sk)
            es.append(jnp.exp2(sh - m).astype(bf16))
            ms.append(m)
            sks.append(sk)
        oe = jnp.dot(jnp.concatenate(es, axis=1), vp, preferred_element_type=f32)
        inv = [1.0 / (oe[:, LANES + hh:LANES + hh + 1] + jnp.exp2(sks[hh] - ms[hh])) for hh in range(2)]
        o = oe[:, :LANES] * jnp.where(lane < HD_A, inv[0], inv[1])
        for p in range(pairs):
            o_ref[qrows, cols[p]:cols[p] + LANES] = o[p * BLOCK:(p + 1) * BLOCK].astype(bf16)


def _attn_a_kernel(sink_ref, q_ref, k_ref, v_ref, kc_ref, vc_ref, o_ref):
    j = pl.program_id(1)
    nstep = SEQ // (QB_A * BLOCK)
    kc = kc_ref[...].astype(f32)
    vc = vc_ref[...].astype(f32)

    @pl.when(j < nstep)
    def _():
        wlen = 3 * BLOCK
        for hb in range(QB_A):
            blk = j * QB_A + hb
            s0 = pl.multiple_of(jnp.clip((blk - 1) * BLOCK, 0, SEQ - wlen), BLOCK)
            kcat = jnp.concatenate([k_ref[pl.ds(s0, wlen), :].astype(f32), kc], axis=0)
            vcat = jnp.concatenate([v_ref[pl.ds(s0, wlen), :].astype(f32), vc], axis=0)
            qpos = blk * BLOCK + lax.broadcasted_iota(jnp.int32, (BLOCK, wlen + CTX_LEN), 0)
            col = lax.broadcasted_iota(jnp.int32, (BLOCK, wlen + CTX_LEN), 1)
            mask = (jnp.abs(qpos - (s0 + col)) <= WINDOW) | (col >= wlen)
            _attend_pairs(q_ref, o_ref, slice(hb * BLOCK, (hb + 1) * BLOCK), sink_ref, kcat, vcat, mask)

    @pl.when(j >= nstep)
    def _():
        for hb in range(CTX_LEN // BLOCK):
            _attend_pairs(q_ref, o_ref, slice(hb * BLOCK, (hb + 1) * BLOCK), sink_ref, kc, vc, None)


def _attn_a(q, k, v, sink):
    assert CTX_LEN == QB_A * BLOCK
    rows = QB_A * BLOCK
    nstep = SEQ // rows
    nk = HKV_A * HD_A
    qrow = lambda b, j, s: (jnp.where(j < nstep, b * nstep + j, T_LAT // rows + b), 0)
    lat_kv = pl.BlockSpec((SEQ, nk), lambda b, j, s: (b, 0))
    ctx_kv = pl.BlockSpec((CTX_LEN, nk), lambda b, j, s: (T_LAT // CTX_LEN + b, 0))
    return pl.pallas_call(
        _attn_a_kernel,
        grid_spec=pltpu.PrefetchScalarGridSpec(
            num_scalar_prefetch=1,
            grid=(BATCH, nstep + 1),
            in_specs=[pl.BlockSpec((rows, HQ_A * HD_A), qrow), lat_kv, lat_kv, ctx_kv, ctx_kv],
            out_specs=pl.BlockSpec((rows, HQ_A * HD_A), qrow),
        ),
        out_shape=jax.ShapeDtypeStruct((T_ALL, HQ_A * HD_A), bf16),
        compiler_params=pltpu.CompilerParams(
            dimension_semantics=("arbitrary", "arbitrary"), vmem_limit_bytes=VMEM_LIMIT),
        name="attn_a",
    )(sink, q, k, v, k, v)


def _attn_b_kernel(q_ref, k_ref, v_ref, kc_ref, vc_ref, o_ref):
    rep = HQ_B // HKV_B
    for h in range(HQ_B):
        g = h // rep
        hs = slice(h * HD_B, (h + 1) * HD_B)
        gs = slice(g * HD_B, (g + 1) * HD_B)
        vs = slice(2 * g * HD_B, 2 * (g + 1) * HD_B)
        q = q_ref[:, hs]
        s1 = lax.dot_general(q, k_ref[:, gs], NT_DIMS, preferred_element_type=f32)
        s2 = lax.dot_general(q, kc_ref[:, gs], NT_DIMS, preferred_element_type=f32)
        m = jnp.maximum(jnp.max(s1, axis=-1, keepdims=True), jnp.max(s2, axis=-1, keepdims=True))
        e1 = jnp.exp2(s1 - m).astype(bf16)
        e2 = jnp.exp2(s2 - m).astype(bf16)
        oe = (jnp.dot(e1, v_ref[:, vs], preferred_element_type=f32)
              + jnp.dot(e2, vc_ref[:, vs], preferred_element_type=f32))
        o_ref[:, hs] = (oe[:, :HD_B] * (1.0 / oe[:, HD_B:HD_B + 1])).astype(bf16)


def _attn_b(q, k, v):
    nblk = SEQ // BQ_B
    nk = HKV_B * HD_B
    qrow = lambda b, j: (b * nblk + j, 0)
    lat = lambda n: pl.BlockSpec((SEQ, n), lambda b, j: (b, 0))
    ctx = lambda n: pl.BlockSpec((CTX_LEN, n), lambda b, j: (T_LAT // CTX_LEN + b, 0))
    return pl.pallas_call(
        _attn_b_kernel,
        grid=(BATCH, nblk),
        in_specs=[pl.BlockSpec((BQ_B, HQ_B * HD_B), qrow), lat(nk), lat(2 * nk), ctx(nk), ctx(2 * nk)],
        out_specs=pl.BlockSpec((BQ_B, HQ_B * HD_B), qrow),
        out_shape=jax.ShapeDtypeStruct((T_LAT, HQ_B * HD_B), bf16),
        compiler_params=pltpu.CompilerParams(
            dimension_semantics=("arbitrary", "arbitrary"), vmem_limit_bytes=VMEM_LIMIT),
        name="attn_b",
    )(q, k, v, k, v)


def _post_attn_kernel(o_ref, wo_ref, x_ref, xc_ref, mod_ref, g1_ref, g2_ref, wrt_ref, brt_ref,
                      xn_ref, f_ref, tit_ref, ti_ref, tw_ref, cnt_ref):
    x_in = _stream_tile(x_ref, xc_ref)
    halves = []
    for h in range(TMP // TM):
        r = slice(h * TM, (h + 1) * TM)
        a = jnp.dot(o_ref[r, :], wo_ref[...], preferred_element_type=f32)
        x = x_in[r] + mod_ref[2:3, :] * _rms(a, g1_ref[...])
        xn_ref[r, :] = x
        fh = _rms(x, g2_ref[...]) * (1.0 + mod_ref[4:5, :]) + mod_ref[3:4, :]
        f_ref[r, :] = fh.astype(bf16)
        halves.append(fh)
    f = jnp.concatenate(halves, axis=0)
    logits = lax.dot_general(wrt_ref[...], f, NT_DIMS, precision=lax.Precision.HIGHEST,
                             preferred_element_type=f32) + brt_ref[...]
    sub = lax.broadcasted_iota(jnp.int32, logits.shape, 0)
    rest = logits
    top_v, top_i = [], []
    for _ in range(TOP_K):
        m = jnp.max(rest, axis=0, keepdims=True)
        idx = jnp.min(jnp.where(rest == m, sub, N_EXPERTS), axis=0, keepdims=True)
        top_v.append(m)
        top_i.append(idx)
        rest = jnp.where(sub == idx, -jnp.inf, rest)
    es = [jnp.exp(v - top_v[0]) for v in top_v]
    inv = 1.0 / (es[0] + es[1] + es[2] + es[3])
    k8 = lax.broadcasted_iota(jnp.int32, (SLOT_ROWS, TMP), 0)
    tit = jnp.full((SLOT_ROWS, TMP), -1.0, f32)
    twt = jnp.zeros((SLOT_ROWS, TMP), f32)
    for k in range(TOP_K):
        tit = jnp.where(k8 == k, top_i[k].astype(f32), tit)
        twt = jnp.where(k8 == k, es[k] * inv, twt)
    tit_ref[...] = tit
    eye = (lax.broadcasted_iota(jnp.int32, (TM, TM), 0)
           == lax.broadcasted_iota(jnp.int32, (TM, TM), 1)).astype(f32)
    to_rows = lambda t: lax.dot_general(eye, t, NT_DIMS, precision=lax.Precision.HIGHEST,
                                        preferred_element_type=f32)
    lane = lax.broadcasted_iota(jnp.int32, (TM, N_EXPERTS), 1).astype(f32)
    for h in range(TMP // TM):
        cols = slice(h * TM, (h + 1) * TM)
        ti = to_rows(tit[:, cols])
        ti_ref[cols, :] = ti
        tw_ref[cols, :] = to_rows(twt[:, cols])
        cnt = jnp.zeros((TM, N_EXPERTS), f32)
        for k in range(TOP_K):
            cnt = cnt + (ti[:, k:k + 1] == lane).astype(f32)
        cnt_ref[h] = jnp.sum(cnt, axis=0, keepdims=True)


def _post_attn(o, wo_bf16, x_lat, x_ctx, mods, layer, g1, g2, wr, br, n_tiles):
    ctx_base = 0 if x_ctx is x_lat else T_LAT // TMP
    seg = functools.partial(_seg_of_tile, tiles_per_seq=SEQ // TMP, lat_tiles=T_LAT // TMP)
    rows = n_tiles * TM
    row = lambda n: pl.BlockSpec((TMP, n), lambda i: (i, 0))
    const = lambda a, b: pl.BlockSpec((a, b), lambda i: (0, 0))
    return pl.pallas_call(
        _post_attn_kernel,
        grid=(rows // TMP,),
        in_specs=[
            row(D_MODEL), const(D_MODEL, D_MODEL), *_stream_specs(TMP, ctx_base),
            pl.BlockSpec((None, None, 6, D_MODEL), lambda i: (layer, seg(i), 0, 0)),
            const(1, D_MODEL), const(1, D_MODEL), const(N_EXPERTS, D_MODEL), const(N_EXPERTS, 1),
        ],
        out_specs=[row(D_MODEL), row(D_MODEL),
                   pl.BlockSpec((SLOT_ROWS, TMP), lambda i: (0, i)),
                   row(SLOT_ROWS), row(SLOT_ROWS),
                   pl.BlockSpec((TMP // TM, 1, N_EXPERTS), lambda i: (i, 0, 0))],
        out_shape=[jax.ShapeDtypeStruct((rows, D_MODEL), f32),
                   jax.ShapeDtypeStruct((rows, D_MODEL), bf16),
                   jax.ShapeDtypeStruct((SLOT_ROWS, rows), f32),
                   jax.ShapeDtypeStruct((rows, SLOT_ROWS), f32),
                   jax.ShapeDtypeStruct((rows, SLOT_ROWS), f32),
                   jax.ShapeDtypeStruct((n_tiles, 1, N_EXPERTS), f32)],
        compiler_params=pltpu.CompilerParams(
            dimension_semantics=("arbitrary",), vmem_limit_bytes=VMEM_LIMIT),
        name=f"post_attn{layer}",
    )(o, wo_bf16, x_lat, x_ctx, mods, g1, g2, wr.T, br.reshape(N_EXPERTS, 1))


def _ffn_tiles_max(n_tiles):
    rows = n_tiles * (TM * TOP_K + N_EXPERTS * (RUN_ALIGN - 1)) + N_EXPERTS * (TME - RUN_ALIGN)
    return -(-rows // TME)


def _route_tables(cnt, n_tiles_max):
    n = cnt.reshape(-1, N_EXPERTS).astype(jnp.int32)
    run = (n + RUN_ALIGN - 1) // RUN_ALIGN * RUN_ALIGN
    loff = jnp.cumsum(run, axis=1) - run
    tot = jnp.sum(run, axis=0)
    gsz = (tot + TME - 1) // TME * TME
    ends = jnp.cumsum(gsz)
    goff = (ends - gsz)[None, :] + jnp.cumsum(run, axis=0) - run
    c0 = jnp.arange(N_CHUNKS, dtype=jnp.int32) * RUN_ALIGN
    owner = jnp.sum(c0[None, :, None] >= (loff + run)[:, None, :], axis=2)
    shift = jnp.where(owner[:, :, None] == jnp.arange(N_EXPERTS)[None, None, :],
                      (goff - loff)[:, None, :], 0)
    tab = jnp.concatenate([jnp.sum(shift, axis=2) + c0[None, :],
                           jnp.sum(run, axis=1, keepdims=True) // RUN_ALIGN], axis=1)[:, None, :]
    n_active = ends[-1] // TME
    tail = jnp.concatenate([ends - gsz + tot, (gsz - tot) // RUN_ALIGN, n_active[None]])[None, :]
    tile = jnp.arange(n_tiles_max, dtype=jnp.int32)
    te = jnp.sum((jnp.minimum(tile, n_active - 1) * TME)[:, None] >= ends[None, :], axis=1)
    of_tile = te[:, None] == jnp.arange(N_EXPERTS)[None, :]
    used = jnp.sum(jnp.where(of_tile, (ends - gsz + tot)[None, :], 0), axis=1) - tile * TME
    half = (used <= TME // 2) & (tile < n_active)
    ids = jnp.arange(N_EXPERTS, dtype=jnp.int32)
    nonempty = gsz > 0
    later = jnp.where((ids[None, :] > ids[:, None]) & nonempty[None, :], ids[None, :], N_EXPERTS)
    nxt_e = jnp.min(later, axis=1)
    nxt_e = jnp.where(nxt_e == N_EXPERTS, -1, nxt_e)
    grp = jnp.sum(jnp.where(of_tile, (jnp.cumsum(nonempty) - 1)[None, :], 0), axis=1)
    nxt = jnp.sum(jnp.where(of_tile, nxt_e[None, :], 0), axis=1)
    loff_f = loff.astype(f32)
    i32 = lambda a: a.astype(jnp.int32)
    return (tab, tail, loff_f[:, None, :], loff_f[:, :, None], i32(te), i32(half), i32(grp), i32(nxt),
            i32(n_active).reshape(1))


def _local_chunk(c):
    if isinstance(c, int):
        return pl.ds(c * RUN_ALIGN, RUN_ALIGN)
    return pl.ds(pl.multiple_of(c * RUN_ALIGN, RUN_ALIGN), RUN_ALIGN)


def _global_chunk(ctab_ref, h, c):
    return pl.ds(pl.multiple_of(ctab_ref[h, 0, c], RUN_ALIGN), RUN_ALIGN)


def _start_chunks(ctab_ref, h, copy):
    for c in range(MIN_CHUNKS):
        copy(c).start()
    lax.fori_loop(MIN_CHUNKS, ctab_ref[h, 0, N_CHUNKS], lambda c, carry: (copy(c).start(), carry)[1], 0)


def _wait_chunks(ctab_ref, h, copy, bulk):
    bulk.wait()
    lax.fori_loop(MIN_CHUNKS, ctab_ref[h, 0, N_CHUNKS], lambda c, carry: (copy(c).wait(), carry)[1], 0)


def _chunk(base, c):
    return pl.ds(pl.multiple_of(base + c * RUN_ALIGN, RUN_ALIGN), RUN_ALIGN)


def _dispatch_kernel(ctab_ref, ctab_prev_ref, tail_ref, f_ref, tit_ref, loffc_ref, xs_hbm,
                     lbuf, zbuf, sem, zsem):
    i = pl.program_id(0)
    slot = i % 2
    sub = lax.broadcasted_iota(jnp.int32, (N_EXPERTS, TM), 0).astype(f32)
    before = (lax.broadcasted_iota(jnp.int32, (TM, TM), 0)
              < lax.broadcasted_iota(jnp.int32, (TM, TM), 1)).astype(bf16)
    rows = lax.broadcasted_iota(jnp.int32, (LOCAL_ROWS, TM), 0).astype(f32)

    def chunk_copy(ctab, h, b):
        return lambda c: pltpu.make_async_copy(
            lbuf.at[b, _local_chunk(c)], xs_hbm.at[_global_chunk(ctab, h, c)], sem.at[b])

    def bulk_copy(b):
        n = MIN_CHUNKS * RUN_ALIGN
        return pltpu.make_async_copy(lbuf.at[b, 0:n], xs_hbm.at[0:n], sem.at[b])

    def wait_tiles(ctab, s):
        for h in range(SUB):
            b = s * SUB + h
            _wait_chunks(ctab, h, chunk_copy(ctab, h, b), bulk_copy(b))

    for h in range(SUB):
        tit = tit_ref[:, h * TM:(h + 1) * TM]
        hit = [tit[k:k + 1, :] == sub for k in range(TOP_K)]
        cnt = sum(x.astype(f32) for x in hit)
        base = loffc_ref[h] + jnp.dot(cnt.astype(bf16), before, preferred_element_type=f32)
        onehot = jnp.zeros((LOCAL_ROWS, TM), f32)
        for k in range(TOP_K):
            lp = jnp.sum(jnp.where(hit[k], base, 0.0), axis=0, keepdims=True)
            onehot = onehot + (rows == lp).astype(f32)
        b = slot * SUB + h
        lbuf[b] = jnp.dot(onehot.astype(bf16), f_ref[h * TM:(h + 1) * TM, :], preferred_element_type=f32)
        _start_chunks(ctab_ref, h, chunk_copy(ctab_ref, h, b))

    @pl.when(i > 0)
    def _():
        wait_tiles(ctab_prev_ref, 1 - slot)

    tail_copy = lambda e, c: pltpu.make_async_copy(zbuf, xs_hbm.at[_chunk(tail_ref[0, e], c)], zsem)

    def for_each_tail(fn):
        for e in range(N_EXPERTS):
            def body(c, carry, e=e):
                fn(e, c)
                return carry
            lax.fori_loop(0, tail_ref[0, N_EXPERTS + e], body, 0)

    @pl.when(i == pl.num_programs(0) - 1)
    def _():
        zbuf[...] = jnp.zeros_like(zbuf)
        for_each_tail(lambda e, c: tail_copy(e, c).start())
        for_each_tail(lambda e, c: tail_copy(e, c).wait())
        wait_tiles(ctab_ref, slot)
        lbuf[0, 0:TME, :] = jnp.zeros((TME, D_MODEL), f32)
        n_active = tail_ref[0, 2 * N_EXPERTS]
        n_spare = xs_hbm.shape[0] // TME - n_active
        spare_copy = lambda j: pltpu.make_async_copy(
            lbuf.at[0, 0:TME], xs_hbm.at[pl.ds(pl.multiple_of((n_active + j) * TME, TME), TME)], zsem)
        lax.fori_loop(0, n_spare, lambda j, carry: (spare_copy(j).start(), carry)[1], 0)
        lax.fori_loop(0, n_spare, lambda j, carry: (spare_copy(j).wait(), carry)[1], 0)


def _dispatch(f, tit, tab, tail, loffc, layer, n_tiles, n_tiles_max):
    smem = functools.partial(pl.BlockSpec, memory_space=pltpu.SMEM)
    return pl.pallas_call(
        _dispatch_kernel,
        grid=(n_tiles // SUB,),
        in_specs=[
            smem((SUB, 1, N_CHUNKS + 1), lambda i: (i, 0, 0)),
            smem((SUB, 1, N_CHUNKS + 1), lambda i: (jnp.maximum(i - 1, 0), 0, 0)),
            smem((1, 2 * N_EXPERTS + 1), lambda i: (0, 0)),
            pl.BlockSpec((SUB * TM, D_MODEL), lambda i: (i, 0)),
            pl.BlockSpec((SLOT_ROWS, SUB * TM), lambda i: (0, i)),
            pl.BlockSpec((SUB, N_EXPERTS, 1), lambda i: (i, 0, 0)),
        ],
        out_specs=pl.BlockSpec(memory_space=pl.ANY),
        out_shape=jax.ShapeDtypeStruct((n_tiles_max * TME, D_MODEL), f32),
        scratch_shapes=[pltpu.VMEM((2 * SUB, LOCAL_ROWS, D_MODEL), f32),
                        pltpu.VMEM((RUN_ALIGN, D_MODEL), f32),
                        pltpu.SemaphoreType.DMA((2 * SUB,)), pltpu.SemaphoreType.DMA(())],
        compiler_params=pltpu.CompilerParams(
            dimension_semantics=("arbitrary",), vmem_limit_bytes=VMEM_LIMIT),
        name=f"dispatch{layer}",
    )(tab, tab, tail, f, tit, loffc)


def _moe_ffn_kernel(te_ref, half_ref, grp_ref, nxt_ref, na_ref, x_ref, w1_hbm, b1_ref, w2_hbm, b2_ref,
                    y_ref, w1f, w2f, w1b, w2b, sem, *, layer):
    i = pl.program_id(0)
    na = na_ref[0]

    def weight_copies(expert, s):
        return (pltpu.make_async_copy(w1_hbm.at[layer, expert], w1f.at[s], sem.at[0, s]),
                pltpu.make_async_copy(w2_hbm.at[layer, expert], w2f.at[s], sem.at[1, s]))

    def ffn(rows):
        u = jnp.dot(x_ref[rows, :].astype(bf16), w1b[...], preferred_element_type=f32) + b1_ref[...]
        glu = jnp.minimum(u[:, :D_FF], SWIGLU_LIMIT)
        lin = jnp.clip(u[:, D_FF:], -SWIGLU_LIMIT, SWIGLU_LIMIT)
        act = glu * jax.nn.sigmoid(SWIGLU_ALPHA * glu) * (lin + 1.0)
        y_ref[rows, :] = jnp.dot(act.astype(bf16), w2b[...], preferred_element_type=f32) + b2_ref[...]

    @pl.when(i >= na)
    def _():
        y_ref[...] = jnp.zeros_like(y_ref)

    @pl.when(i == 0)
    def _():
        for cp in weight_copies(te_ref[0], 0):
            cp.start()

    @pl.when(i < na)
    def _():
        @pl.when((i == 0) | (te_ref[i] != te_ref[jnp.maximum(i - 1, 0)]))
        def _():
            s = grp_ref[i] % 2
            for cp in weight_copies(te_ref[i], s):
                cp.wait()
            w1b[...] = w1f[s].astype(bf16)
            w2b[...] = w2f[s].astype(bf16)

            @pl.when(nxt_ref[i] >= 0)
            def _():
                for cp in weight_copies(nxt_ref[i], 1 - s):
                    cp.start()

        @pl.when(half_ref[i] == 0)
        def _():
            ffn(slice(0, TME))

        @pl.when(half_ref[i] != 0)
        def _():
            ffn(slice(0, TME // 2))
            y_ref[TME // 2:, :] = jnp.zeros((TME // 2, D_MODEL), f32)


def _moe_ffn(xs_sorted, te, half, grp, nxt, na, w1, b1, w2, b2, layer, n_tiles_max):
    ex = lambda i, te, *_: (layer, te[i], 0, 0)
    return pl.pallas_call(
        functools.partial(_moe_ffn_kernel, layer=layer),
        grid_spec=pltpu.PrefetchScalarGridSpec(
            num_scalar_prefetch=5,
            grid=(n_tiles_max,),
            in_specs=[
                pl.BlockSpec((TME, D_MODEL),
                             lambda i, te, half, grp, nxt, na: (jnp.maximum(jnp.minimum(i, na[0] - 1), 0), 0)),
                pl.BlockSpec(memory_space=pl.ANY),
                pl.BlockSpec((None, None, 1, 2 * D_FF), ex),
                pl.BlockSpec(memory_space=pl.ANY),
                pl.BlockSpec((None, None, 1, D_MODEL), ex),
            ],
            out_specs=pl.BlockSpec((TME, D_MODEL), lambda i, *_: (i, 0)),
            scratch_shapes=[pltpu.VMEM((2, D_MODEL, 2 * D_FF), f32), pltpu.VMEM((2, D_FF, D_MODEL), f32),
                            pltpu.VMEM((D_MODEL, 2 * D_FF), bf16), pltpu.VMEM((D_FF, D_MODEL), bf16),
                            pltpu.SemaphoreType.DMA((2, 2))],
        ),
        out_shape=jax.ShapeDtypeStruct((n_tiles_max * TME, D_MODEL), f32),
        compiler_params=pltpu.CompilerParams(
            dimension_semantics=("arbitrary",), vmem_limit_bytes=VMEM_LIMIT),
        name=f"moe_ffn{layer}",
    )(te, half, grp, nxt, na, xs_sorted, w1, b1.reshape(DEPTH, N_EXPERTS, 1, 2 * D_FF), w2,
      b2.reshape(DEPTH, N_EXPERTS, 1, D_MODEL))


def _combine_kernel(ctab_ref, ctab_next_ref, y_hbm, ti_ref, tw_ref, loffr_ref, x_ref, mod_ref, g_ref,
                    xn_ref, ybuf, sem):
    i = pl.program_id(0)
    slot = i % 2

    def chunk_copy(ctab, h, b):
        return lambda c: pltpu.make_async_copy(
            y_hbm.at[_global_chunk(ctab, h, c)], ybuf.at[b, _local_chunk(c)], sem.at[b])

    def bulk_copy(b):
        n = MIN_CHUNKS * RUN_ALIGN
        return pltpu.make_async_copy(y_hbm.at[0:n], ybuf.at[b, 0:n], sem.at[b])

    def start_tiles(ctab, s):
        for h in range(SUB):
            _start_chunks(ctab, h, chunk_copy(ctab, h, s * SUB + h))

    @pl.when(i == 0)
    def _():
        ybuf[...] = jnp.zeros_like(ybuf)
        start_tiles(ctab_ref, 0)

    @pl.when(i + 1 < pl.num_programs(0))
    def _():
        start_tiles(ctab_next_ref, 1 - slot)

    lane = lax.broadcasted_iota(jnp.int32, (TM, N_EXPERTS), 1).astype(f32)
    before = (lax.broadcasted_iota(jnp.int32, (TM, TM), 1)
              < lax.broadcasted_iota(jnp.int32, (TM, TM), 0)).astype(bf16)
    cols = lax.broadcasted_iota(jnp.int32, (TM, LOCAL_ROWS), 1).astype(f32)
    for h in range(SUB):
        tok = slice(h * TM, (h + 1) * TM)
        b = slot * SUB + h
        ti = ti_ref[tok, :]
        tw = tw_ref[tok, :]
        hit = [ti[:, k:k + 1] == lane for k in range(TOP_K)]
        cnt = sum(x.astype(f32) for x in hit)
        base = loffr_ref[h] + jnp.dot(before, cnt.astype(bf16), preferred_element_type=f32)
        w = jnp.zeros((TM, LOCAL_ROWS), f32)
        for k in range(TOP_K):
            lp = jnp.sum(jnp.where(hit[k], base, 0.0), axis=1, keepdims=True)
            w = jnp.where(cols == lp, tw[:, k:k + 1], w)
        _wait_chunks(ctab_ref, h, chunk_copy(ctab_ref, h, b), bulk_copy(b))
        acc = jnp.dot(w.astype(bf16), ybuf[b].astype(bf16), preferred_element_type=f32)
        xn_ref[tok, :] = x_ref[tok, :] + mod_ref[5:6, :] * _rms(acc, g_ref[...])


def _combine(y, tab, ti, tw, loffr, xs, mods, layer, g, n_tiles):
    rows = SUB * TM
    n_steps = n_tiles // SUB
    seg = functools.partial(_seg_of_tile, tiles_per_seq=SEQ // rows, lat_tiles=T_LAT // rows)
    row = lambda n: pl.BlockSpec((rows, n), lambda i: (i, 0))
    smem = functools.partial(pl.BlockSpec, memory_space=pltpu.SMEM)
    return pl.pallas_call(
        _combine_kernel,
        grid=(n_steps,),
        in_specs=[
            smem((SUB, 1, N_CHUNKS + 1), lambda i: (i, 0, 0)),
            smem((SUB, 1, N_CHUNKS + 1), lambda i: (jnp.minimum(i + 1, n_steps - 1), 0, 0)),
            pl.BlockSpec(memory_space=pl.ANY),
            row(SLOT_ROWS), row(SLOT_ROWS),
            pl.BlockSpec((SUB, 1, N_EXPERTS), lambda i: (i, 0, 0)),
            row(D_MODEL),
            pl.BlockSpec((None, None, 6, D_MODEL), lambda i: (layer, seg(i), 0, 0)),
            pl.BlockSpec((1, D_MODEL), lambda i: (0, 0)),
        ],
        out_specs=row(D_MODEL),
        out_shape=jax.ShapeDtypeStruct((n_tiles * TM, D_MODEL), f32),
        scratch_shapes=[pltpu.VMEM((2 * SUB, LOCAL_ROWS, D_MODEL), f32),
                        pltpu.SemaphoreType.DMA((2 * SUB,))],
        compiler_params=pltpu.CompilerParams(
            dimension_semantics=("arbitrary",), vmem_limit_bytes=VMEM_LIMIT),
        name=f"combine{layer}",
    )(tab, tab, y, ti, tw, loffr, xs, mods, g)


def _moe(f, tit, ti, tw, cnt, xs, mods, layer, g, w1, b1, w2, b2, n_tiles):
    n_tiles_max = _ffn_tiles_max(n_tiles)
    tab, tail, loffr, loffc, te, half, grp, nxt, na = _route_tables(cnt, n_tiles_max)
    xs_sorted = _dispatch(f, tit, tab, tail, loffc, layer, n_tiles, n_tiles_max)
    y = _moe_ffn(xs_sorted, te, half, grp, nxt, na, w1, b1, w2, b2, layer, n_tiles_max)
    return _combine(y, tab, ti, tw, loffr, xs, mods, layer, g, n_tiles)


def kernel(x, c, ctx, c_ctx, w_ada, b_ada, norm_g, a_w_qkv, a_w_o, a_sink, b_w_qkv, b_q_norm, b_k_norm,
           b_w_o, moe_w_router, moe_b_router, moe_w1, moe_b1, moe_w2, moe_b2):
    assert DEPTH == 2 and x.shape == (BATCH, SEQ, D_MODEL) and ctx.shape == (BATCH, CTX_LEN, D_MODEL)
    x_lat, x_ctx = x.reshape(T_LAT, D_MODEL), ctx.reshape(T_CTX, D_MODEL)
    c_all = jnp.concatenate(
        [c, c_ctx[None, :], jnp.zeros((MOD_ROWS - BATCH - 1, D_MODEL), f32)], axis=0)
    mods = _adaln(c_all, w_ada, b_ada).reshape(DEPTH, MOD_ROWS, 6, D_MODEL)
    ones = jnp.ones((1, LANES), f32)
    g = lambda i, j: norm_g[i, j][None, :]

    q, k, v = _qkv(x_lat, x_ctx, mods, 0, g(0, 0), a_w_qkv[0].astype(bf16), _rope_tables(HD_A), ones, ones,
                   hq=HQ_A, hkv=HKV_A, hd=HD_A, qk_norm=False, v_ones=False, tm=TMP, chain=TM)
    o = _attn_a(q, k, v, a_sink[0])
    xs, *routed = _post_attn(o, a_w_o[0].astype(bf16), x_lat, x_ctx, mods, 0, g(0, 1), g(0, 2),
                             moe_w_router[0], moe_b_router[0], ALL_TILES)
    xs = _moe(*routed, xs, mods, 0, g(0, 3), moe_w1, moe_b1, moe_w2, moe_b2, ALL_TILES)

    q, k, v = _qkv(xs, xs, mods, 1, g(1, 0), b_w_qkv[0].astype(bf16), _rope_tables(HD_B),
                   b_q_norm[0][None, :], b_k_norm[0][None, :],
                   hq=HQ_B, hkv=HKV_B, hd=HD_B, qk_norm=True, v_ones=True, tm=TM, chain=TM)
    o = _attn_b(q, k, v)
    xl, *routed = _post_attn(o, b_w_o[0].astype(bf16), xs, xs, mods, 1, g(1, 1), g(1, 2),
                             moe_w_router[1], moe_b_router[1], LAT_TILES)
    xl = _moe(*routed, xl, mods, 1, g(1, 3), moe_w1, moe_b1, moe_w2, moe_b2, LAT_TILES)
    return xl.reshape(BATCH, SEQ, D_MODEL)
```

```python
import functools

import jax
import jax.numpy as jnp
from jax import lax
from jax.experimental import pallas as pl
from jax.experimental.pallas import tpu as pltpu

D_MODEL = 1024
BATCH = 8
SEQ = 2048
DEPTH = 2
GRID_W = 64
CTX_LEN = 256
BLOCK = 128
WINDOW = 128
ROPE_BASE = 10000.0
EPS = 1e-6
HQ_A, HKV_A, HD_A = 16, 2, 64
HQ_B, HKV_B, HD_B = 8, 2, 128
N_EXPERTS = 32
TOP_K = 4
D_FF = D_MODEL
SWIGLU_LIMIT = 7.0
SWIGLU_ALPHA = 1.702

T_LAT = BATCH * SEQ
T_CTX = BATCH * CTX_LEN
T_ALL = T_LAT + T_CTX
LANES = 128
TM = 256
LAT_TILES = T_LAT // TM
ALL_TILES = T_ALL // TM
TILES_PER_SEQ = SEQ // TM
MOD_ROWS = 16
CTX_MOD_ROW = BATCH
TMP = 512
PAIR_STACK = 2
QB_A = 2
SUB = 2
BQ_B = 512
TME = 512
SLOT_ROWS = 8
RUN_ALIGN = 8
LOCAL_ROWS = -(-(TM * TOP_K + N_EXPERTS * (RUN_ALIGN - 1)) // LANES) * LANES
N_CHUNKS = LOCAL_ROWS // RUN_ALIGN
MIN_CHUNKS = TM * TOP_K // RUN_ALIGN
NEG = -1e30
LOG2E = 1.4426950408889634
NT_DIMS = (((1,), (1,)), ((), ()))
VMEM_LIMIT = 56 * 1024 * 1024

f32 = jnp.float32
bf16 = jnp.bfloat16


def _seg_of_tile(i, tiles_per_seq, lat_tiles):
    return jnp.where(i < lat_tiles, i // tiles_per_seq, CTX_MOD_ROW)


def _adaln_kernel(c_ref, w_ref, b_ref, o_ref):
    c = c_ref[...]
    s = c * jax.nn.sigmoid(c)
    o_ref[...] = jnp.dot(s, w_ref[...], precision=lax.Precision.HIGHEST,
                         preferred_element_type=f32) + b_ref[...]


def _adaln(c_all, w_ada, b_ada):
    tn = 1536
    return pl.pallas_call(
        _adaln_kernel,
        grid=(DEPTH, 6 * D_MODEL // tn),
        in_specs=[
            pl.BlockSpec((MOD_ROWS, D_MODEL), lambda l, j: (0, 0)),
            pl.BlockSpec((None, D_MODEL, tn), lambda l, j: (l, 0, j)),
            pl.BlockSpec((None, 1, tn), lambda l, j: (l, 0, j)),
        ],
        out_specs=pl.BlockSpec((None, MOD_ROWS, tn), lambda l, j: (l, 0, j)),
        out_shape=jax.ShapeDtypeStruct((DEPTH, MOD_ROWS, 6 * D_MODEL), f32),
        compiler_params=pltpu.CompilerParams(
            dimension_semantics=("arbitrary", "arbitrary"), vmem_limit_bytes=VMEM_LIMIT),
        name="adaln",
    )(c_all, w_ada, b_ada.reshape(DEPTH, 1, 6 * D_MODEL))


def _rms(x, g):
    return x * lax.rsqrt(jnp.mean(x * x, axis=-1, keepdims=True) + EPS) * g


def _stream_specs(tm, ctx_base):
    lat_tiles = T_LAT // tm
    return [pl.BlockSpec((tm, D_MODEL), lambda i, *_: (jnp.minimum(i, lat_tiles - 1), 0)),
            pl.BlockSpec((tm, D_MODEL), lambda i, *_: (jnp.maximum(i, lat_tiles) - ctx_base, 0))]


def _stream_tile(x_ref, xc_ref):
    lat_tiles = T_LAT // x_ref.shape[0]
    return jnp.where(pl.program_id(0) < lat_tiles, x_ref[...], xc_ref[...])


def _qkv_kernel(x_ref, xc_ref, mod_ref, g_ref, w_ref, cos_ref, sa_ref, sb_ref, qn_ref, kn_ref,
                q_ref, k_ref, v_ref, *, nq, nk, hd, qk_norm, v_ones, chain):
    x_in = _stream_tile(x_ref, xc_ref)
    quarter = hd // 4
    scale = hd ** -0.5 * LOG2E
    for r0 in range(0, x_ref.shape[0], chain):
        r = slice(r0, r0 + chain)
        h = _rms(x_in[r], g_ref[...]) * (1.0 + mod_ref[1:2, :]) + mod_ref[0:1, :]
        qkv = jnp.dot(h.astype(bf16), w_ref[...], preferred_element_type=f32)
        cos, sa, sb = cos_ref[r, :], sa_ref[r, :], sb_ref[r, :]

        def rope(c):
            return c * cos + pltpu.roll(c, quarter, 1) * sa + pltpu.roll(c, LANES - quarter, 1) * sb

        for j in range(nq // LANES):
            c = qkv[:, j * LANES:(j + 1) * LANES]
            if qk_norm:
                c = _rms(c, qn_ref[...])
            q_ref[r, j * LANES:(j + 1) * LANES] = (rope(c) * scale).astype(bf16)
        for j in range(nk // LANES):
            c = qkv[:, nq + j * LANES:nq + (j + 1) * LANES]
            if qk_norm:
                c = _rms(c, kn_ref[...])
            k_ref[r, j * LANES:(j + 1) * LANES] = rope(c).astype(bf16)
        if v_ones:
            for j in range(nk // LANES):
                v_ref[r, 2 * j * LANES:(2 * j + 1) * LANES] = (
                    qkv[:, nq + nk + j * LANES:nq + nk + (j + 1) * LANES].astype(bf16))
                v_ref[r, (2 * j + 1) * LANES:(2 * j + 2) * LANES] = jnp.ones((chain, LANES), bf16)
        else:
            v_ref[r, :] = qkv[:, nq + nk:].astype(bf16)


def _qkv(x_lat, x_ctx, mods, layer, g, w_bf16, tabs, qn, kn, *, hq, hkv, hd, qk_norm, v_ones, tm, chain):
    ctx_base = 0 if x_ctx is x_lat else T_LAT // tm
    nq, nk = hq * hd, hkv * hd
    nv = 2 * nk if v_ones else nk
    per_seq, lat_tiles = SEQ // tm, T_LAT // tm
    seg = functools.partial(_seg_of_tile, tiles_per_seq=per_seq, lat_tiles=lat_tiles)
    tab_idx = lambda i: (jnp.where(i < lat_tiles, i % per_seq, per_seq), 0)
    tab_spec = pl.BlockSpec((tm, LANES), tab_idx)
    row = lambda n: pl.BlockSpec((tm, n), lambda i: (i, 0))
    return pl.pallas_call(
        functools.partial(_qkv_kernel, nq=nq, nk=nk, hd=hd, qk_norm=qk_norm, v_ones=v_ones, chain=chain),
        grid=(T_ALL // tm,),
        in_specs=[
            *_stream_specs(tm, ctx_base),
            pl.BlockSpec((None, None, 6, D_MODEL), lambda i: (layer, seg(i), 0, 0)),
            pl.BlockSpec((1, D_MODEL), lambda i: (0, 0)),
            pl.BlockSpec((D_MODEL, nq + 2 * nk), lambda i: (0, 0)),
            tab_spec, tab_spec, tab_spec,
            pl.BlockSpec((1, LANES), lambda i: (0, 0)),
            pl.BlockSpec((1, LANES), lambda i: (0, 0)),
        ],
        out_specs=[row(nq), row(nk), row(nv)],
        out_shape=[jax.ShapeDtypeStruct((T_ALL, nq), bf16),
                   jax.ShapeDtypeStruct((T_ALL, nk), bf16),
                   jax.ShapeDtypeStruct((T_ALL, nv), bf16)],
        compiler_params=pltpu.CompilerParams(
            dimension_semantics=("arbitrary",), vmem_limit_bytes=VMEM_LIMIT),
        name=f"qkv{layer}",
    )(x_lat, x_ctx, mods, g, w_bf16, *tabs, qn, kn)


def _rope_tables(hd):
    quarter = hd // 4
    inv_freq = jnp.float32(ROPE_BASE) ** (-jnp.arange(quarter, dtype=f32) / quarter)
    t = jnp.arange(SEQ)
    ang_r = (t // GRID_W).astype(f32)[:, None] * inv_freq[None, :]
    ang_c = (t % GRID_W).astype(f32)[:, None] * inv_freq[None, :]
    z = jnp.zeros_like(ang_r)
    cos = jnp.concatenate([jnp.cos(ang_r)] * 2 + [jnp.cos(ang_c)] * 2, axis=-1)
    sa = jnp.concatenate([z, jnp.sin(ang_r), z, jnp.sin(ang_c)], axis=-1)
    sb = jnp.concatenate([-jnp.sin(ang_r), z, -jnp.sin(ang_c), z], axis=-1)
    rep = LANES // hd
    pad = lambda a, v: jnp.concatenate(
        [jnp.tile(a, (1, rep)), jnp.full((TMP, LANES), v, f32)], axis=0)
    return pad(cos, 1.0), pad(sa, 0.0), pad(sb, 0.0)


def _pair_operand(x, g):
    lane = lax.broadcasted_iota(jnp.int32, x.shape, 1)
    swapped = pltpu.roll(x, HD_A, 1)
    lo_src, hi_src = (x, swapped) if g == 0 else (swapped, x)
    lo = jnp.where(lane < HD_A, lo_src, 0.0)
    hi = jnp.where(lane >= HD_A, hi_src, 0.0)
    return jnp.concatenate([lo, hi], axis=0)


def _attend_pairs(q_ref, o_ref, qrows, sink_ref, kcat, vcat, mask):
    n = kcat.shape[0]
    pairs = PAIR_STACK
    rows = pairs * BLOCK
    lane = lax.broadcasted_iota(jnp.int32, (rows, LANES), 1)
    pair_of_row = lax.broadcasted_iota(jnp.int32, (rows, 1), 0) // BLOCK
    ind_row = lax.broadcasted_iota(jnp.int32, (2 * n, LANES), 0)
    ind_lane = lax.broadcasted_iota(jnp.int32, (2 * n, LANES), 1)
    ind = jnp.where(ind_lane == ind_row // n, 1.0, 0.0)
    if mask is not None:
        mask = jnp.concatenate([mask] * pairs, axis=0)
    operands = [(_pair_operand(kcat, g).astype(bf16),
                 jnp.concatenate([_pair_operand(vcat, g), ind], axis=1).astype(bf16))
                for g in range(HKV_A)]
    pairs_per_group = HQ_A // HKV_A // 2
    for c0 in range(0, HQ_A // 2, pairs):
        kp, vp = operands[c0 // pairs_per_group]
        cols = [(c0 + p) * LANES for p in range(pairs)]
        qs = jnp.concatenate([q_ref[qrows, c:c + LANES] for c in cols], axis=0)
        s = lax.dot_general(qs, kp, NT_DIMS, preferred_element_type=f32)
        es, ms, sks = [], [], []
        for hh in range(2):
            sh = s[:, hh * n:(hh + 1) * n]
            if mask is not None:
                sh = jnp.where(mask, sh, NEG)
            sk = jnp.zeros((rows, 1), f32)
            for p in range(pairs):
                sk = jnp.where(pair_of_row == p, sink_ref[cols[p] // HD_A + hh] * LOG2E, sk)
            m = jnp.maximum(jnp.max(sh, axis=-1, keepdims=True), sk)
            es.append(jnp.exp2(sh - m).astype(bf16))
            ms.append(m)
            sks.append(sk)
        oe = jnp.dot(jnp.concatenate(es, axis=1), vp, preferred_element_type=f32)
        inv = [1.0 / (oe[:, LANES + hh:LANES + hh + 1] + jnp.exp2(sks[hh] - ms[hh])) for hh in range(2)]
        o = oe[:, :LANES] * jnp.where(lane < HD_A, inv[0], inv[1])
        for p in range(pairs):
            o_ref[qrows, cols[p]:cols[p] + LANES] = o[p * BLOCK:(p + 1) * BLOCK].astype(bf16)


def _attn_a_kernel(sink_ref, q_ref, k_ref, v_ref, kc_ref, vc_ref, o_ref):
    j = pl.program_id(1)
    nstep = SEQ // (QB_A * BLOCK)
    kc = kc_ref[...].astype(f32)
    vc = vc_ref[...].astype(f32)

    @pl.when(j < nstep)
    def _():
        wlen = 3 * BLOCK
        for hb in range(QB_A):
            blk = j * QB_A + hb
            s0 = pl.multiple_of(jnp.clip((blk - 1) * BLOCK, 0, SEQ - wlen), BLOCK)
            kcat = jnp.concatenate([k_ref[pl.ds(s0, wlen), :].astype(f32), kc], axis=0)
            vcat = jnp.concatenate([v_ref[pl.ds(s0, wlen), :].astype(f32), vc], axis=0)
            qpos = blk * BLOCK + lax.broadcasted_iota(jnp.int32, (BLOCK, wlen + CTX_LEN), 0)
            col = lax.broadcasted_iota(jnp.int32, (BLOCK, wlen + CTX_LEN), 1)
            mask = (jnp.abs(qpos - (s0 + col)) <= WINDOW) | (col >= wlen)
            _attend_pairs(q_ref, o_ref, slice(hb * BLOCK, (hb + 1) * BLOCK), sink_ref, kcat, vcat, mask)

    @pl.when(j >= nstep)
    def _():
        for hb in range(CTX_LEN // BLOCK):
            _attend_pairs(q_ref, o_ref, slice(hb * BLOCK, (hb + 1) * BLOCK), sink_ref, kc, vc, None)


def _attn_a(q, k, v, sink):
    assert CTX_LEN == QB_A * BLOCK
    rows = QB_A * BLOCK
    nstep = SEQ // rows
    nk = HKV_A * HD_A
    qrow = lambda b, j, s: (jnp.where(j < nstep, b * nstep + j, T_LAT // rows + b), 0)
    lat_kv = pl.BlockSpec((SEQ, nk), lambda b, j, s: (b, 0))
    ctx_kv = pl.BlockSpec((CTX_LEN, nk), lambda b, j, s: (T_LAT // CTX_LEN + b, 0))
    return pl.pallas_call(
        _attn_a_kernel,
        grid_spec=pltpu.PrefetchScalarGridSpec(
            num_scalar_prefetch=1,
            grid=(BATCH, nstep + 1),
            in_specs=[pl.BlockSpec((rows, HQ_A * HD_A), qrow), lat_kv, lat_kv, ctx_kv, ctx_kv],
            out_specs=pl.BlockSpec((rows, HQ_A * HD_A), qrow),
        ),
        out_shape=jax.ShapeDtypeStruct((T_ALL, HQ_A * HD_A), bf16),
        compiler_params=pltpu.CompilerParams(
            dimension_semantics=("arbitrary", "arbitrary"), vmem_limit_bytes=VMEM_LIMIT),
        name="attn_a",
    )(sink, q, k, v, k, v)


def _attn_b_kernel(q_ref, k_ref, v_ref, kc_ref, vc_ref, o_ref):
    rep = HQ_B // HKV_B
    for h in range(HQ_B):
        g = h // rep
        hs = slice(h * HD_B, (h + 1) * HD_B)
        gs = slice(g * HD_B, (g + 1) * HD_B)
        vs = slice(2 * g * HD_B, 2 * (g + 1) * HD_B)
        q = q_ref[:, hs]
        s1 = lax.dot_general(q, k_ref[:, gs], NT_DIMS, preferred_element_type=f32)
        s2 = lax.dot_general(q, kc_ref[:, gs], NT_DIMS, preferred_element_type=f32)
        m = jnp.maximum(jnp.max(s1, axis=-1, keepdims=True), jnp.max(s2, axis=-1, keepdims=True))
        e1 = jnp.exp2(s1 - m).astype(bf16)
        e2 = jnp.exp2(s2 - m).astype(bf16)
        oe = (jnp.dot(e1, v_ref[:, vs], preferred_element_type=f32)
              + jnp.dot(e2, vc_ref[:, vs], preferred_element_type=f32))
        o_ref[:, hs] = (oe[:, :HD_B] * (1.0 / oe[:, HD_B:HD_B + 1])).astype(bf16)


def _attn_b(q, k, v):
    nblk = SEQ // BQ_B
    nk = HKV_B * HD_B
    qrow = lambda b, j: (b * nblk + j, 0)
    lat = lambda n: pl.BlockSpec((SEQ, n), lambda b, j: (b, 0))
    ctx = lambda n: pl.BlockSpec((CTX_LEN, n), lambda b, j: (T_LAT // CTX_LEN + b, 0))
    return pl.pallas_call(
        _attn_b_kernel,
        grid=(BATCH, nblk),
        in_specs=[pl.BlockSpec((BQ_B, HQ_B * HD_B), qrow), lat(nk), lat(2 * nk), ctx(nk), ctx(2 * nk)],
        out_specs=pl.BlockSpec((BQ_B, HQ_B * HD_B), qrow),
        out_shape=jax.ShapeDtypeStruct((T_LAT, HQ_B * HD_B), bf16),
        compiler_params=pltpu.CompilerParams(
            dimension_semantics=("arbitrary", "arbitrary"), vmem_limit_bytes=VMEM_LIMIT),
        name="attn_b",
    )(q, k, v, k, v)


def _post_attn_kernel(o_ref, wo_ref, x_ref, xc_ref, mod_ref, g1_ref, g2_ref, wrt_ref, brt_ref,
                      xn_ref, f_ref, tit_ref, ti_ref, tw_ref, cnt_ref):
    x_in = _stream_tile(x_ref, xc_ref)
    halves = []
    for h in range(TMP // TM):
        r = slice(h * TM, (h + 1) * TM)
        a = jnp.dot(o_ref[r, :], wo_ref[...], preferred_element_type=f32)
        x = x_in[r] + mod_ref[2:3, :] * _rms(a, g1_ref[...])
        xn_ref[r, :] = x
        fh = _rms(x, g2_ref[...]) * (1.0 + mod_ref[4:5, :]) + mod_ref[3:4, :]
        f_ref[r, :] = fh.astype(bf16)
        halves.append(fh)
    f = jnp.concatenate(halves, axis=0)
    logits = lax.dot_general(wrt_ref[...], f, NT_DIMS, precision=lax.Precision.HIGHEST,
                             preferred_element_type=f32) + brt_ref[...]
    sub = lax.broadcasted_iota(jnp.int32, logits.shape, 0)
    rest = logits
    top_v, top_i = [], []
    for _ in range(TOP_K):
        m = jnp.max(rest, axis=0, keepdims=True)
        idx = jnp.min(jnp.where(rest == m, sub, N_EXPERTS), axis=0, keepdims=True)
        top_v.append(m)
        top_i.append(idx)
        rest = jnp.where(sub == idx, -jnp.inf, rest)
    es = [jnp.exp(v - top_v[0]) for v in top_v]
    inv = 1.0 / (es[0] + es[1] + es[2] + es[3])
    k8 = lax.broadcasted_iota(jnp.int32, (SLOT_ROWS, TMP), 0)
    tit = jnp.full((SLOT_ROWS, TMP), -1.0, f32)
    twt = jnp.zeros((SLOT_ROWS, TMP), f32)
    for k in range(TOP_K):
        tit = jnp.where(k8 == k, top_i[k].astype(f32), tit)
        twt = jnp.where(k8 == k, es[k] * inv, twt)
    tit_ref[...] = tit
    eye = (lax.broadcasted_iota(jnp.int32, (TM, TM), 0)
           == lax.broadcasted_iota(jnp.int32, (TM, TM), 1)).astype(f32)
    to_rows = lambda t: lax.dot_general(eye, t, NT_DIMS, precision=lax.Precision.HIGHEST,
                                        preferred_element_type=f32)
    lane = lax.broadcasted_iota(jnp.int32, (TM, N_EXPERTS), 1).astype(f32)
    for h in range(TMP // TM):
        cols = slice(h * TM, (h + 1) * TM)
        ti = to_rows(tit[:, cols])
        ti_ref[cols, :] = ti
        tw_ref[cols, :] = to_rows(twt[:, cols])
        cnt = jnp.zeros((TM, N_EXPERTS), f32)
        for k in range(TOP_K):
            cnt = cnt + (ti[:, k:k + 1] == lane).astype(f32)
        cnt_ref[h] = jnp.sum(cnt, axis=0, keepdims=True)


def _post_attn(o, wo_bf16, x_lat, x_ctx, mods, layer, g1, g2, wr, br, n_tiles):
    ctx_base = 0 if x_ctx is x_lat else T_LAT // TMP
    seg = functools.partial(_seg_of_tile, tiles_per_seq=SEQ // TMP, lat_tiles=T_LAT // TMP)
    rows = n_tiles * TM
    row = lambda n: pl.BlockSpec((TMP, n), lambda i: (i, 0))
    const = lambda a, b: pl.BlockSpec((a, b), lambda i: (0, 0))
    return pl.pallas_call(
        _post_attn_kernel,
        grid=(rows // TMP,),
        in_specs=[
            row(D_MODEL), const(D_MODEL, D_MODEL), *_stream_specs(TMP, ctx_base),
            pl.BlockSpec((None, None, 6, D_MODEL), lambda i: (layer, seg(i), 0, 0)),
            const(1, D_MODEL), const(1, D_MODEL), const(N_EXPERTS, D_MODEL), const(N_EXPERTS, 1),
        ],
        out_specs=[row(D_MODEL), row(D_MODEL),
                   pl.BlockSpec((SLOT_ROWS, TMP), lambda i: (0, i)),
                   row(SLOT_ROWS), row(SLOT_ROWS),
                   pl.BlockSpec((TMP // TM, 1, N_EXPERTS), lambda i: (i, 0, 0))],
        out_shape=[jax.ShapeDtypeStruct((rows, D_MODEL), f32),
                   jax.ShapeDtypeStruct((rows, D_MODEL), bf16),
                   jax.ShapeDtypeStruct((SLOT_ROWS, rows), f32),
                   jax.ShapeDtypeStruct((rows, SLOT_ROWS), f32),
                   jax.ShapeDtypeStruct((rows, SLOT_ROWS), f32),
                   jax.ShapeDtypeStruct((n_tiles, 1, N_EXPERTS), f32)],
        compiler_params=pltpu.CompilerParams(
            dimension_semantics=("arbitrary",), vmem_limit_bytes=VMEM_LIMIT),
        name=f"post_attn{layer}",
    )(o, wo_bf16, x_lat, x_ctx, mods, g1, g2, wr.T, br.reshape(N_EXPERTS, 1))


def _ffn_tiles_max(n_tiles):
    rows = n_tiles * (TM * TOP_K + N_EXPERTS * (RUN_ALIGN - 1)) + N_EXPERTS * (TME - RUN_ALIGN)
    return -(-rows // TME)


def _route_tables(cnt, n_tiles_max):
    n = cnt.reshape(-1, N_EXPERTS).astype(jnp.int32)
    run = (n + RUN_ALIGN - 1) // RUN_ALIGN * RUN_ALIGN
    loff = jnp.cumsum(run, axis=1) - run
    tot = jnp.sum(run, axis=0)
    gsz = (tot + TME - 1) // TME * TME
    ends = jnp.cumsum(gsz)
    goff = (ends - gsz)[None, :] + jnp.cumsum(run, axis=0) - run
    c0 = jnp.arange(N_CHUNKS, dtype=jnp.int32) * RUN_ALIGN
    owner = jnp.sum(c0[None, :, None] >= (loff + run)[:, None, :], axis=2)
    shift = jnp.where(owner[:, :, None] == jnp.arange(N_EXPERTS)[None, None, :],
                      (goff - loff)[:, None, :], 0)
    tab = jnp.concatenate([jnp.sum(shift, axis=2) + c0[None, :],
                           jnp.sum(run, axis=1, keepdims=True) // RUN_ALIGN], axis=1)[:, None, :]
    n_active = ends[-1] // TME
    tail = jnp.concatenate([ends - gsz + tot, (gsz - tot) // RUN_ALIGN, n_active[None]])[None, :]
    tile = jnp.arange(n_tiles_max, dtype=jnp.int32)
    te = jnp.sum((jnp.minimum(tile, n_active - 1) * TME)[:, None] >= ends[None, :], axis=1)
    of_tile = te[:, None] == jnp.arange(N_EXPERTS)[None, :]
    used = jnp.sum(jnp.where(of_tile, (ends - gsz + tot)[None, :], 0), axis=1) - tile * TME
    half = (used <= TME // 2) & (tile < n_active)
    ids = jnp.arange(N_EXPERTS, dtype=jnp.int32)
    nonempty = gsz > 0
    later = jnp.where((ids[None, :] > ids[:, None]) & nonempty[None, :], ids[None, :], N_EXPERTS)
    nxt_e = jnp.min(later, axis=1)
    nxt_e = jnp.where(nxt_e == N_EXPERTS, -1, nxt_e)
    grp = jnp.sum(jnp.where(of_tile, (jnp.cumsum(nonempty) - 1)[None, :], 0), axis=1)
    nxt = jnp.sum(jnp.where(of_tile, nxt_e[None, :], 0), axis=1)
    loff_f = loff.astype(f32)
    i32 = lambda a: a.astype(jnp.int32)
    return (tab, tail, loff_f[:, None, :], loff_f[:, :, None], i32(te), i32(half), i32(grp), i32(nxt),
            i32(n_active).reshape(1))


def _local_chunk(c):
    if isinstance(c, int):
        return pl.ds(c * RUN_ALIGN, RUN_ALIGN)
    return pl.ds(pl.multiple_of(c * RUN_ALIGN, RUN_ALIGN), RUN_ALIGN)


def _global_chunk(ctab_ref, h, c):
    return pl.ds(pl.multiple_of(ctab_ref[h, 0, c], RUN_ALIGN), RUN_ALIGN)


def _start_chunks(ctab_ref, h, copy):
    for c in range(MIN_CHUNKS):
        copy(c).start()
    lax.fori_loop(MIN_CHUNKS, ctab_ref[h, 0, N_CHUNKS], lambda c, carry: (copy(c).start(), carry)[1], 0)


def _wait_chunks(ctab_ref, h, copy, bulk):
    bulk.wait()
    lax.fori_loop(MIN_CHUNKS, ctab_ref[h, 0, N_CHUNKS], lambda c, carry: (copy(c).wait(), carry)[1], 0)


def _chunk(base, c):
    return pl.ds(pl.multiple_of(base + c * RUN_ALIGN, RUN_ALIGN), RUN_ALIGN)


def _dispatch_kernel(ctab_ref, ctab_prev_ref, tail_ref, f_ref, tit_ref, loffc_ref, xs_hbm,
                     lbuf, zbuf, sem, zsem):
    i = pl.program_id(0)
    slot = i % 2
    sub = lax.broadcasted_iota(jnp.int32, (N_EXPERTS, TM), 0).astype(f32)
    before = (lax.broadcasted_iota(jnp.int32, (TM, TM), 0)
              < lax.broadcasted_iota(jnp.int32, (TM, TM), 1)).astype(bf16)
    rows = lax.broadcasted_iota(jnp.int32, (LOCAL_ROWS, TM), 0).astype(f32)

    def chunk_copy(ctab, h, b):
        return lambda c: pltpu.make_async_copy(
            lbuf.at[b, _local_chunk(c)], xs_hbm.at[_global_chunk(ctab, h, c)], sem.at[b])

    def bulk_copy(b):
        n = MIN_CHUNKS * RUN_ALIGN
        return pltpu.make_async_copy(lbuf.at[b, 0:n], xs_hbm.at[0:n], sem.at[b])

    def wait_tiles(ctab, s):
        for h in range(SUB):
            b = s * SUB + h
            _wait_chunks(ctab, h, chunk_copy(ctab, h, b), bulk_copy(b))

    for h in range(SUB):
        tit = tit_ref[:, h * TM:(h + 1) * TM]
        hit = [tit[k:k + 1, :] == sub for k in range(TOP_K)]
        cnt = sum(x.astype(f32) for x in hit)
        base = loffc_ref[h] + jnp.dot(cnt.astype(bf16), before, preferred_element_type=f32)
        onehot = jnp.zeros((LOCAL_ROWS, TM), f32)
        for k in range(TOP_K):
            lp = jnp.sum(jnp.where(hit[k], base, 0.0), axis=0, keepdims=True)
            onehot = onehot + (rows == lp).astype(f32)
        b = slot * SUB + h
        lbuf[b] = jnp.dot(onehot.astype(bf16), f_ref[h * TM:(h + 1) * TM, :], preferred_element_type=f32)
        _start_chunks(ctab_ref, h, chunk_copy(ctab_ref, h, b))

    @pl.when(i > 0)
    def _():
        wait_tiles(ctab_prev_ref, 1 - slot)

    tail_copy = lambda e, c: pltpu.make_async_copy(zbuf, xs_hbm.at[_chunk(tail_ref[0, e], c)], zsem)

    def for_each_tail(fn):
        for e in range(N_EXPERTS):
            def body(c, carry, e=e):
                fn(e, c)
                return carry
            lax.fori_loop(0, tail_ref[0, N_EXPERTS + e], body, 0)

    @pl.when(i == pl.num_programs(0) - 1)
    def _():
        zbuf[...] = jnp.zeros_like(zbuf)
        for_each_tail(lambda e, c: tail_copy(e, c).start())
        for_each_tail(lambda e, c: tail_copy(e, c).wait())
        wait_tiles(ctab_ref, slot)
        lbuf[0, 0:TME, :] = jnp.zeros((TME, D_MODEL), f32)
        n_active = tail_ref[0, 2 * N_EXPERTS]
        n_spare = xs_hbm.shape[0] // TME - n_active
        spare_copy = lambda j: pltpu.make_async_copy(
            lbuf.at[0, 0:TME], xs_hbm.at[pl.ds(pl.multiple_of((n_active + j) * TME, TME), TME)], zsem)
        lax.fori_loop(0, n_spare, lambda j, carry: (spare_copy(j).start(), carry)[1], 0)
        lax.fori_loop(0, n_spare, lambda j, carry: (spare_copy(j).wait(), carry)[1], 0)


def _dispatch(f, tit, tab, tail, loffc, layer, n_tiles, n_tiles_max):
    smem = functools.partial(pl.BlockSpec, memory_space=pltpu.SMEM)
    return pl.pallas_call(
        _dispatch_kernel,
        grid=(n_tiles // SUB,),
        in_specs=[
            smem((SUB, 1, N_CHUNKS + 1), lambda i: (i, 0, 0)),
            smem((SUB, 1, N_CHUNKS + 1), lambda i: (jnp.maximum(i - 1, 0), 0, 0)),
            smem((1, 2 * N_EXPERTS + 1), lambda i: (0, 0)),
            pl.BlockSpec((SUB * TM, D_MODEL), lambda i: (i, 0)),
            pl.BlockSpec((SLOT_ROWS, SUB * TM), lambda i: (0, i)),
            pl.BlockSpec((SUB, N_EXPERTS, 1), lambda i: (i, 0, 0)),
        ],
        out_specs=pl.BlockSpec(memory_space=pl.ANY),
        out_shape=jax.ShapeDtypeStruct((n_tiles_max * TME, D_MODEL), f32),
        scratch_shapes=[pltpu.VMEM((2 * SUB, LOCAL_ROWS, D_MODEL), f32),
                        pltpu.VMEM((RUN_ALIGN, D_MODEL), f32),
                        pltpu.SemaphoreType.DMA((2 * SUB,)), pltpu.SemaphoreType.DMA(())],
        compiler_params=pltpu.CompilerParams(
            dimension_semantics=("arbitrary",), vmem_limit_bytes=VMEM_LIMIT),
        name=f"dispatch{layer}",
    )(tab, tab, tail, f, tit, loffc)


def _moe_ffn_kernel(te_ref, half_ref, grp_ref, nxt_ref, na_ref, x_ref, w1_hbm, b1_ref, w2_hbm, b2_ref,
                    y_ref, w1f, w2f, w1b, w2b, sem, *, layer):
    i = pl.program_id(0)
    na = na_ref[0]

    def weight_copies(expert, s):
        return (pltpu.make_async_copy(w1_hbm.at[layer, expert], w1f.at[s], sem.at[0, s]),
                pltpu.make_async_copy(w2_hbm.at[layer, expert], w2f.at[s], sem.at[1, s]))

    def ffn(rows):
        u = jnp.dot(x_ref[rows, :].astype(bf16), w1b[...], preferred_element_type=f32) + b1_ref[...]
        glu = jnp.minimum(u[:, :D_FF], SWIGLU_LIMIT)
        lin = jnp.clip(u[:, D_FF:], -SWIGLU_LIMIT, SWIGLU_LIMIT)
        act = glu * jax.nn.sigmoid(SWIGLU_ALPHA * glu) * (lin + 1.0)
        y_ref[rows, :] = jnp.dot(act.astype(bf16), w2b[...], preferred_element_type=f32) + b2_ref[...]

    @pl.when(i == 0)
    def _():
        for cp in weight_copies(te_ref[0], 0):
            cp.start()

    @pl.when(i < na)
    def _():
        @pl.when((i == 0) | (te_ref[i] != te_ref[jnp.maximum(i - 1, 0)]))
        def _():
            s = grp_ref[i] % 2
            for cp in weight_copies(te_ref[i], s):
                cp.wait()
            w1b[...] = w1f[s].astype(bf16)
            w2b[...] = w2f[s].astype(bf16)

            @pl.when(nxt_ref[i] >= 0)
            def _():
                for cp in weight_copies(nxt_ref[i], 1 - s):
                    cp.start()

        @pl.when(half_ref[i] == 0)
        def _():
            ffn(slice(0, TME // 2))
            ffn(slice(TME // 2, TME))

        @pl.when(half_ref[i] != 0)
        def _():
            ffn(slice(0, TME // 2))
            y_ref[TME // 2:, :] = jnp.zeros((TME // 2, D_MODEL), f32)


def _moe_ffn(xs_sorted, te, half, grp, nxt, na, w1, b1, w2, b2, layer, n_tiles_max):
    ex = lambda i, te, *_: (layer, te[i], 0, 0)
    tile = pl.BlockSpec((TME, D_MODEL),
                        lambda i, te, half, grp, nxt, na: (jnp.maximum(jnp.minimum(i, na[0] - 1), 0), 0))
    n_prefetch = 5
    return pl.pallas_call(
        functools.partial(_moe_ffn_kernel, layer=layer),
        grid_spec=pltpu.PrefetchScalarGridSpec(
            num_scalar_prefetch=n_prefetch,
            grid=(n_tiles_max,),
            in_specs=[
                tile,
                pl.BlockSpec(memory_space=pl.ANY),
                pl.BlockSpec((None, None, 1, 2 * D_FF), ex),
                pl.BlockSpec(memory_space=pl.ANY),
                pl.BlockSpec((None, None, 1, D_MODEL), ex),
            ],
            out_specs=tile,
            scratch_shapes=[pltpu.VMEM((2, D_MODEL, 2 * D_FF), f32), pltpu.VMEM((2, D_FF, D_MODEL), f32),
                            pltpu.VMEM((D_MODEL, 2 * D_FF), bf16), pltpu.VMEM((D_FF, D_MODEL), bf16),
                            pltpu.SemaphoreType.DMA((2, 2))],
        ),
        out_shape=jax.ShapeDtypeStruct((n_tiles_max * TME, D_MODEL), f32),
        input_output_aliases={n_prefetch: 0},
        compiler_params=pltpu.CompilerParams(
            dimension_semantics=("arbitrary",), vmem_limit_bytes=VMEM_LIMIT),
        name=f"moe_ffn{layer}",
    )(te, half, grp, nxt, na, xs_sorted, w1, b1.reshape(DEPTH, N_EXPERTS, 1, 2 * D_FF), w2,
      b2.reshape(DEPTH, N_EXPERTS, 1, D_MODEL))


def _combine_kernel(ctab_ref, ctab_next_ref, y_hbm, ti_ref, tw_ref, loffr_ref, x_ref, mod_ref, g_ref,
                    xn_ref, ybuf, sem):
    i = pl.program_id(0)
    slot = i % 2

    def chunk_copy(ctab, h, b):
        return lambda c: pltpu.make_async_copy(
            y_hbm.at[_global_chunk(ctab, h, c)], ybuf.at[b, _local_chunk(c)], sem.at[b])

    def bulk_copy(b):
        n = MIN_CHUNKS * RUN_ALIGN
        return pltpu.make_async_copy(y_hbm.at[0:n], ybuf.at[b, 0:n], sem.at[b])

    def start_tiles(ctab, s):
        for h in range(SUB):
            _start_chunks(ctab, h, chunk_copy(ctab, h, s * SUB + h))

    @pl.when(i == 0)
    def _():
        ybuf[...] = jnp.zeros_like(ybuf)
        start_tiles(ctab_ref, 0)

    @pl.when(i + 1 < pl.num_programs(0))
    def _():
        start_tiles(ctab_next_ref, 1 - slot)

    lane = lax.broadcasted_iota(jnp.int32, (TM, N_EXPERTS), 1).astype(f32)
    before = (lax.broadcasted_iota(jnp.int32, (TM, TM), 1)
              < lax.broadcasted_iota(jnp.int32, (TM, TM), 0)).astype(bf16)
    cols = lax.broadcasted_iota(jnp.int32, (TM, LOCAL_ROWS), 1).astype(f32)
    for h in range(SUB):
        tok = slice(h * TM, (h + 1) * TM)
        b = slot * SUB + h
        ti = ti_ref[tok, :]
        tw = tw_ref[tok, :]
        hit = [ti[:, k:k + 1] == lane for k in range(TOP_K)]
        cnt = sum(x.astype(f32) for x in hit)
        base = loffr_ref[h] + jnp.dot(before, cnt.astype(bf16), preferred_element_type=f32)
        w = jnp.zeros((TM, LOCAL_ROWS), f32)
        for k in range(TOP_K):
            lp = jnp.sum(jnp.where(hit[k], base, 0.0), axis=1, keepdims=True)
            w = jnp.where(cols == lp, tw[:, k:k + 1], w)
        _wait_chunks(ctab_ref, h, chunk_copy(ctab_ref, h, b), bulk_copy(b))
        acc = jnp.dot(w.astype(bf16), ybuf[b].astype(bf16), preferred_element_type=f32)
        xn_ref[tok, :] = x_ref[tok, :] + mod_ref[5:6, :] * _rms(acc, g_ref[...])


def _combine(y, tab, ti, tw, loffr, xs, mods, layer, g, n_tiles):
    rows = SUB * TM
    n_steps = n_tiles // SUB
    seg = functools.partial(_seg_of_tile, tiles_per_seq=SEQ // rows, lat_tiles=T_LAT // rows)
    row = lambda n: pl.BlockSpec((rows, n), lambda i: (i, 0))
    smem = functools.partial(pl.BlockSpec, memory_space=pltpu.SMEM)
    return pl.pallas_call(
        _combine_kernel,
        grid=(n_steps,),
        in_specs=[
            smem((SUB, 1, N_CHUNKS + 1), lambda i: (i, 0, 0)),
            smem((SUB, 1, N_CHUNKS + 1), lambda i: (jnp.minimum(i + 1, n_steps - 1), 0, 0)),
            pl.BlockSpec(memory_space=pl.ANY),
            row(SLOT_ROWS), row(SLOT_ROWS),
            pl.BlockSpec((SUB, 1, N_EXPERTS), lambda i: (i, 0, 0)),
            row(D_MODEL),
            pl.BlockSpec((None, None, 6, D_MODEL), lambda i: (layer, seg(i), 0, 0)),
            pl.BlockSpec((1, D_MODEL), lambda i: (0, 0)),
        ],
        out_specs=row(D_MODEL),
        out_shape=jax.ShapeDtypeStruct((n_tiles * TM, D_MODEL), f32),
        scratch_shapes=[pltpu.VMEM((2 * SUB, LOCAL_ROWS, D_MODEL), f32),
                        pltpu.SemaphoreType.DMA((2 * SUB,))],
        compiler_params=pltpu.CompilerParams(
            dimension_semantics=("arbitrary",), vmem_limit_bytes=VMEM_LIMIT),
        name=f"combine{layer}",
    )(tab, tab, y, ti, tw, loffr, xs, mods, g)


def _moe(f, tit, ti, tw, cnt, xs, mods, layer, g, w1, b1, w2, b2, n_tiles):
    n_tiles_max = _ffn_tiles_max(n_tiles)
    tab, tail, loffr, loffc, te, half, grp, nxt, na = _route_tables(cnt, n_tiles_max)
    xs_sorted = _dispatch(f, tit, tab, tail, loffc, layer, n_tiles, n_tiles_max)
    y = _moe_ffn(xs_sorted, te, half, grp, nxt, na, w1, b1, w2, b2, layer, n_tiles_max)
    return _combine(y, tab, ti, tw, loffr, xs, mods, layer, g, n_tiles)


def kernel(x, c, ctx, c_ctx, w_ada, b_ada, norm_g, a_w_qkv, a_w_o, a_sink, b_w_qkv, b_q_norm, b_k_norm,
           b_w_o, moe_w_router, moe_b_router, moe_w1, moe_b1, moe_w2, moe_b2):
    assert DEPTH == 2 and x.shape == (BATCH, SEQ, D_MODEL) and ctx.shape == (BATCH, CTX_LEN, D_MODEL)
    x_lat, x_ctx = x.reshape(T_LAT, D_MODEL), ctx.reshape(T_CTX, D_MODEL)
    c_all = jnp.concatenate(
        [c, c_ctx[None, :], jnp.zeros((MOD_ROWS - BATCH - 1, D_MODEL), f32)], axis=0)
    mods = _adaln(c_all, w_ada, b_ada).reshape(DEPTH, MOD_ROWS, 6, D_MODEL)
    ones = jnp.ones((1, LANES), f32)
    g = lambda i, j: norm_g[i, j][None, :]

    q, k, v = _qkv(x_lat, x_ctx, mods, 0, g(0, 0), a_w_qkv[0].astype(bf16), _rope_tables(HD_A), ones, ones,
                   hq=HQ_A, hkv=HKV_A, hd=HD_A, qk_norm=False, v_ones=False, tm=TMP, chain=TM)
    o = _attn_a(q, k, v, a_sink[0])
    xs, *routed = _post_attn(o, a_w_o[0].astype(bf16), x_lat, x_ctx, mods, 0, g(0, 1), g(0, 2),
                             moe_w_router[0], moe_b_router[0], ALL_TILES)
    xs = _moe(*routed, xs, mods, 0, g(0, 3), moe_w1, moe_b1, moe_w2, moe_b2, ALL_TILES)

    q, k, v = _qkv(xs, xs, mods, 1, g(1, 0), b_w_qkv[0].astype(bf16), _rope_tables(HD_B),
                   b_q_norm[0][None, :], b_k_norm[0][None, :],
                   hq=HQ_B, hkv=HKV_B, hd=HD_B, qk_norm=True, v_ones=True, tm=TM, chain=TM)
    o = _attn_b(q, k, v)
    xl, *routed = _post_attn(o, b_w_o[0].astype(bf16), xs, xs, mods, 1, g(1, 1), g(1, 2),
                             moe_w_router[1], moe_b_router[1], LAT_TILES)
    xl = _moe(*routed, xl, mods, 1, g(1, 3), moe_w1, moe_b1, moe_w2, moe_b2, LAT_TILES)
    return xl.reshape(BATCH, SEQ, D_MODEL)
```

```python
import functools

import jax
import jax.numpy as jnp
from jax import lax
from jax.experimental import pallas as pl
from jax.experimental.pallas import tpu as pltpu

D_MODEL = 1024
BATCH = 8
SEQ = 2048
DEPTH = 2
GRID_W = 64
CTX_LEN = 256
BLOCK = 128
WINDOW = 128
ROPE_BASE = 10000.0
EPS = 1e-6
HQ_A, HKV_A, HD_A = 16, 2, 64
HQ_B, HKV_B, HD_B = 8, 2, 128
N_EXPERTS = 32
TOP_K = 4
D_FF = D_MODEL
SWIGLU_LIMIT = 7.0
SWIGLU_ALPHA = 1.702

T_LAT = BATCH * SEQ
T_CTX = BATCH * CTX_LEN
T_ALL = T_LAT + T_CTX
LANES = 128
TM = 256
LAT_TILES = T_LAT // TM
ALL_TILES = T_ALL // TM
TILES_PER_SEQ = SEQ // TM
MOD_ROWS = 16
CTX_MOD_ROW = BATCH
TMP = 1024
PAIR_STACK = 2
QB_A = 2
SUB = 2
BQ_B = 512
TME = 512
SLOT_ROWS = 8
RUN_ALIGN = 8
LOCAL_ROWS = -(-(TM * TOP_K + N_EXPERTS * (RUN_ALIGN - 1)) // LANES) * LANES
N_CHUNKS = LOCAL_ROWS // RUN_ALIGN
MIN_CHUNKS = TM * TOP_K // RUN_ALIGN
NEG = -1e30
LOG2E = 1.4426950408889634
NT_DIMS = (((1,), (1,)), ((), ()))
VMEM_LIMIT = 56 * 1024 * 1024

f32 = jnp.float32
bf16 = jnp.bfloat16


def _seg_of_tile(i, tiles_per_seq, lat_tiles):
    return jnp.where(i < lat_tiles, i // tiles_per_seq, CTX_MOD_ROW)


def _adaln_kernel(c_ref, w_ref, b_ref, o_ref):
    c = c_ref[...]
    s = c * jax.nn.sigmoid(c)
    o_ref[...] = jnp.dot(s, w_ref[...], precision=lax.Precision.HIGHEST,
                         preferred_element_type=f32) + b_ref[...]


def _adaln(c_all, w_ada, b_ada):
    tn = 1536
    return pl.pallas_call(
        _adaln_kernel,
        grid=(DEPTH, 6 * D_MODEL // tn),
        in_specs=[
            pl.BlockSpec((MOD_ROWS, D_MODEL), lambda l, j: (0, 0)),
            pl.BlockSpec((None, D_MODEL, tn), lambda l, j: (l, 0, j)),
            pl.BlockSpec((None, 1, tn), lambda l, j: (l, 0, j)),
        ],
        out_specs=pl.BlockSpec((None, MOD_ROWS, tn), lambda l, j: (l, 0, j)),
        out_shape=jax.ShapeDtypeStruct((DEPTH, MOD_ROWS, 6 * D_MODEL), f32),
        compiler_params=pltpu.CompilerParams(
            dimension_semantics=("arbitrary", "arbitrary"), vmem_limit_bytes=VMEM_LIMIT),
        name="adaln",
    )(c_all, w_ada, b_ada.reshape(DEPTH, 1, 6 * D_MODEL))


def _rms(x, g):
    return x * lax.rsqrt(jnp.mean(x * x, axis=-1, keepdims=True) + EPS) * g


def _stream_specs(tm, ctx_base):
    lat_tiles = T_LAT // tm
    return [pl.BlockSpec((tm, D_MODEL), lambda i, *_: (jnp.minimum(i, lat_tiles - 1), 0)),
            pl.BlockSpec((tm, D_MODEL), lambda i, *_: (jnp.maximum(i, lat_tiles) - ctx_base, 0))]


def _stream_tile(x_ref, xc_ref):
    lat_tiles = T_LAT // x_ref.shape[0]
    return jnp.where(pl.program_id(0) < lat_tiles, x_ref[...], xc_ref[...])


def _qkv_kernel(x_ref, xc_ref, mod_ref, g_ref, w_ref, cos_ref, sa_ref, sb_ref, qn_ref, kn_ref,
                q_ref, k_ref, v_ref, *, nq, nk, hd, qk_norm, v_ones, chain):
    x_in = _stream_tile(x_ref, xc_ref)
    quarter = hd // 4
    scale = hd ** -0.5 * LOG2E
    for r0 in range(0, x_ref.shape[0], chain):
        r = slice(r0, r0 + chain)
        h = _rms(x_in[r], g_ref[...]) * (1.0 + mod_ref[1:2, :]) + mod_ref[0:1, :]
        qkv = jnp.dot(h.astype(bf16), w_ref[...], preferred_element_type=f32)
        cos, sa, sb = cos_ref[r, :], sa_ref[r, :], sb_ref[r, :]

        def rope(c):
            return c * cos + pltpu.roll(c, quarter, 1) * sa + pltpu.roll(c, LANES - quarter, 1) * sb

        for j in range(nq // LANES):
            c = qkv[:, j * LANES:(j + 1) * LANES]
            if qk_norm:
                c = _rms(c, qn_ref[...])
            q_ref[r, j * LANES:(j + 1) * LANES] = (rope(c) * scale).astype(bf16)
        for j in range(nk // LANES):
            c = qkv[:, nq + j * LANES:nq + (j + 1) * LANES]
            if qk_norm:
                c = _rms(c, kn_ref[...])
            k_ref[r, j * LANES:(j + 1) * LANES] = rope(c).astype(bf16)
        if v_ones:
            for j in range(nk // LANES):
                v_ref[r, 2 * j * LANES:(2 * j + 1) * LANES] = (
                    qkv[:, nq + nk + j * LANES:nq + nk + (j + 1) * LANES].astype(bf16))
                v_ref[r, (2 * j + 1) * LANES:(2 * j + 2) * LANES] = jnp.ones((chain, LANES), bf16)
        else:
            v_ref[r, :] = qkv[:, nq + nk:].astype(bf16)


def _qkv(x_lat, x_ctx, mods, layer, g, w_bf16, tabs, qn, kn, *, hq, hkv, hd, qk_norm, v_ones, tm, chain):
    ctx_base = 0 if x_ctx is x_lat else T_LAT // tm
    nq, nk = hq * hd, hkv * hd
    nv = 2 * nk if v_ones else nk
    per_seq, lat_tiles = SEQ // tm, T_LAT // tm
    seg = functools.partial(_seg_of_tile, tiles_per_seq=per_seq, lat_tiles=lat_tiles)
    tab_idx = lambda i: (jnp.where(i < lat_tiles, i % per_seq, per_seq), 0)
    tab_spec = pl.BlockSpec((tm, LANES), tab_idx)
    row = lambda n: pl.BlockSpec((tm, n), lambda i: (i, 0))
    return pl.pallas_call(
        functools.partial(_qkv_kernel, nq=nq, nk=nk, hd=hd, qk_norm=qk_norm, v_ones=v_ones, chain=chain),
        grid=(T_ALL // tm,),
        in_specs=[
            *_stream_specs(tm, ctx_base),
            pl.BlockSpec((None, None, 6, D_MODEL), lambda i: (layer, seg(i), 0, 0)),
            pl.BlockSpec((1, D_MODEL), lambda i: (0, 0)),
            pl.BlockSpec((D_MODEL, nq + 2 * nk), lambda i: (0, 0)),
            tab_spec, tab_spec, tab_spec,
            pl.BlockSpec((1, LANES), lambda i: (0, 0)),
            pl.BlockSpec((1, LANES), lambda i: (0, 0)),
        ],
        out_specs=[row(nq), row(nk), row(nv)],
        out_shape=[jax.ShapeDtypeStruct((T_ALL, nq), bf16),
                   jax.ShapeDtypeStruct((T_ALL, nk), bf16),
                   jax.ShapeDtypeStruct((T_ALL, nv), bf16)],
        compiler_params=pltpu.CompilerParams(
            dimension_semantics=("arbitrary",), vmem_limit_bytes=VMEM_LIMIT),
        name=f"qkv{layer}",
    )(x_lat, x_ctx, mods, g, w_bf16, *tabs, qn, kn)


def _rope_tables(hd):
    quarter = hd // 4
    inv_freq = jnp.float32(ROPE_BASE) ** (-jnp.arange(quarter, dtype=f32) / quarter)
    t = jnp.arange(SEQ)
    ang_r = (t // GRID_W).astype(f32)[:, None] * inv_freq[None, :]
    ang_c = (t % GRID_W).astype(f32)[:, None] * inv_freq[None, :]
    z = jnp.zeros_like(ang_r)
    cos = jnp.concatenate([jnp.cos(ang_r)] * 2 + [jnp.cos(ang_c)] * 2, axis=-1)
    sa = jnp.concatenate([z, jnp.sin(ang_r), z, jnp.sin(ang_c)], axis=-1)
    sb = jnp.concatenate([-jnp.sin(ang_r), z, -jnp.sin(ang_c), z], axis=-1)
    rep = LANES // hd
    pad = lambda a, v: jnp.concatenate(
        [jnp.tile(a, (1, rep)), jnp.full((TMP, LANES), v, f32)], axis=0)
    return pad(cos, 1.0), pad(sa, 0.0), pad(sb, 0.0)


def _pair_operand(x, g):
    lane = lax.broadcasted_iota(jnp.int32, x.shape, 1)
    swapped = pltpu.roll(x, HD_A, 1)
    lo_src, hi_src = (x, swapped) if g == 0 else (swapped, x)
    lo = jnp.where(lane < HD_A, lo_src, 0.0)
    hi = jnp.where(lane >= HD_A, hi_src, 0.0)
    return jnp.concatenate([lo, hi], axis=0)


def _attend_pairs(q_ref, o_ref, qrows, sink_ref, kcat, vcat, mask):
    n = kcat.shape[0]
    pairs = PAIR_STACK
    rows = pairs * BLOCK
    lane = lax.broadcasted_iota(jnp.int32, (rows, LANES), 1)
    pair_of_row = lax.broadcasted_iota(jnp.int32, (rows, 1), 0) // BLOCK
    ind_row = lax.broadcasted_iota(jnp.int32, (2 * n, LANES), 0)
    ind_lane = lax.broadcasted_iota(jnp.int32, (2 * n, LANES), 1)
    ind = jnp.where(ind_lane == ind_row // n, 1.0, 0.0)
    if mask is not None:
        mask = jnp.concatenate([mask] * pairs, axis=0)
    operands = [(_pair_operand(kcat, g).astype(bf16),
                 jnp.concatenate([_pair_operand(vcat, g), ind], axis=1).astype(bf16))
                for g in range(HKV_A)]
    pairs_per_group = HQ_A // HKV_A // 2
    for c0 in range(0, HQ_A // 2, pairs):
        kp, vp = operands[c0 // pairs_per_group]
        cols = [(c0 + p) * LANES for p in range(pairs)]
        qs = jnp.concatenate([q_ref[qrows, c:c + LANES] for c in cols], axis=0)
        s = lax.dot_general(qs, kp, NT_DIMS, preferred_element_type=f32)
        es, ms, sks = [], [], []
        for hh in range(2):
            sh = s[:, hh * n:(hh + 1) * n]
            if mask is not None:
                sh = jnp.where(mask, sh, NEG)
            sk = jnp.zeros((rows, 1), f32)
            for p in range(pairs):
                sk = jnp.where(pair_of_row == p, sink_ref[cols[p] // HD_A + hh] * LOG2E, sk)
            m = jnp.maximum(jnp.max(sh, axis=-1, keepdims=True), sk)
            es.append(jnp.exp2(sh - m).astype(bf16))
            ms.append(m)
            sks.append(sk)
        oe = jnp.dot(jnp.concatenate(es, axis=1), vp, preferred_element_type=f32)
        inv = [1.0 / (oe[:, LANES + hh:LANES + hh + 1] + jnp.exp2(sks[hh] - ms[hh])) for hh in range(2)]
        o = oe[:, :LANES] * jnp.where(lane < HD_A, inv[0], inv[1])
        for p in range(pairs):
            o_ref[qrows, cols[p]:cols[p] + LANES] = o[p * BLOCK:(p + 1) * BLOCK].astype(bf16)


def _attn_a_kernel(sink_ref, q_ref, k_ref, v_ref, kc_ref, vc_ref, o_ref):
    j = pl.program_id(1)
    nstep = SEQ // (QB_A * BLOCK)
    kc = kc_ref[...].astype(f32)
    vc = vc_ref[...].astype(f32)

    @pl.when(j < nstep)
    def _():
        wlen = 3 * BLOCK
        for hb in range(QB_A):
            blk = j * QB_A + hb
            s0 = pl.multiple_of(jnp.clip((blk - 1) * BLOCK, 0, SEQ - wlen), BLOCK)
            kcat = jnp.concatenate([k_ref[pl.ds(s0, wlen), :].astype(f32), kc], axis=0)
            vcat = jnp.concatenate([v_ref[pl.ds(s0, wlen), :].astype(f32), vc], axis=0)
            qpos = blk * BLOCK + lax.broadcasted_iota(jnp.int32, (BLOCK, wlen + CTX_LEN), 0)
            col = lax.broadcasted_iota(jnp.int32, (BLOCK, wlen + CTX_LEN), 1)
            mask = (jnp.abs(qpos - (s0 + col)) <= WINDOW) | (col >= wlen)
            _attend_pairs(q_ref, o_ref, slice(hb * BLOCK, (hb + 1) * BLOCK), sink_ref, kcat, vcat, mask)

    @pl.when(j >= nstep)
    def _():
        for hb in range(CTX_LEN // BLOCK):
            _attend_pairs(q_ref, o_ref, slice(hb * BLOCK, (hb + 1) * BLOCK), sink_ref, kc, vc, None)


def _attn_a(q, k, v, sink):
    assert CTX_LEN == QB_A * BLOCK
    rows = QB_A * BLOCK
    nstep = SEQ // rows
    nk = HKV_A * HD_A
    qrow = lambda b, j, s: (jnp.where(j < nstep, b * nstep + j, T_LAT // rows + b), 0)
    lat_kv = pl.BlockSpec((SEQ, nk), lambda b, j, s: (b, 0))
    ctx_kv = pl.BlockSpec((CTX_LEN, nk), lambda b, j, s: (T_LAT // CTX_LEN + b, 0))
    return pl.pallas_call(
        _attn_a_kernel,
        grid_spec=pltpu.PrefetchScalarGridSpec(
            num_scalar_prefetch=1,
            grid=(BATCH, nstep + 1),
            in_specs=[pl.BlockSpec((rows, HQ_A * HD_A), qrow), lat_kv, lat_kv, ctx_kv, ctx_kv],
            out_specs=pl.BlockSpec((rows, HQ_A * HD_A), qrow),
        ),
        out_shape=jax.ShapeDtypeStruct((T_ALL, HQ_A * HD_A), bf16),
        compiler_params=pltpu.CompilerParams(
            dimension_semantics=("arbitrary", "arbitrary"), vmem_limit_bytes=VMEM_LIMIT),
        name="attn_a",
    )(sink, q, k, v, k, v)


def _attn_b_kernel(q_ref, k_ref, v_ref, kc_ref, vc_ref, o_ref):
    rep = HQ_B // HKV_B
    for h in range(HQ_B):
        g = h // rep
        hs = slice(h * HD_B, (h + 1) * HD_B)
        gs = slice(g * HD_B, (g + 1) * HD_B)
        vs = slice(2 * g * HD_B, 2 * (g + 1) * HD_B)
        q = q_ref[:, hs]
        s1 = lax.dot_general(q, k_ref[:, gs], NT_DIMS, preferred_element_type=f32)
        s2 = lax.dot_general(q, kc_ref[:, gs], NT_DIMS, preferred_element_type=f32)
        m = jnp.maximum(jnp.max(s1, axis=-1, keepdims=True), jnp.max(s2, axis=-1, keepdims=True))
        e1 = jnp.exp2(s1 - m).astype(bf16)
        e2 = jnp.exp2(s2 - m).astype(bf16)
        oe = (jnp.dot(e1, v_ref[:, vs], preferred_element_type=f32)
              + jnp.dot(e2, vc_ref[:, vs], preferred_element_type=f32))
        o_ref[:, hs] = (oe[:, :HD_B] * (1.0 / oe[:, HD_B:HD_B + 1])).astype(bf16)


def _attn_b(q, k, v):
    nblk = SEQ // BQ_B
    nk = HKV_B * HD_B
    qrow = lambda b, j: (b * nblk + j, 0)
    lat = lambda n: pl.BlockSpec((SEQ, n), lambda b, j: (b, 0))
    ctx = lambda n: pl.BlockSpec((CTX_LEN, n), lambda b, j: (T_LAT // CTX_LEN + b, 0))
    return pl.pallas_call(
        _attn_b_kernel,
        grid=(BATCH, nblk),
        in_specs=[pl.BlockSpec((BQ_B, HQ_B * HD_B), qrow), lat(nk), lat(2 * nk), ctx(nk), ctx(2 * nk)],
        out_specs=pl.BlockSpec((BQ_B, HQ_B * HD_B), qrow),
        out_shape=jax.ShapeDtypeStruct((T_LAT, HQ_B * HD_B), bf16),
        compiler_params=pltpu.CompilerParams(
            dimension_semantics=("arbitrary", "arbitrary"), vmem_limit_bytes=VMEM_LIMIT),
        name="attn_b",
    )(q, k, v, k, v)


def _post_attn_kernel(o_ref, wo_ref, x_ref, xc_ref, mod_ref, g1_ref, g2_ref, wrt_ref, brt_ref,
                      xn_ref, f_ref, tit_ref, ti_ref, tw_ref, cnt_ref):
    x_in = _stream_tile(x_ref, xc_ref)
    halves = []
    for h in range(TMP // TM):
        r = slice(h * TM, (h + 1) * TM)
        a = jnp.dot(o_ref[r, :], wo_ref[...], preferred_element_type=f32)
        x = x_in[r] + mod_ref[2:3, :] * _rms(a, g1_ref[...])
        xn_ref[r, :] = x
        fh = _rms(x, g2_ref[...]) * (1.0 + mod_ref[4:5, :]) + mod_ref[3:4, :]
        f_ref[r, :] = fh.astype(bf16)
        halves.append(fh)
    f = jnp.concatenate(halves, axis=0)
    logits = lax.dot_general(wrt_ref[...], f, NT_DIMS, precision=lax.Precision.HIGHEST,
                             preferred_element_type=f32) + brt_ref[...]
    sub = lax.broadcasted_iota(jnp.int32, logits.shape, 0)
    rest = logits
    top_v, top_i = [], []
    for _ in range(TOP_K):
        m = jnp.max(rest, axis=0, keepdims=True)
        idx = jnp.min(jnp.where(rest == m, sub, N_EXPERTS), axis=0, keepdims=True)
        top_v.append(m)
        top_i.append(idx)
        rest = jnp.where(sub == idx, -jnp.inf, rest)
    es = [jnp.exp(v - top_v[0]) for v in top_v]
    inv = 1.0 / (es[0] + es[1] + es[2] + es[3])
    k8 = lax.broadcasted_iota(jnp.int32, (SLOT_ROWS, TMP), 0)
    tit = jnp.full((SLOT_ROWS, TMP), -1.0, f32)
    twt = jnp.zeros((SLOT_ROWS, TMP), f32)
    for k in range(TOP_K):
        tit = jnp.where(k8 == k, top_i[k].astype(f32), tit)
        twt = jnp.where(k8 == k, es[k] * inv, twt)
    tit_ref[...] = tit
    eye = (lax.broadcasted_iota(jnp.int32, (TM, TM), 0)
           == lax.broadcasted_iota(jnp.int32, (TM, TM), 1)).astype(f32)
    to_rows = lambda t: lax.dot_general(eye, t, NT_DIMS, precision=lax.Precision.HIGHEST,
                                        preferred_element_type=f32)
    lane = lax.broadcasted_iota(jnp.int32, (TM, N_EXPERTS), 1).astype(f32)
    for h in range(TMP // TM):
        cols = slice(h * TM, (h + 1) * TM)
        ti = to_rows(tit[:, cols])
        ti_ref[cols, :] = ti
        tw_ref[cols, :] = to_rows(twt[:, cols])
        cnt = jnp.zeros((TM, N_EXPERTS), f32)
        for k in range(TOP_K):
            cnt = cnt + (ti[:, k:k + 1] == lane).astype(f32)
        cnt_ref[h] = jnp.sum(cnt, axis=0, keepdims=True)


def _post_attn(o, wo_bf16, x_lat, x_ctx, mods, layer, g1, g2, wr, br, n_tiles):
    ctx_base = 0 if x_ctx is x_lat else T_LAT // TMP
    seg = functools.partial(_seg_of_tile, tiles_per_seq=SEQ // TMP, lat_tiles=T_LAT // TMP)
    rows = n_tiles * TM
    row = lambda n: pl.BlockSpec((TMP, n), lambda i: (i, 0))
    const = lambda a, b: pl.BlockSpec((a, b), lambda i: (0, 0))
    return pl.pallas_call(
        _post_attn_kernel,
        grid=(rows // TMP,),
        in_specs=[
            row(D_MODEL), const(D_MODEL, D_MODEL), *_stream_specs(TMP, ctx_base),
            pl.BlockSpec((None, None, 6, D_MODEL), lambda i: (layer, seg(i), 0, 0)),
            const(1, D_MODEL), const(1, D_MODEL), const(N_EXPERTS, D_MODEL), const(N_EXPERTS, 1),
        ],
        out_specs=[row(D_MODEL), row(D_MODEL),
                   pl.BlockSpec((SLOT_ROWS, TMP), lambda i: (0, i)),
                   row(SLOT_ROWS), row(SLOT_ROWS),
                   pl.BlockSpec((TMP // TM, 1, N_EXPERTS), lambda i: (i, 0, 0))],
        out_shape=[jax.ShapeDtypeStruct((rows, D_MODEL), f32),
                   jax.ShapeDtypeStruct((rows, D_MODEL), bf16),
                   jax.ShapeDtypeStruct((SLOT_ROWS, rows), f32),
                   jax.ShapeDtypeStruct((rows, SLOT_ROWS), f32),
                   jax.ShapeDtypeStruct((rows, SLOT_ROWS), f32),
                   jax.ShapeDtypeStruct((n_tiles, 1, N_EXPERTS), f32)],
        compiler_params=pltpu.CompilerParams(
            dimension_semantics=("arbitrary",), vmem_limit_bytes=VMEM_LIMIT),
        name=f"post_attn{layer}",
    )(o, wo_bf16, x_lat, x_ctx, mods, g1, g2, wr.T, br.reshape(N_EXPERTS, 1))


def _ffn_tiles_max(n_tiles):
    rows = n_tiles * (TM * TOP_K + N_EXPERTS * (RUN_ALIGN - 1)) + N_EXPERTS * (TME - RUN_ALIGN)
    return -(-rows // TME)


def _route_tables(cnt, n_tiles_max):
    n = cnt.reshape(-1, N_EXPERTS).astype(jnp.int32)
    run = (n + RUN_ALIGN - 1) // RUN_ALIGN * RUN_ALIGN
    loff = jnp.cumsum(run, axis=1) - run
    tot = jnp.sum(run, axis=0)
    gsz = (tot + TME - 1) // TME * TME
    ends = jnp.cumsum(gsz)
    goff = (ends - gsz)[None, :] + jnp.cumsum(run, axis=0) - run
    c0 = jnp.arange(N_CHUNKS, dtype=jnp.int32) * RUN_ALIGN
    owner = jnp.sum(c0[None, :, None] >= (loff + run)[:, None, :], axis=2)
    shift = jnp.where(owner[:, :, None] == jnp.arange(N_EXPERTS)[None, None, :],
                      (goff - loff)[:, None, :], 0)
    tab = jnp.concatenate([jnp.sum(shift, axis=2) + c0[None, :],
                           jnp.sum(run, axis=1, keepdims=True) // RUN_ALIGN], axis=1)[:, None, :]
    n_active = ends[-1] // TME
    tail = jnp.concatenate([ends - gsz + tot, (gsz - tot) // RUN_ALIGN, n_active[None]])[None, :]
    tile = jnp.arange(n_tiles_max, dtype=jnp.int32)
    te = jnp.sum((jnp.minimum(tile, n_active - 1) * TME)[:, None] >= ends[None, :], axis=1)
    of_tile = te[:, None] == jnp.arange(N_EXPERTS)[None, :]
    used = jnp.sum(jnp.where(of_tile, (ends - gsz + tot)[None, :], 0), axis=1) - tile * TME
    half = (used <= TME // 2) & (tile < n_active)
    ids = jnp.arange(N_EXPERTS, dtype=jnp.int32)
    nonempty = gsz > 0
    later = jnp.where((ids[None, :] > ids[:, None]) & nonempty[None, :], ids[None, :], N_EXPERTS)
    nxt_e = jnp.min(later, axis=1)
    nxt_e = jnp.where(nxt_e == N_EXPERTS, -1, nxt_e)
    grp = jnp.sum(jnp.where(of_tile, (jnp.cumsum(nonempty) - 1)[None, :], 0), axis=1)
    nxt = jnp.sum(jnp.where(of_tile, nxt_e[None, :], 0), axis=1)
    loff_f = loff.astype(f32)
    i32 = lambda a: a.astype(jnp.int32)
    return (tab, tail, loff_f[:, None, :], loff_f[:, :, None], i32(te), i32(half), i32(grp), i32(nxt),
            i32(n_active).reshape(1))


def _local_chunk(c):
    if isinstance(c, int):
        return pl.ds(c * RUN_ALIGN, RUN_ALIGN)
    return pl.ds(pl.multiple_of(c * RUN_ALIGN, RUN_ALIGN), RUN_ALIGN)


def _global_chunk(ctab_ref, h, c):
    return pl.ds(pl.multiple_of(ctab_ref[h, 0, c], RUN_ALIGN), RUN_ALIGN)


def _start_chunks(ctab_ref, h, copy):
    for c in range(MIN_CHUNKS):
        copy(c).start()
    lax.fori_loop(MIN_CHUNKS, ctab_ref[h, 0, N_CHUNKS], lambda c, carry: (copy(c).start(), carry)[1], 0)


def _wait_chunks(ctab_ref, h, copy, bulk):
    bulk.wait()
    lax.fori_loop(MIN_CHUNKS, ctab_ref[h, 0, N_CHUNKS], lambda c, carry: (copy(c).wait(), carry)[1], 0)


def _chunk(base, c):
    return pl.ds(pl.multiple_of(base + c * RUN_ALIGN, RUN_ALIGN), RUN_ALIGN)


def _dispatch_kernel(ctab_ref, ctab_prev_ref, tail_ref, f_ref, tit_ref, loffc_ref, xs_hbm,
                     lbuf, zbuf, sem, zsem):
    i = pl.program_id(0)
    slot = i % 2
    sub = lax.broadcasted_iota(jnp.int32, (N_EXPERTS, TM), 0).astype(f32)
    before = (lax.broadcasted_iota(jnp.int32, (TM, TM), 0)
              < lax.broadcasted_iota(jnp.int32, (TM, TM), 1)).astype(bf16)
    rows = lax.broadcasted_iota(jnp.int32, (LOCAL_ROWS, TM), 0).astype(f32)

    def chunk_copy(ctab, h, b):
        return lambda c: pltpu.make_async_copy(
            lbuf.at[b, _local_chunk(c)], xs_hbm.at[_global_chunk(ctab, h, c)], sem.at[b])

    def bulk_copy(b):
        n = MIN_CHUNKS * RUN_ALIGN
        return pltpu.make_async_copy(lbuf.at[b, 0:n], xs_hbm.at[0:n], sem.at[b])

    def wait_tiles(ctab, s):
        for h in range(SUB):
            b = s * SUB + h
            _wait_chunks(ctab, h, chunk_copy(ctab, h, b), bulk_copy(b))

    for h in range(SUB):
        tit = tit_ref[:, h * TM:(h + 1) * TM]
        hit = [tit[k:k + 1, :] == sub for k in range(TOP_K)]
        cnt = sum(x.astype(f32) for x in hit)
        base = loffc_ref[h] + jnp.dot(cnt.astype(bf16), before, preferred_element_type=f32)
        onehot = jnp.zeros((LOCAL_ROWS, TM), f32)
        for k in range(TOP_K):
            lp = jnp.sum(jnp.where(hit[k], base, 0.0), axis=0, keepdims=True)
            onehot = onehot + (rows == lp).astype(f32)
        b = slot * SUB + h
        lbuf[b] = jnp.dot(onehot.astype(bf16), f_ref[h * TM:(h + 1) * TM, :], preferred_element_type=f32)
        _start_chunks(ctab_ref, h, chunk_copy(ctab_ref, h, b))

    @pl.when(i > 0)
    def _():
        wait_tiles(ctab_prev_ref, 1 - slot)

    tail_copy = lambda e, c: pltpu.make_async_copy(zbuf, xs_hbm.at[_chunk(tail_ref[0, e], c)], zsem)

    def for_each_tail(fn):
        for e in range(N_EXPERTS):
            def body(c, carry, e=e):
                fn(e, c)
                return carry
            lax.fori_loop(0, tail_ref[0, N_EXPERTS + e], body, 0)

    @pl.when(i == pl.num_programs(0) - 1)
    def _():
        zbuf[...] = jnp.zeros_like(zbuf)
        for_each_tail(lambda e, c: tail_copy(e, c).start())
        for_each_tail(lambda e, c: tail_copy(e, c).wait())
        wait_tiles(ctab_ref, slot)
        lbuf[0, 0:TME, :] = jnp.zeros((TME, D_MODEL), f32)
        n_active = tail_ref[0, 2 * N_EXPERTS]
        n_spare = xs_hbm.shape[0] // TME - n_active
        spare_copy = lambda j: pltpu.make_async_copy(
            lbuf.at[0, 0:TME], xs_hbm.at[pl.ds(pl.multiple_of((n_active + j) * TME, TME), TME)], zsem)
        lax.fori_loop(0, n_spare, lambda j, carry: (spare_copy(j).start(), carry)[1], 0)
        lax.fori_loop(0, n_spare, lambda j, carry: (spare_copy(j).wait(), carry)[1], 0)


def _dispatch(f, tit, tab, tail, loffc, layer, n_tiles, n_tiles_max):
    smem = functools.partial(pl.BlockSpec, memory_space=pltpu.SMEM)
    return pl.pallas_call(
        _dispatch_kernel,
        grid=(n_tiles // SUB,),
        in_specs=[
            smem((SUB, 1, N_CHUNKS + 1), lambda i: (i, 0, 0)),
            smem((SUB, 1, N_CHUNKS + 1), lambda i: (jnp.maximum(i - 1, 0), 0, 0)),
            smem((1, 2 * N_EXPERTS + 1), lambda i: (0, 0)),
            pl.BlockSpec((SUB * TM, D_MODEL), lambda i: (i, 0)),
            pl.BlockSpec((SLOT_ROWS, SUB * TM), lambda i: (0, i)),
            pl.BlockSpec((SUB, N_EXPERTS, 1), lambda i: (i, 0, 0)),
        ],
        out_specs=pl.BlockSpec(memory_space=pl.ANY),
        out_shape=jax.ShapeDtypeStruct((n_tiles_max * TME, D_MODEL), f32),
        scratch_shapes=[pltpu.VMEM((2 * SUB, LOCAL_ROWS, D_MODEL), f32),
                        pltpu.VMEM((RUN_ALIGN, D_MODEL), f32),
                        pltpu.SemaphoreType.DMA((2 * SUB,)), pltpu.SemaphoreType.DMA(())],
        compiler_params=pltpu.CompilerParams(
            dimension_semantics=("arbitrary",), vmem_limit_bytes=VMEM_LIMIT),
        name=f"dispatch{layer}",
    )(tab, tab, tail, f, tit, loffc)


def _moe_ffn_kernel(te_ref, half_ref, grp_ref, nxt_ref, na_ref, x_ref, w1_hbm, b1_ref, w2_hbm, b2_ref,
                    y_ref, w1f, w2f, w1b, w2b, sem, *, layer):
    i = pl.program_id(0)
    na = na_ref[0]

    def weight_copies(expert, s):
        return (pltpu.make_async_copy(w1_hbm.at[layer, expert], w1f.at[s], sem.at[0, s]),
                pltpu.make_async_copy(w2_hbm.at[layer, expert], w2f.at[s], sem.at[1, s]))

    def ffn(rows):
        u = jnp.dot(x_ref[rows, :].astype(bf16), w1b[...], preferred_element_type=f32) + b1_ref[...]
        glu = jnp.minimum(u[:, :D_FF], SWIGLU_LIMIT)
        lin = jnp.clip(u[:, D_FF:], -SWIGLU_LIMIT, SWIGLU_LIMIT)
        act = glu * jax.nn.sigmoid(SWIGLU_ALPHA * glu) * (lin + 1.0)
        y_ref[rows, :] = jnp.dot(act.astype(bf16), w2b[...], preferred_element_type=f32) + b2_ref[...]

    @pl.when(i == 0)
    def _():
        for cp in weight_copies(te_ref[0], 0):
            cp.start()

    @pl.when(i < na)
    def _():
        @pl.when((i == 0) | (te_ref[i] != te_ref[jnp.maximum(i - 1, 0)]))
        def _():
            s = grp_ref[i] % 2
            for cp in weight_copies(te_ref[i], s):
                cp.wait()
            w1b[...] = w1f[s].astype(bf16)
            w2b[...] = w2f[s].astype(bf16)

            @pl.when(nxt_ref[i] >= 0)
            def _():
                for cp in weight_copies(nxt_ref[i], 1 - s):
                    cp.start()

        @pl.when(half_ref[i] == 0)
        def _():
            ffn(slice(0, TME // 2))
            ffn(slice(TME // 2, TME))

        @pl.when(half_ref[i] != 0)
        def _():
            ffn(slice(0, TME // 2))
            y_ref[TME // 2:, :] = jnp.zeros((TME // 2, D_MODEL), f32)


def _moe_ffn(xs_sorted, te, half, grp, nxt, na, w1, b1, w2, b2, layer, n_tiles_max):
    ex = lambda i, te, *_: (layer, te[i], 0, 0)
    tile = pl.BlockSpec((TME, D_MODEL),
                        lambda i, te, half, grp, nxt, na: (jnp.maximum(jnp.minimum(i, na[0] - 1), 0), 0))
    n_prefetch = 5
    return pl.pallas_call(
        functools.partial(_moe_ffn_kernel, layer=layer),
        grid_spec=pltpu.PrefetchScalarGridSpec(
            num_scalar_prefetch=n_prefetch,
            grid=(n_tiles_max,),
            in_specs=[
                tile,
                pl.BlockSpec(memory_space=pl.ANY),
                pl.BlockSpec((None, None, 1, 2 * D_FF), ex),
                pl.BlockSpec(memory_space=pl.ANY),
                pl.BlockSpec((None, None, 1, D_MODEL), ex),
            ],
            out_specs=tile,
            scratch_shapes=[pltpu.VMEM((2, D_MODEL, 2 * D_FF), f32), pltpu.VMEM((2, D_FF, D_MODEL), f32),
                            pltpu.VMEM((D_MODEL, 2 * D_FF), bf16), pltpu.VMEM((D_FF, D_MODEL), bf16),
                            pltpu.SemaphoreType.DMA((2, 2))],
        ),
        out_shape=jax.ShapeDtypeStruct((n_tiles_max * TME, D_MODEL), f32),
        input_output_aliases={n_prefetch: 0},
        compiler_params=pltpu.CompilerParams(
            dimension_semantics=("arbitrary",), vmem_limit_bytes=VMEM_LIMIT),
        name=f"moe_ffn{layer}",
    )(te, half, grp, nxt, na, xs_sorted, w1, b1.reshape(DEPTH, N_EXPERTS, 1, 2 * D_FF), w2,
      b2.reshape(DEPTH, N_EXPERTS, 1, D_MODEL))


def _combine_kernel(ctab_ref, ctab_next_ref, y_hbm, ti_ref, tw_ref, loffr_ref, x_ref, mod_ref, g_ref,
                    xn_ref, ybuf, sem):
    i = pl.program_id(0)
    slot = i % 2

    def chunk_copy(ctab, h, b):
        return lambda c: pltpu.make_async_copy(
            y_hbm.at[_global_chunk(ctab, h, c)], ybuf.at[b, _local_chunk(c)], sem.at[b])

    def bulk_copy(b):
        n = MIN_CHUNKS * RUN_ALIGN
        return pltpu.make_async_copy(y_hbm.at[0:n], ybuf.at[b, 0:n], sem.at[b])

    def start_tiles(ctab, s):
        for h in range(SUB):
            _start_chunks(ctab, h, chunk_copy(ctab, h, s * SUB + h))

    @pl.when(i == 0)
    def _():
        ybuf[...] = jnp.zeros_like(ybuf)
        start_tiles(ctab_ref, 0)

    @pl.when(i + 1 < pl.num_programs(0))
    def _():
        start_tiles(ctab_next_ref, 1 - slot)

    lane = lax.broadcasted_iota(jnp.int32, (TM, N_EXPERTS), 1).astype(f32)
    before = (lax.broadcasted_iota(jnp.int32, (TM, TM), 1)
              < lax.broadcasted_iota(jnp.int32, (TM, TM), 0)).astype(bf16)
    cols = lax.broadcasted_iota(jnp.int32, (TM, LOCAL_ROWS), 1).astype(f32)
    for h in range(SUB):
        tok = slice(h * TM, (h + 1) * TM)
        b = slot * SUB + h
        ti = ti_ref[tok, :]
        tw = tw_ref[tok, :]
        hit = [ti[:, k:k + 1] == lane for k in range(TOP_K)]
        cnt = sum(x.astype(f32) for x in hit)
        base = loffr_ref[h] + jnp.dot(before, cnt.astype(bf16), preferred_element_type=f32)
        w = jnp.zeros((TM, LOCAL_ROWS), f32)
        for k in range(TOP_K):
            lp = jnp.sum(jnp.where(hit[k], base, 0.0), axis=1, keepdims=True)
            w = jnp.where(cols == lp, tw[:, k:k + 1], w)
        _wait_chunks(ctab_ref, h, chunk_copy(ctab_ref, h, b), bulk_copy(b))
        acc = jnp.dot(w.astype(bf16), ybuf[b].astype(bf16), preferred_element_type=f32)
        xn_ref[tok, :] = x_ref[tok, :] + mod_ref[5:6, :] * _rms(acc, g_ref[...])


def _combine(y, tab, ti, tw, loffr, xs, mods, layer, g, n_tiles):
    rows = SUB * TM
    n_steps = n_tiles // SUB
    seg = functools.partial(_seg_of_tile, tiles_per_seq=SEQ // rows, lat_tiles=T_LAT // rows)
    row = lambda n: pl.BlockSpec((rows, n), lambda i: (i, 0))
    smem = functools.partial(pl.BlockSpec, memory_space=pltpu.SMEM)
    return pl.pallas_call(
        _combine_kernel,
        grid=(n_steps,),
        in_specs=[
            smem((SUB, 1, N_CHUNKS + 1), lambda i: (i, 0, 0)),
            smem((SUB, 1, N_CHUNKS + 1), lambda i: (jnp.minimum(i + 1, n_steps - 1), 0, 0)),
            pl.BlockSpec(memory_space=pl.ANY),
            row(SLOT_ROWS), row(SLOT_ROWS),
            pl.BlockSpec((SUB, 1, N_EXPERTS), lambda i: (i, 0, 0)),
            row(D_MODEL),
            pl.BlockSpec((None, None, 6, D_MODEL), lambda i: (layer, seg(i), 0, 0)),
            pl.BlockSpec((1, D_MODEL), lambda i: (0, 0)),
        ],
        out_specs=row(D_MODEL),
        out_shape=jax.ShapeDtypeStruct((n_tiles * TM, D_MODEL), f32),
        scratch_shapes=[pltpu.VMEM((2 * SUB, LOCAL_ROWS, D_MODEL), f32),
                        pltpu.SemaphoreType.DMA((2 * SUB,))],
        compiler_params=pltpu.CompilerParams(
            dimension_semantics=("arbitrary",), vmem_limit_bytes=VMEM_LIMIT),
        name=f"combine{layer}",
    )(tab, tab, y, ti, tw, loffr, xs, mods, g)


def _moe(f, tit, ti, tw, cnt, xs, mods, layer, g, w1, b1, w2, b2, n_tiles):
    n_tiles_max = _ffn_tiles_max(n_tiles)
    tab, tail, loffr, loffc, te, half, grp, nxt, na = _route_tables(cnt, n_tiles_max)
    xs_sorted = _dispatch(f, tit, tab, tail, loffc, layer, n_tiles, n_tiles_max)
    y = _moe_ffn(xs_sorted, te, half, grp, nxt, na, w1, b1, w2, b2, layer, n_tiles_max)
    return _combine(y, tab, ti, tw, loffr, xs, mods, layer, g, n_tiles)


def kernel(x, c, ctx, c_ctx, w_ada, b_ada, norm_g, a_w_qkv, a_w_o, a_sink, b_w_qkv, b_q_norm, b_k_norm,
           b_w_o, moe_w_router, moe_b_router, moe_w1, moe_b1, moe_w2, moe_b2):
    assert DEPTH == 2 and x.shape == (BATCH, SEQ, D_MODEL) and ctx.shape == (BATCH, CTX_LEN, D_MODEL)
    x_lat, x_ctx = x.reshape(T_LAT, D_MODEL), ctx.reshape(T_CTX, D_MODEL)
    c_all = jnp.concatenate(
        [c, c_ctx[None, :], jnp.zeros((MOD_ROWS - BATCH - 1, D_MODEL), f32)], axis=0)
    mods = _adaln(c_all, w_ada, b_ada).reshape(DEPTH, MOD_ROWS, 6, D_MODEL)
    ones = jnp.ones((1, LANES), f32)
    g = lambda i, j: norm_g[i, j][None, :]

    q, k, v = _qkv(x_lat, x_ctx, mods, 0, g(0, 0), a_w_qkv[0].astype(bf16), _rope_tables(HD_A), ones, ones,
                   hq=HQ_A, hkv=HKV_A, hd=HD_A, qk_norm=False, v_ones=False, tm=TMP, chain=TM)
    o = _attn_a(q, k, v, a_sink[0])
    xs, *routed = _post_attn(o, a_w_o[0].astype(bf16), x_lat, x_ctx, mods, 0, g(0, 1), g(0, 2),
                             moe_w_router[0], moe_b_router[0], ALL_TILES)
    xs = _moe(*routed, xs, mods, 0, g(0, 3), moe_w1, moe_b1, moe_w2, moe_b2, ALL_TILES)

    q, k, v = _qkv(xs, xs, mods, 1, g(1, 0), b_w_qkv[0].astype(bf16), _rope_tables(HD_B),
                   b_q_norm[0][None, :], b_k_norm[0][None, :],
                   hq=HQ_B, hkv=HKV_B, hd=HD_B, qk_norm=True, v_ones=True, tm=TM, chain=TM)
    o = _attn_b(q, k, v)
    xl, *routed = _post_attn(o, b_w_o[0].astype(bf16), xs, xs, mods, 1, g(1, 1), g(1, 2),
                             moe_w_router[1], moe_b_router[1], LAT_TILES)
    xl = _moe(*routed, xl, mods, 1, g(1, 3), moe_w1, moe_b1, moe_w2, moe_b2, LAT_TILES)
    return xl.reshape(BATCH, SEQ, D_MODEL)
```

```python
import functools

import jax
import jax.numpy as jnp
from jax import lax
from jax.experimental import pallas as pl
from jax.experimental.pallas import tpu as pltpu

D_MODEL = 1024
BATCH = 8
SEQ = 2048
DEPTH = 2
GRID_W = 64
CTX_LEN = 256
BLOCK = 128
WINDOW = 128
ROPE_BASE = 10000.0
EPS = 1e-6
HQ_A, HKV_A, HD_A = 16, 2, 64
HQ_B, HKV_B, HD_B = 8, 2, 128
N_EXPERTS = 32
TOP_K = 4
D_FF = D_MODEL
SWIGLU_LIMIT = 7.0
SWIGLU_ALPHA = 1.702

T_LAT = BATCH * SEQ
T_CTX = BATCH * CTX_LEN
T_ALL = T_LAT + T_CTX
LANES = 128
TM = 256
LAT_TILES = T_LAT // TM
ALL_TILES = T_ALL // TM
MOD_ROWS = 16
CTX_MOD_ROW = BATCH
TMP = 1024
PAIR_STACK = 2
QB_A = 2
SUB = 2
BQ_B = 1024
TME = 512
SLOT_ROWS = 8
RUN_ALIGN = 8
LOCAL_ROWS = -(-(TM * TOP_K + N_EXPERTS * (RUN_ALIGN - 1)) // LANES) * LANES
N_CHUNKS = LOCAL_ROWS // RUN_ALIGN
MIN_CHUNKS = TM * TOP_K // RUN_ALIGN
NEG = -1e30
LOG2E = 1.4426950408889634
NT_DIMS = (((1,), (1,)), ((), ()))
VMEM_LIMIT = 56 * 1024 * 1024

f32 = jnp.float32
bf16 = jnp.bfloat16


def _seg_of_tile(i, tiles_per_seq, lat_tiles):
    return jnp.where(i < lat_tiles, i // tiles_per_seq, CTX_MOD_ROW)


def _adaln_kernel(c_ref, w_ref, b_ref, o_ref):
    c = c_ref[...]
    s = c * jax.nn.sigmoid(c)
    o_ref[...] = jnp.dot(s, w_ref[...], precision=lax.Precision.HIGHEST,
                         preferred_element_type=f32) + b_ref[...]


def _adaln(c_all, w_ada, b_ada):
    tn = 3072
    return pl.pallas_call(
        _adaln_kernel,
        grid=(DEPTH, 6 * D_MODEL // tn),
        in_specs=[
            pl.BlockSpec((MOD_ROWS, D_MODEL), lambda l, j: (0, 0)),
            pl.BlockSpec((None, D_MODEL, tn), lambda l, j: (l, 0, j)),
            pl.BlockSpec((None, 1, tn), lambda l, j: (l, 0, j)),
        ],
        out_specs=pl.BlockSpec((None, MOD_ROWS, tn), lambda l, j: (l, 0, j)),
        out_shape=jax.ShapeDtypeStruct((DEPTH, MOD_ROWS, 6 * D_MODEL), f32),
        compiler_params=pltpu.CompilerParams(
            dimension_semantics=("arbitrary", "arbitrary"), vmem_limit_bytes=VMEM_LIMIT),
        name="adaln",
    )(c_all, w_ada, b_ada.reshape(DEPTH, 1, 6 * D_MODEL))


def _rms(x, g):
    return x * lax.rsqrt(jnp.mean(x * x, axis=-1, keepdims=True) + EPS) * g


def _stream_specs(tm, ctx_base):
    lat_tiles = T_LAT // tm
    return [pl.BlockSpec((tm, D_MODEL), lambda i, *_: (jnp.minimum(i, lat_tiles - 1), 0)),
            pl.BlockSpec((tm, D_MODEL), lambda i, *_: (jnp.maximum(i, lat_tiles) - ctx_base, 0))]


def _stream_tile(x_ref, xc_ref):
    lat_tiles = T_LAT // x_ref.shape[0]
    return jnp.where(pl.program_id(0) < lat_tiles, x_ref[...], xc_ref[...])


def _qkv_kernel(x_ref, xc_ref, mod_ref, g_ref, w_ref, cos_ref, sa_ref, sb_ref, qn_ref, kn_ref,
                q_ref, k_ref, v_ref, *, nq, nk, hd, qk_norm, v_ones, chain):
    x_in = _stream_tile(x_ref, xc_ref)
    quarter = hd // 4
    scale = hd ** -0.5 * LOG2E
    for r0 in range(0, x_ref.shape[0], chain):
        r = slice(r0, r0 + chain)
        h = _rms(x_in[r], g_ref[...]) * (1.0 + mod_ref[1:2, :]) + mod_ref[0:1, :]
        qkv = jnp.dot(h.astype(bf16), w_ref[...], preferred_element_type=f32)
        cos, sa, sb = cos_ref[r, :], sa_ref[r, :], sb_ref[r, :]

        def rope(c):
            return c * cos + pltpu.roll(c, quarter, 1) * sa + pltpu.roll(c, LANES - quarter, 1) * sb

        for j in range(nq // LANES):
            c = qkv[:, j * LANES:(j + 1) * LANES]
            if qk_norm:
                c = _rms(c, qn_ref[...])
            q_ref[r, j * LANES:(j + 1) * LANES] = (rope(c) * scale).astype(bf16)
        for j in range(nk // LANES):
            c = qkv[:, nq + j * LANES:nq + (j + 1) * LANES]
            if qk_norm:
                c = _rms(c, kn_ref[...])
            k_ref[r, j * LANES:(j + 1) * LANES] = rope(c).astype(bf16)
        if v_ones:
            for j in range(nk // LANES):
                v_ref[r, 2 * j * LANES:(2 * j + 1) * LANES] = (
                    qkv[:, nq + nk + j * LANES:nq + nk + (j + 1) * LANES].astype(bf16))
                v_ref[r, (2 * j + 1) * LANES:(2 * j + 2) * LANES] = jnp.ones((chain, LANES), bf16)
        else:
            v_ref[r, :] = qkv[:, nq + nk:].astype(bf16)


def _qkv(x_lat, x_ctx, mods, layer, g, w_bf16, tabs, qn, kn, *, hq, hkv, hd, qk_norm, v_ones, tm, chain):
    ctx_base = 0 if x_ctx is x_lat else T_LAT // tm
    nq, nk = hq * hd, hkv * hd
    nv = 2 * nk if v_ones else nk
    per_seq, lat_tiles = SEQ // tm, T_LAT // tm
    seg = functools.partial(_seg_of_tile, tiles_per_seq=per_seq, lat_tiles=lat_tiles)
    tab_idx = lambda i: (jnp.where(i < lat_tiles, i % per_seq, per_seq), 0)
    tab_spec = pl.BlockSpec((tm, LANES), tab_idx)
    row = lambda n: pl.BlockSpec((tm, n), lambda i: (i, 0))
    return pl.pallas_call(
        functools.partial(_qkv_kernel, nq=nq, nk=nk, hd=hd, qk_norm=qk_norm, v_ones=v_ones, chain=chain),
        grid=(T_ALL // tm,),
        in_specs=[
            *_stream_specs(tm, ctx_base),
            pl.BlockSpec((None, None, 6, D_MODEL), lambda i: (layer, seg(i), 0, 0)),
            pl.BlockSpec((1, D_MODEL), lambda i: (0, 0)),
            pl.BlockSpec((D_MODEL, nq + 2 * nk), lambda i: (0, 0)),
            tab_spec, tab_spec, tab_spec,
            pl.BlockSpec((1, LANES), lambda i: (0, 0)),
            pl.BlockSpec((1, LANES), lambda i: (0, 0)),
        ],
        out_specs=[row(nq), row(nk), row(nv)],
        out_shape=[jax.ShapeDtypeStruct((T_ALL, nq), bf16),
                   jax.ShapeDtypeStruct((T_ALL, nk), bf16),
                   jax.ShapeDtypeStruct((T_ALL, nv), bf16)],
        compiler_params=pltpu.CompilerParams(
            dimension_semantics=("arbitrary",), vmem_limit_bytes=VMEM_LIMIT),
        name=f"qkv{layer}",
    )(x_lat, x_ctx, mods, g, w_bf16, *tabs, qn, kn)


def _rope_tables(hd):
    quarter = hd // 4
    inv_freq = jnp.float32(ROPE_BASE) ** (-jnp.arange(quarter, dtype=f32) / quarter)
    t = jnp.arange(SEQ)
    ang_r = (t // GRID_W).astype(f32)[:, None] * inv_freq[None, :]
    ang_c = (t % GRID_W).astype(f32)[:, None] * inv_freq[None, :]
    z = jnp.zeros_like(ang_r)
    cos = jnp.concatenate([jnp.cos(ang_r)] * 2 + [jnp.cos(ang_c)] * 2, axis=-1)
    sa = jnp.concatenate([z, jnp.sin(ang_r), z, jnp.sin(ang_c)], axis=-1)
    sb = jnp.concatenate([-jnp.sin(ang_r), z, -jnp.sin(ang_c), z], axis=-1)
    rep = LANES // hd
    pad = lambda a, v: jnp.concatenate(
        [jnp.tile(a, (1, rep)), jnp.full((TMP, LANES), v, f32)], axis=0)
    return pad(cos, 1.0), pad(sa, 0.0), pad(sb, 0.0)


def _pair_operand(x, g):
    lane = lax.broadcasted_iota(jnp.int32, x.shape, 1)
    swapped = pltpu.roll(x, HD_A, 1)
    lo_src, hi_src = (x, swapped) if g == 0 else (swapped, x)
    lo = jnp.where(lane < HD_A, lo_src, 0.0)
    hi = jnp.where(lane >= HD_A, hi_src, 0.0)
    return jnp.concatenate([lo, hi], axis=0)


def _attend_pairs(q_ref, o_ref, qrows, sink_ref, kcat, vcat, mask):
    n = kcat.shape[0]
    pairs = PAIR_STACK
    rows = pairs * BLOCK
    lane = lax.broadcasted_iota(jnp.int32, (rows, LANES), 1)
    pair_of_row = lax.broadcasted_iota(jnp.int32, (rows, 1), 0) // BLOCK
    ind_row = lax.broadcasted_iota(jnp.int32, (2 * n, LANES), 0)
    ind_lane = lax.broadcasted_iota(jnp.int32, (2 * n, LANES), 1)
    ind = jnp.where(ind_lane == ind_row // n, 1.0, 0.0)
    if mask is not None:
        mask = jnp.concatenate([mask] * pairs, axis=0)
    operands = [(_pair_operand(kcat, g).astype(bf16),
                 jnp.concatenate([_pair_operand(vcat, g), ind], axis=1).astype(bf16))
                for g in range(HKV_A)]
    pairs_per_group = HQ_A // HKV_A // 2
    for c0 in range(0, HQ_A // 2, pairs):
        kp, vp = operands[c0 // pairs_per_group]
        cols = [(c0 + p) * LANES for p in range(pairs)]
        qs = jnp.concatenate([q_ref[qrows, c:c + LANES] for c in cols], axis=0)
        s = lax.dot_general(qs, kp, NT_DIMS, preferred_element_type=f32)
        es, ms, sks = [], [], []
        for hh in range(2):
            sh = s[:, hh * n:(hh + 1) * n]
            if mask is not None:
                sh = jnp.where(mask, sh, NEG)
            sk = jnp.zeros((rows, 1), f32)
            for p in range(pairs):
                sk = jnp.where(pair_of_row == p, sink_ref[cols[p] // HD_A + hh] * LOG2E, sk)
            m = jnp.maximum(jnp.max(sh, axis=-1, keepdims=True), sk)
            es.append(jnp.exp2(sh - m).astype(bf16))
            ms.append(m)
            sks.append(sk)
        oe = jnp.dot(jnp.concatenate(es, axis=1), vp, preferred_element_type=f32)
        inv = [1.0 / (oe[:, LANES + hh:LANES + hh + 1] + jnp.exp2(sks[hh] - ms[hh])) for hh in range(2)]
        o = oe[:, :LANES] * jnp.where(lane < HD_A, inv[0], inv[1])
        for p in range(pairs):
            o_ref[qrows, cols[p]:cols[p] + LANES] = o[p * BLOCK:(p + 1) * BLOCK].astype(bf16)


def _attn_a_kernel(sink_ref, q_ref, k_ref, v_ref, kc_ref, vc_ref, o_ref):
    j = pl.program_id(1)
    nstep = SEQ // (QB_A * BLOCK)
    kc = kc_ref[...].astype(f32)
    vc = vc_ref[...].astype(f32)

    @pl.when(j < nstep)
    def _():
        wlen = 3 * BLOCK
        for hb in range(QB_A):
            blk = j * QB_A + hb
            s0 = pl.multiple_of(jnp.clip((blk - 1) * BLOCK, 0, SEQ - wlen), BLOCK)
            kcat = jnp.concatenate([k_ref[pl.ds(s0, wlen), :].astype(f32), kc], axis=0)
            vcat = jnp.concatenate([v_ref[pl.ds(s0, wlen), :].astype(f32), vc], axis=0)
            qpos = blk * BLOCK + lax.broadcasted_iota(jnp.int32, (BLOCK, wlen + CTX_LEN), 0)
            col = lax.broadcasted_iota(jnp.int32, (BLOCK, wlen + CTX_LEN), 1)
            mask = (jnp.abs(qpos - (s0 + col)) <= WINDOW) | (col >= wlen)
            _attend_pairs(q_ref, o_ref, slice(hb * BLOCK, (hb + 1) * BLOCK), sink_ref, kcat, vcat, mask)

    @pl.when(j >= nstep)
    def _():
        for hb in range(CTX_LEN // BLOCK):
            _attend_pairs(q_ref, o_ref, slice(hb * BLOCK, (hb + 1) * BLOCK), sink_ref, kc, vc, None)


def _attn_a(q, k, v, sink):
    assert CTX_LEN == QB_A * BLOCK
    rows = QB_A * BLOCK
    nstep = SEQ // rows
    nk = HKV_A * HD_A
    qrow = lambda b, j, s: (jnp.where(j < nstep, b * nstep + j, T_LAT // rows + b), 0)
    lat_kv = pl.BlockSpec((SEQ, nk), lambda b, j, s: (b, 0))
    ctx_kv = pl.BlockSpec((CTX_LEN, nk), lambda b, j, s: (T_LAT // CTX_LEN + b, 0))
    return pl.pallas_call(
        _attn_a_kernel,
        grid_spec=pltpu.PrefetchScalarGridSpec(
            num_scalar_prefetch=1,
            grid=(BATCH, nstep + 1),
            in_specs=[pl.BlockSpec((rows, HQ_A * HD_A), qrow), lat_kv, lat_kv, ctx_kv, ctx_kv],
            out_specs=pl.BlockSpec((rows, HQ_A * HD_A), qrow),
        ),
        out_shape=jax.ShapeDtypeStruct((T_ALL, HQ_A * HD_A), bf16),
        compiler_params=pltpu.CompilerParams(
            dimension_semantics=("arbitrary", "arbitrary"), vmem_limit_bytes=VMEM_LIMIT),
        name="attn_a",
    )(sink, q, k, v, k, v)


def _attn_b_kernel(q_ref, k_ref, v_ref, kc_ref, vc_ref, o_ref):
    rep = HQ_B // HKV_B
    for h in range(HQ_B):
        g = h // rep
        hs = slice(h * HD_B, (h + 1) * HD_B)
        gs = slice(g * HD_B, (g + 1) * HD_B)
        vs = slice(2 * g * HD_B, 2 * (g + 1) * HD_B)
        q = q_ref[:, hs]
        s1 = lax.dot_general(q, k_ref[:, gs], NT_DIMS, preferred_element_type=f32)
        s2 = lax.dot_general(q, kc_ref[:, gs], NT_DIMS, preferred_element_type=f32)
        m = jnp.maximum(jnp.max(s1, axis=-1, keepdims=True), jnp.max(s2, axis=-1, keepdims=True))
        e1 = jnp.exp2(s1 - m).astype(bf16)
        e2 = jnp.exp2(s2 - m).astype(bf16)
        oe = (jnp.dot(e1, v_ref[:, vs], preferred_element_type=f32)
              + jnp.dot(e2, vc_ref[:, vs], preferred_element_type=f32))
        o_ref[:, hs] = (oe[:, :HD_B] * (1.0 / oe[:, HD_B:HD_B + 1])).astype(bf16)


def _attn_b(q, k, v):
    nblk = SEQ // BQ_B
    nk = HKV_B * HD_B
    qrow = lambda b, j: (b * nblk + j, 0)
    lat = lambda n: pl.BlockSpec((SEQ, n), lambda b, j: (b, 0))
    ctx = lambda n: pl.BlockSpec((CTX_LEN, n), lambda b, j: (T_LAT // CTX_LEN + b, 0))
    return pl.pallas_call(
        _attn_b_kernel,
        grid=(BATCH, nblk),
        in_specs=[pl.BlockSpec((BQ_B, HQ_B * HD_B), qrow), lat(nk), lat(2 * nk), ctx(nk), ctx(2 * nk)],
        out_specs=pl.BlockSpec((BQ_B, HQ_B * HD_B), qrow),
        out_shape=jax.ShapeDtypeStruct((T_LAT, HQ_B * HD_B), bf16),
        compiler_params=pltpu.CompilerParams(
            dimension_semantics=("arbitrary", "arbitrary"), vmem_limit_bytes=VMEM_LIMIT),
        name="attn_b",
    )(q, k, v, k, v)


def _post_attn_kernel(o_ref, wo_ref, x_ref, xc_ref, mod_ref, g1_ref, g2_ref, wrt_ref, brt_ref,
                      xn_ref, f_ref, tit_ref, ti_ref, tw_ref, cnt_ref):
    x_in = _stream_tile(x_ref, xc_ref)
    halves = []
    for h in range(TMP // TM):
        r = slice(h * TM, (h + 1) * TM)
        a = jnp.dot(o_ref[r, :], wo_ref[...], preferred_element_type=f32)
        x = x_in[r] + mod_ref[2:3, :] * _rms(a, g1_ref[...])
        xn_ref[r, :] = x
        fh = _rms(x, g2_ref[...]) * (1.0 + mod_ref[4:5, :]) + mod_ref[3:4, :]
        f_ref[r, :] = fh.astype(bf16)
        halves.append(fh)
    f = jnp.concatenate(halves, axis=0)
    logits = lax.dot_general(wrt_ref[...], f, NT_DIMS, precision=lax.Precision.HIGHEST,
                             preferred_element_type=f32) + brt_ref[...]
    sub = lax.broadcasted_iota(jnp.int32, logits.shape, 0)
    rest = logits
    top_v, top_i = [], []
    for _ in range(TOP_K):
        m = jnp.max(rest, axis=0, keepdims=True)
        idx = jnp.min(jnp.where(rest == m, sub, N_EXPERTS), axis=0, keepdims=True)
        top_v.append(m)
        top_i.append(idx)
        rest = jnp.where(sub == idx, -jnp.inf, rest)
    es = [jnp.exp(v - top_v[0]) for v in top_v]
    inv = 1.0 / (es[0] + es[1] + es[2] + es[3])
    k8 = lax.broadcasted_iota(jnp.int32, (SLOT_ROWS, TMP), 0)
    tit = jnp.full((SLOT_ROWS, TMP), -1.0, f32)
    twt = jnp.zeros((SLOT_ROWS, TMP), f32)
    for k in range(TOP_K):
        tit = jnp.where(k8 == k, top_i[k].astype(f32), tit)
        twt = jnp.where(k8 == k, es[k] * inv, twt)
    tit_ref[...] = tit
    eye = (lax.broadcasted_iota(jnp.int32, (TM, TM), 0)
           == lax.broadcasted_iota(jnp.int32, (TM, TM), 1)).astype(f32)
    to_rows = lambda t: lax.dot_general(eye, t, NT_DIMS, precision=lax.Precision.HIGHEST,
                                        preferred_element_type=f32)
    lane = lax.broadcasted_iota(jnp.int32, (TM, N_EXPERTS), 1).astype(f32)
    for h in range(TMP // TM):
        cols = slice(h * TM, (h + 1) * TM)
        ti = to_rows(tit[:, cols])
        ti_ref[cols, :] = ti
        tw_ref[cols, :] = to_rows(twt[:, cols])
        cnt = jnp.zeros((TM, N_EXPERTS), f32)
        for k in range(TOP_K):
            cnt = cnt + (ti[:, k:k + 1] == lane).astype(f32)
        cnt_ref[h] = jnp.sum(cnt, axis=0, keepdims=True)


def _post_attn(o, wo_bf16, x_lat, x_ctx, mods, layer, g1, g2, wr, br, n_tiles):
    ctx_base = 0 if x_ctx is x_lat else T_LAT // TMP
    seg = functools.partial(_seg_of_tile, tiles_per_seq=SEQ // TMP, lat_tiles=T_LAT // TMP)
    rows = n_tiles * TM
    row = lambda n: pl.BlockSpec((TMP, n), lambda i: (i, 0))
    const = lambda a, b: pl.BlockSpec((a, b), lambda i: (0, 0))
    return pl.pallas_call(
        _post_attn_kernel,
        grid=(rows // TMP,),
        in_specs=[
            row(D_MODEL), const(D_MODEL, D_MODEL), *_stream_specs(TMP, ctx_base),
            pl.BlockSpec((None, None, 6, D_MODEL), lambda i: (layer, seg(i), 0, 0)),
            const(1, D_MODEL), const(1, D_MODEL), const(N_EXPERTS, D_MODEL), const(N_EXPERTS, 1),
        ],
        out_specs=[row(D_MODEL), row(D_MODEL),
                   pl.BlockSpec((SLOT_ROWS, TMP), lambda i: (0, i)),
                   row(SLOT_ROWS), row(SLOT_ROWS),
                   pl.BlockSpec((TMP // TM, 1, N_EXPERTS), lambda i: (i, 0, 0))],
        out_shape=[jax.ShapeDtypeStruct((rows, D_MODEL), f32),
                   jax.ShapeDtypeStruct((rows, D_MODEL), bf16),
                   jax.ShapeDtypeStruct((SLOT_ROWS, rows), f32),
                   jax.ShapeDtypeStruct((rows, SLOT_ROWS), f32),
                   jax.ShapeDtypeStruct((rows, SLOT_ROWS), f32),
                   jax.ShapeDtypeStruct((n_tiles, 1, N_EXPERTS), f32)],
        compiler_params=pltpu.CompilerParams(
            dimension_semantics=("arbitrary",), vmem_limit_bytes=VMEM_LIMIT),
        name=f"post_attn{layer}",
    )(o, wo_bf16, x_lat, x_ctx, mods, g1, g2, wr.T, br.reshape(N_EXPERTS, 1))


def _ffn_tiles_max(n_tiles):
    rows = n_tiles * (TM * TOP_K + N_EXPERTS * (RUN_ALIGN - 1)) + N_EXPERTS * (TME - RUN_ALIGN)
    return -(-rows // TME)


def _route_tables(cnt, n_tiles_max):
    n = cnt.reshape(-1, N_EXPERTS).astype(jnp.int32)
    run = (n + RUN_ALIGN - 1) // RUN_ALIGN * RUN_ALIGN
    loff = jnp.cumsum(run, axis=1) - run
    tot = jnp.sum(run, axis=0)
    gsz = (tot + TME - 1) // TME * TME
    ends = jnp.cumsum(gsz)
    goff = (ends - gsz)[None, :] + jnp.cumsum(run, axis=0) - run
    c0 = jnp.arange(N_CHUNKS, dtype=jnp.int32) * RUN_ALIGN
    owner = jnp.sum(c0[None, :, None] >= (loff + run)[:, None, :], axis=2)
    shift = jnp.where(owner[:, :, None] == jnp.arange(N_EXPERTS)[None, None, :],
                      (goff - loff)[:, None, :], 0)
    tab = jnp.concatenate([jnp.sum(shift, axis=2) + c0[None, :],
                           jnp.sum(run, axis=1, keepdims=True) // RUN_ALIGN], axis=1)[:, None, :]
    n_active = ends[-1] // TME
    tail = jnp.concatenate([ends - gsz + tot, (gsz - tot) // RUN_ALIGN, n_active[None]])[None, :]
    tile = jnp.arange(n_tiles_max, dtype=jnp.int32)
    te = jnp.sum((jnp.minimum(tile, n_active - 1) * TME)[:, None] >= ends[None, :], axis=1)
    of_tile = te[:, None] == jnp.arange(N_EXPERTS)[None, :]
    used = jnp.sum(jnp.where(of_tile, (ends - gsz + tot)[None, :], 0), axis=1) - tile * TME
    half = (used <= TME // 2) & (tile < n_active)
    ids = jnp.arange(N_EXPERTS, dtype=jnp.int32)
    nonempty = gsz > 0
    later = jnp.where((ids[None, :] > ids[:, None]) & nonempty[None, :], ids[None, :], N_EXPERTS)
    nxt_e = jnp.min(later, axis=1)
    nxt_e = jnp.where(nxt_e == N_EXPERTS, -1, nxt_e)
    grp = jnp.sum(jnp.where(of_tile, (jnp.cumsum(nonempty) - 1)[None, :], 0), axis=1)
    nxt = jnp.sum(jnp.where(of_tile, nxt_e[None, :], 0), axis=1)
    loff_f = loff.astype(f32)
    i32 = lambda a: a.astype(jnp.int32)
    return (tab, tail, loff_f[:, None, :], loff_f[:, :, None], i32(te), i32(half), i32(grp), i32(nxt),
            i32(n_active).reshape(1))


def _local_chunk(c):
    if isinstance(c, int):
        return pl.ds(c * RUN_ALIGN, RUN_ALIGN)
    return pl.ds(pl.multiple_of(c * RUN_ALIGN, RUN_ALIGN), RUN_ALIGN)


def _global_chunk(ctab_ref, h, c):
    return pl.ds(pl.multiple_of(ctab_ref[h, 0, c], RUN_ALIGN), RUN_ALIGN)


def _start_chunks(ctab_ref, h, copy):
    for c in range(MIN_CHUNKS):
        copy(c).start()
    lax.fori_loop(MIN_CHUNKS, ctab_ref[h, 0, N_CHUNKS], lambda c, carry: (copy(c).start(), carry)[1], 0)


def _wait_chunks(ctab_ref, h, copy, bulk):
    bulk.wait()
    lax.fori_loop(MIN_CHUNKS, ctab_ref[h, 0, N_CHUNKS], lambda c, carry: (copy(c).wait(), carry)[1], 0)


def _chunk(base, c):
    return pl.ds(pl.multiple_of(base + c * RUN_ALIGN, RUN_ALIGN), RUN_ALIGN)


def _dispatch_kernel(ctab_ref, ctab_prev_ref, tail_ref, f_ref, tit_ref, loffc_ref, xs_hbm,
                     lbuf, zbuf, sem, zsem):
    i = pl.program_id(0)
    slot = i % 2
    sub = lax.broadcasted_iota(jnp.int32, (N_EXPERTS, TM), 0).astype(f32)
    before = (lax.broadcasted_iota(jnp.int32, (TM, TM), 0)
              < lax.broadcasted_iota(jnp.int32, (TM, TM), 1)).astype(bf16)
    rows = lax.broadcasted_iota(jnp.int32, (LOCAL_ROWS, TM), 0).astype(f32)

    def chunk_copy(ctab, h, b):
        return lambda c: pltpu.make_async_copy(
            lbuf.at[b, _local_chunk(c)], xs_hbm.at[_global_chunk(ctab, h, c)], sem.at[b])

    def bulk_copy(b):
        n = MIN_CHUNKS * RUN_ALIGN
        return pltpu.make_async_copy(lbuf.at[b, 0:n], xs_hbm.at[0:n], sem.at[b])

    def wait_tiles(ctab, s):
        for h in range(SUB):
            b = s * SUB + h
            _wait_chunks(ctab, h, chunk_copy(ctab, h, b), bulk_copy(b))

    for h in range(SUB):
        tit = tit_ref[:, h * TM:(h + 1) * TM]
        hit = [tit[k:k + 1, :] == sub for k in range(TOP_K)]
        cnt = sum(x.astype(f32) for x in hit)
        base = loffc_ref[h] + jnp.dot(cnt.astype(bf16), before, preferred_element_type=f32)
        onehot = jnp.zeros((LOCAL_ROWS, TM), f32)
        for k in range(TOP_K):
            lp = jnp.sum(jnp.where(hit[k], base, 0.0), axis=0, keepdims=True)
            onehot = onehot + (rows == lp).astype(f32)
        b = slot * SUB + h
        lbuf[b] = jnp.dot(onehot.astype(bf16), f_ref[h * TM:(h + 1) * TM, :], preferred_element_type=f32)
        _start_chunks(ctab_ref, h, chunk_copy(ctab_ref, h, b))

    @pl.when(i > 0)
    def _():
        wait_tiles(ctab_prev_ref, 1 - slot)

    tail_copy = lambda e, c: pltpu.make_async_copy(zbuf, xs_hbm.at[_chunk(tail_ref[0, e], c)], zsem)

    def for_each_tail(fn):
        for e in range(N_EXPERTS):
            def body(c, carry, e=e):
                fn(e, c)
                return carry
            lax.fori_loop(0, tail_ref[0, N_EXPERTS + e], body, 0)

    @pl.when(i == pl.num_programs(0) - 1)
    def _():
        zbuf[...] = jnp.zeros_like(zbuf)
        for_each_tail(lambda e, c: tail_copy(e, c).start())
        for_each_tail(lambda e, c: tail_copy(e, c).wait())
        wait_tiles(ctab_ref, slot)
        lbuf[0, 0:TME, :] = jnp.zeros((TME, D_MODEL), f32)
        n_active = tail_ref[0, 2 * N_EXPERTS]
        n_spare = xs_hbm.shape[0] // TME - n_active
        spare_copy = lambda j: pltpu.make_async_copy(
            lbuf.at[0, 0:TME], xs_hbm.at[pl.ds(pl.multiple_of((n_active + j) * TME, TME), TME)], zsem)
        lax.fori_loop(0, n_spare, lambda j, carry: (spare_copy(j).start(), carry)[1], 0)
        lax.fori_loop(0, n_spare, lambda j, carry: (spare_copy(j).wait(), carry)[1], 0)


def _dispatch(f, tit, tab, tail, loffc, layer, n_tiles, n_tiles_max):
    smem = functools.partial(pl.BlockSpec, memory_space=pltpu.SMEM)
    return pl.pallas_call(
        _dispatch_kernel,
        grid=(n_tiles // SUB,),
        in_specs=[
            smem((SUB, 1, N_CHUNKS + 1), lambda i: (i, 0, 0)),
            smem((SUB, 1, N_CHUNKS + 1), lambda i: (jnp.maximum(i - 1, 0), 0, 0)),
            smem((1, 2 * N_EXPERTS + 1), lambda i: (0, 0)),
            pl.BlockSpec((SUB * TM, D_MODEL), lambda i: (i, 0)),
            pl.BlockSpec((SLOT_ROWS, SUB * TM), lambda i: (0, i)),
            pl.BlockSpec((SUB, N_EXPERTS, 1), lambda i: (i, 0, 0)),
        ],
        out_specs=pl.BlockSpec(memory_space=pl.ANY),
        out_shape=jax.ShapeDtypeStruct((n_tiles_max * TME, D_MODEL), f32),
        scratch_shapes=[pltpu.VMEM((2 * SUB, LOCAL_ROWS, D_MODEL), f32),
                        pltpu.VMEM((RUN_ALIGN, D_MODEL), f32),
                        pltpu.SemaphoreType.DMA((2 * SUB,)), pltpu.SemaphoreType.DMA(())],
        compiler_params=pltpu.CompilerParams(
            dimension_semantics=("arbitrary",), vmem_limit_bytes=VMEM_LIMIT),
        name=f"dispatch{layer}",
    )(tab, tab, tail, f, tit, loffc)


def _moe_ffn_kernel(te_ref, half_ref, grp_ref, nxt_ref, na_ref, x_ref, w1_hbm, b1_ref, w2_hbm, b2_ref,
                    y_ref, w1f, w2f, w1b, w2b, sem, *, layer):
    i = pl.program_id(0)
    na = na_ref[0]

    def weight_copies(expert, s):
        return (pltpu.make_async_copy(w1_hbm.at[layer, expert], w1f.at[s], sem.at[0, s]),
                pltpu.make_async_copy(w2_hbm.at[layer, expert], w2f.at[s], sem.at[1, s]))

    def ffn(rows):
        u = jnp.dot(x_ref[rows, :].astype(bf16), w1b[...], preferred_element_type=f32) + b1_ref[...]
        glu = jnp.minimum(u[:, :D_FF], SWIGLU_LIMIT)
        lin = jnp.clip(u[:, D_FF:], -SWIGLU_LIMIT, SWIGLU_LIMIT)
        act = glu * jax.nn.sigmoid(SWIGLU_ALPHA * glu) * (lin + 1.0)
        y_ref[rows, :] = jnp.dot(act.astype(bf16), w2b[...], preferred_element_type=f32) + b2_ref[...]

    @pl.when(i == 0)
    def _():
        for cp in weight_copies(te_ref[0], 0):
            cp.start()

    @pl.when(i < na)
    def _():
        @pl.when((i == 0) | (te_ref[i] != te_ref[jnp.maximum(i - 1, 0)]))
        def _():
            s = grp_ref[i] % 2
            for cp in weight_copies(te_ref[i], s):
                cp.wait()
            w1b[...] = w1f[s].astype(bf16)
            w2b[...] = w2f[s].astype(bf16)

            @pl.when(nxt_ref[i] >= 0)
            def _():
                for cp in weight_copies(nxt_ref[i], 1 - s):
                    cp.start()

        @pl.when(half_ref[i] == 0)
        def _():
            ffn(slice(0, TME // 2))
            ffn(slice(TME // 2, TME))

        @pl.when(half_ref[i] != 0)
        def _():
            ffn(slice(0, TME // 2))
            y_ref[TME // 2:, :] = jnp.zeros((TME // 2, D_MODEL), f32)


def _moe_ffn(xs_sorted, te, half, grp, nxt, na, w1, b1, w2, b2, layer, n_tiles_max):
    ex = lambda i, te, *_: (layer, te[i], 0, 0)
    tile = pl.BlockSpec((TME, D_MODEL),
                        lambda i, te, half, grp, nxt, na: (jnp.maximum(jnp.minimum(i, na[0] - 1), 0), 0))
    n_prefetch = 5
    return pl.pallas_call(
        functools.partial(_moe_ffn_kernel, layer=layer),
        grid_spec=pltpu.PrefetchScalarGridSpec(
            num_scalar_prefetch=n_prefetch,
            grid=(n_tiles_max,),
            in_specs=[
                tile,
                pl.BlockSpec(memory_space=pl.ANY),
                pl.BlockSpec((None, None, 1, 2 * D_FF), ex),
                pl.BlockSpec(memory_space=pl.ANY),
                pl.BlockSpec((None, None, 1, D_MODEL), ex),
            ],
            out_specs=tile,
            scratch_shapes=[pltpu.VMEM((2, D_MODEL, 2 * D_FF), f32), pltpu.VMEM((2, D_FF, D_MODEL), f32),
                            pltpu.VMEM((D_MODEL, 2 * D_FF), bf16), pltpu.VMEM((D_FF, D_MODEL), bf16),
                            pltpu.SemaphoreType.DMA((2, 2))],
        ),
        out_shape=jax.ShapeDtypeStruct((n_tiles_max * TME, D_MODEL), f32),
        input_output_aliases={n_prefetch: 0},
        compiler_params=pltpu.CompilerParams(
            dimension_semantics=("arbitrary",), vmem_limit_bytes=VMEM_LIMIT),
        name=f"moe_ffn{layer}",
    )(te, half, grp, nxt, na, xs_sorted, w1, b1.reshape(DEPTH, N_EXPERTS, 1, 2 * D_FF), w2,
      b2.reshape(DEPTH, N_EXPERTS, 1, D_MODEL))


def _combine_kernel(ctab_ref, ctab_next_ref, y_hbm, ti_ref, tw_ref, loffr_ref, x_ref, mod_ref, g_ref,
                    xn_ref, ybuf, sem):
    i = pl.program_id(0)
    slot = i % 2

    def chunk_copy(ctab, h, b):
        return lambda c: pltpu.make_async_copy(
            y_hbm.at[_global_chunk(ctab, h, c)], ybuf.at[b, _local_chunk(c)], sem.at[b])

    def bulk_copy(b):
        n = MIN_CHUNKS * RUN_ALIGN
        return pltpu.make_async_copy(y_hbm.at[0:n], ybuf.at[b, 0:n], sem.at[b])

    def start_tiles(ctab, s):
        for h in range(SUB):
            _start_chunks(ctab, h, chunk_copy(ctab, h, s * SUB + h))

    @pl.when(i == 0)
    def _():
        ybuf[...] = jnp.zeros_like(ybuf)
        start_tiles(ctab_ref, 0)

    @pl.when(i + 1 < pl.num_programs(0))
    def _():
        start_tiles(ctab_next_ref, 1 - slot)

    lane = lax.broadcasted_iota(jnp.int32, (TM, N_EXPERTS), 1).astype(f32)
    before = (lax.broadcasted_iota(jnp.int32, (TM, TM), 1)
              < lax.broadcasted_iota(jnp.int32, (TM, TM), 0)).astype(bf16)
    cols = lax.broadcasted_iota(jnp.int32, (TM, LOCAL_ROWS), 1).astype(f32)
    for h in range(SUB):
        tok = slice(h * TM, (h + 1) * TM)
        b = slot * SUB + h
        ti = ti_ref[tok, :]
        tw = tw_ref[tok, :]
        hit = [ti[:, k:k + 1] == lane for k in range(TOP_K)]
        cnt = sum(x.astype(f32) for x in hit)
        base = loffr_ref[h] + jnp.dot(before, cnt.astype(bf16), preferred_element_type=f32)
        w = jnp.zeros((TM, LOCAL_ROWS), f32)
        for k in range(TOP_K):
            lp = jnp.sum(jnp.where(hit[k], base, 0.0), axis=1, keepdims=True)
            w = jnp.where(cols == lp, tw[:, k:k + 1], w)
        _wait_chunks(ctab_ref, h, chunk_copy(ctab_ref, h, b), bulk_copy(b))
        acc = jnp.dot(w.astype(bf16), ybuf[b].astype(bf16), preferred_element_type=f32)
        xn_ref[tok, :] = x_ref[tok, :] + mod_ref[5:6, :] * _rms(acc, g_ref[...])


def _combine(y, tab, ti, tw, loffr, xs, mods, layer, g, n_tiles):
    rows = SUB * TM
    n_steps = n_tiles // SUB
    seg = functools.partial(_seg_of_tile, tiles_per_seq=SEQ // rows, lat_tiles=T_LAT // rows)
    row = lambda n: pl.BlockSpec((rows, n), lambda i: (i, 0))
    smem = functools.partial(pl.BlockSpec, memory_space=pltpu.SMEM)
    return pl.pallas_call(
        _combine_kernel,
        grid=(n_steps,),
        in_specs=[
            smem((SUB, 1, N_CHUNKS + 1), lambda i: (i, 0, 0)),
            smem((SUB, 1, N_CHUNKS + 1), lambda i: (jnp.minimum(i + 1, n_steps - 1), 0, 0)),
            pl.BlockSpec(memory_space=pl.ANY),
            row(SLOT_ROWS), row(SLOT_ROWS),
            pl.BlockSpec((SUB, 1, N_EXPERTS), lambda i: (i, 0, 0)),
            row(D_MODEL),
            pl.BlockSpec((None, None, 6, D_MODEL), lambda i: (layer, seg(i), 0, 0)),
            pl.BlockSpec((1, D_MODEL), lambda i: (0, 0)),
        ],
        out_specs=row(D_MODEL),
        out_shape=jax.ShapeDtypeStruct((n_tiles * TM, D_MODEL), f32),
        scratch_shapes=[pltpu.VMEM((2 * SUB, LOCAL_ROWS, D_MODEL), f32),
                        pltpu.SemaphoreType.DMA((2 * SUB,))],
        compiler_params=pltpu.CompilerParams(
            dimension_semantics=("arbitrary",), vmem_limit_bytes=VMEM_LIMIT),
        name=f"combine{layer}",
    )(tab, tab, y, ti, tw, loffr, xs, mods, g)


def _moe(f, tit, ti, tw, cnt, xs, mods, layer, g, w1, b1, w2, b2, n_tiles):
    n_tiles_max = _ffn_tiles_max(n_tiles)
    tab, tail, loffr, loffc, te, half, grp, nxt, na = _route_tables(cnt, n_tiles_max)
    xs_sorted = _dispatch(f, tit, tab, tail, loffc, layer, n_tiles, n_tiles_max)
    y = _moe_ffn(xs_sorted, te, half, grp, nxt, na, w1, b1, w2, b2, layer, n_tiles_max)
    return _combine(y, tab, ti, tw, loffr, xs, mods, layer, g, n_tiles)


def kernel(x, c, ctx, c_ctx, w_ada, b_ada, norm_g, a_w_qkv, a_w_o, a_sink, b_w_qkv, b_q_norm, b_k_norm,
           b_w_o, moe_w_router, moe_b_router, moe_w1, moe_b1, moe_w2, moe_b2):
    assert DEPTH == 2 and x.shape == (BATCH, SEQ, D_MODEL) and ctx.shape == (BATCH, CTX_LEN, D_MODEL)
    x_lat, x_ctx = x.reshape(T_LAT, D_MODEL), ctx.reshape(T_CTX, D_MODEL)
    c_all = jnp.concatenate(
        [c, c_ctx[None, :], jnp.zeros((MOD_ROWS - BATCH - 1, D_MODEL), f32)], axis=0)
    mods = _adaln(c_all, w_ada, b_ada).reshape(DEPTH, MOD_ROWS, 6, D_MODEL)
    ones = jnp.ones((1, LANES), f32)
    g = lambda i, j: norm_g[i, j][None, :]

    q, k, v = _qkv(x_lat, x_ctx, mods, 0, g(0, 0), a_w_qkv[0].astype(bf16), _rope_tables(HD_A), ones, ones,
                   hq=HQ_A, hkv=HKV_A, hd=HD_A, qk_norm=False, v_ones=False, tm=TMP, chain=TM)
    o = _attn_a(q, k, v, a_sink[0])
    xs, *routed = _post_attn(o, a_w_o[0].astype(bf16), x_lat, x_ctx, mods, 0, g(0, 1), g(0, 2),
                             moe_w_router[0], moe_b_router[0], ALL_TILES)
    xs = _moe(*routed, xs, mods, 0, g(0, 3), moe_w1, moe_b1, moe_w2, moe_b2, ALL_TILES)

    q, k, v = _qkv(xs, xs, mods, 1, g(1, 0), b_w_qkv[0].astype(bf16), _rope_tables(HD_B),
                   b_q_norm[0][None, :], b_k_norm[0][None, :],
                   hq=HQ_B, hkv=HKV_B, hd=HD_B, qk_norm=True, v_ones=True, tm=TM, chain=TM)
    o = _attn_b(q, k, v)
    xl, *routed = _post_attn(o, b_w_o[0].astype(bf16), xs, xs, mods, 1, g(1, 1), g(1, 2),
                             moe_w_router[1], moe_b_router[1], LAT_TILES)
    xl = _moe(*routed, xl, mods, 1, g(1, 3), moe_w1, moe_b1, moe_w2, moe_b2, LAT_TILES)
    return xl.reshape(BATCH, SEQ, D_MODEL)
```

```python
import functools

import jax
import jax.numpy as jnp
from jax import lax
from jax.experimental import pallas as pl
from jax.experimental.pallas import tpu as pltpu

D_MODEL = 1024
BATCH = 8
SEQ = 2048
DEPTH = 2
GRID_W = 64
CTX_LEN = 256
BLOCK = 128
WINDOW = 128
ROPE_BASE = 10000.0
EPS = 1e-6
HQ_A, HKV_A, HD_A = 16, 2, 64
HQ_B, HKV_B, HD_B = 8, 2, 128
N_EXPERTS = 32
TOP_K = 4
D_FF = D_MODEL
SWIGLU_LIMIT = 7.0
SWIGLU_ALPHA = 1.702

T_LAT = BATCH * SEQ
T_CTX = BATCH * CTX_LEN
T_ALL = T_LAT + T_CTX
LANES = 128
TM = 256
LAT_TILES = T_LAT // TM
ALL_TILES = T_ALL // TM
MOD_ROWS = 16
CTX_MOD_ROW = BATCH
TMP = 1024
PAIR_STACK = 2
QB_A = 2
SUB = 2
BQ_B = 1024
TME = 512
SLOT_ROWS = 8
RUN_ALIGN = 8
LOCAL_ROWS = -(-(TM * TOP_K + N_EXPERTS * (RUN_ALIGN - 1)) // LANES) * LANES
N_CHUNKS = LOCAL_ROWS // RUN_ALIGN
MIN_CHUNKS = TM * TOP_K // RUN_ALIGN
NEG = -1e30
LOG2E = 1.4426950408889634
NT_DIMS = (((1,), (1,)), ((), ()))
VMEM_LIMIT = 56 * 1024 * 1024

f32 = jnp.float32
bf16 = jnp.bfloat16


def _seg_of_tile(i, tiles_per_seq, lat_tiles):
    return jnp.where(i < lat_tiles, i // tiles_per_seq, CTX_MOD_ROW)


def _adaln_kernel(c_ref, w_ref, b_ref, o_ref):
    c = c_ref[...]
    s = c * jax.nn.sigmoid(c)
    o_ref[...] = jnp.dot(s, w_ref[...], precision=lax.Precision.HIGHEST,
                         preferred_element_type=f32) + b_ref[...]


def _adaln(c_all, w_ada, b_ada):
    tn = 3072
    return pl.pallas_call(
        _adaln_kernel,
        grid=(DEPTH, 6 * D_MODEL // tn),
        in_specs=[
            pl.BlockSpec((MOD_ROWS, D_MODEL), lambda l, j: (0, 0)),
            pl.BlockSpec((None, D_MODEL, tn), lambda l, j: (l, 0, j)),
            pl.BlockSpec((None, 1, tn), lambda l, j: (l, 0, j)),
        ],
        out_specs=pl.BlockSpec((None, MOD_ROWS, tn), lambda l, j: (l, 0, j)),
        out_shape=jax.ShapeDtypeStruct((DEPTH, MOD_ROWS, 6 * D_MODEL), f32),
        compiler_params=pltpu.CompilerParams(
            dimension_semantics=("arbitrary", "arbitrary"), vmem_limit_bytes=VMEM_LIMIT),
        name="adaln",
    )(c_all, w_ada, b_ada.reshape(DEPTH, 1, 6 * D_MODEL))


def _rms(x, g):
    return x * lax.rsqrt(jnp.mean(x * x, axis=-1, keepdims=True) + EPS) * g


def _stream_specs(tm, ctx_base):
    lat_tiles = T_LAT // tm
    return [pl.BlockSpec((tm, D_MODEL), lambda i, *_: (jnp.minimum(i, lat_tiles - 1), 0)),
            pl.BlockSpec((tm, D_MODEL), lambda i, *_: (jnp.maximum(i, lat_tiles) - ctx_base, 0))]


def _stream_tile(x_ref, xc_ref):
    lat_tiles = T_LAT // x_ref.shape[0]
    return jnp.where(pl.program_id(0) < lat_tiles, x_ref[...], xc_ref[...])


def _qkv_kernel(x_ref, xc_ref, mod_ref, g_ref, w_ref, cos_ref, sa_ref, sb_ref, qn_ref, kn_ref,
                q_ref, k_ref, v_ref, *, nq, nk, hd, qk_norm, v_ones, chain):
    x_in = _stream_tile(x_ref, xc_ref)
    quarter = hd // 4
    scale = hd ** -0.5 * LOG2E
    for r0 in range(0, x_ref.shape[0], chain):
        r = slice(r0, r0 + chain)
        h = _rms(x_in[r], g_ref[...]) * (1.0 + mod_ref[1:2, :]) + mod_ref[0:1, :]
        qkv = jnp.dot(h.astype(bf16), w_ref[...], preferred_element_type=f32)
        cos, sa, sb = cos_ref[r, :], sa_ref[r, :], sb_ref[r, :]

        def rope(c):
            return c * cos + pltpu.roll(c, quarter, 1) * sa + pltpu.roll(c, LANES - quarter, 1) * sb

        for j in range(nq // LANES):
            c = qkv[:, j * LANES:(j + 1) * LANES]
            if qk_norm:
                c = _rms(c, qn_ref[...])
            q_ref[r, j * LANES:(j + 1) * LANES] = (rope(c) * scale).astype(bf16)
        for j in range(nk // LANES):
            c = qkv[:, nq + j * LANES:nq + (j + 1) * LANES]
            if qk_norm:
                c = _rms(c, kn_ref[...])
            k_ref[r, j * LANES:(j + 1) * LANES] = rope(c).astype(bf16)
        if v_ones:
            for j in range(nk // LANES):
                v_ref[r, 2 * j * LANES:(2 * j + 1) * LANES] = (
                    qkv[:, nq + nk + j * LANES:nq + nk + (j + 1) * LANES].astype(bf16))
                v_ref[r, (2 * j + 1) * LANES:(2 * j + 2) * LANES] = jnp.ones((chain, LANES), bf16)
        else:
            v_ref[r, :] = qkv[:, nq + nk:].astype(bf16)


def _qkv(x_lat, x_ctx, mods, layer, g, w_bf16, tabs, qn, kn, *, hq, hkv, hd, qk_norm, v_ones, tm, chain):
    ctx_base = 0 if x_ctx is x_lat else T_LAT // tm
    nq, nk = hq * hd, hkv * hd
    nv = 2 * nk if v_ones else nk
    per_seq, lat_tiles = SEQ // tm, T_LAT // tm
    seg = functools.partial(_seg_of_tile, tiles_per_seq=per_seq, lat_tiles=lat_tiles)
    tab_idx = lambda i: (jnp.where(i < lat_tiles, i % per_seq, per_seq), 0)
    tab_spec = pl.BlockSpec((tm, LANES), tab_idx)
    row = lambda n: pl.BlockSpec((tm, n), lambda i: (i, 0))
    return pl.pallas_call(
        functools.partial(_qkv_kernel, nq=nq, nk=nk, hd=hd, qk_norm=qk_norm, v_ones=v_ones, chain=chain),
        grid=(T_ALL // tm,),
        in_specs=[
            *_stream_specs(tm, ctx_base),
            pl.BlockSpec((None, None, 6, D_MODEL), lambda i: (layer, seg(i), 0, 0)),
            pl.BlockSpec((1, D_MODEL), lambda i: (0, 0)),
            pl.BlockSpec((D_MODEL, nq + 2 * nk), lambda i: (0, 0)),
            tab_spec, tab_spec, tab_spec,
            pl.BlockSpec((1, LANES), lambda i: (0, 0)),
            pl.BlockSpec((1, LANES), lambda i: (0, 0)),
        ],
        out_specs=[row(nq), row(nk), row(nv)],
        out_shape=[jax.ShapeDtypeStruct((T_ALL, nq), bf16),
                   jax.ShapeDtypeStruct((T_ALL, nk), bf16),
                   jax.ShapeDtypeStruct((T_ALL, nv), bf16)],
        compiler_params=pltpu.CompilerParams(
            dimension_semantics=("arbitrary",), vmem_limit_bytes=VMEM_LIMIT),
        name=f"qkv{layer}",
    )(x_lat, x_ctx, mods, g, w_bf16, *tabs, qn, kn)


def _rope_tables(hd):
    quarter = hd // 4
    inv_freq = jnp.float32(ROPE_BASE) ** (-jnp.arange(quarter, dtype=f32) / quarter)
    t = jnp.arange(SEQ)
    ang_r = (t // GRID_W).astype(f32)[:, None] * inv_freq[None, :]
    ang_c = (t % GRID_W).astype(f32)[:, None] * inv_freq[None, :]
    z = jnp.zeros_like(ang_r)
    cos = jnp.concatenate([jnp.cos(ang_r)] * 2 + [jnp.cos(ang_c)] * 2, axis=-1)
    sa = jnp.concatenate([z, jnp.sin(ang_r), z, jnp.sin(ang_c)], axis=-1)
    sb = jnp.concatenate([-jnp.sin(ang_r), z, -jnp.sin(ang_c), z], axis=-1)
    rep = LANES // hd
    pad = lambda a, v: jnp.concatenate(
        [jnp.tile(a, (1, rep)), jnp.full((TMP, LANES), v, f32)], axis=0)
    return pad(cos, 1.0), pad(sa, 0.0), pad(sb, 0.0)


def _pair_operand(x, g):
    lane = lax.broadcasted_iota(jnp.int32, x.shape, 1)
    swapped = pltpu.roll(x, HD_A, 1)
    lo_src, hi_src = (x, swapped) if g == 0 else (swapped, x)
    lo = jnp.where(lane < HD_A, lo_src, 0.0)
    hi = jnp.where(lane >= HD_A, hi_src, 0.0)
    return jnp.concatenate([lo, hi], axis=0)


def _attend_pairs(q_ref, o_ref, qrows, sink_ref, kcat, vcat, mask):
    n = kcat.shape[0]
    pairs = PAIR_STACK
    rows = pairs * BLOCK
    lane = lax.broadcasted_iota(jnp.int32, (rows, LANES), 1)
    pair_of_row = lax.broadcasted_iota(jnp.int32, (rows, 1), 0) // BLOCK
    ind_row = lax.broadcasted_iota(jnp.int32, (2 * n, LANES), 0)
    ind_lane = lax.broadcasted_iota(jnp.int32, (2 * n, LANES), 1)
    ind = jnp.where(ind_lane == ind_row // n, 1.0, 0.0)
    if mask is not None:
        mask = jnp.concatenate([mask] * pairs, axis=0)
    operands = [(_pair_operand(kcat, g).astype(bf16),
                 jnp.concatenate([_pair_operand(vcat, g), ind], axis=1).astype(bf16))
                for g in range(HKV_A)]
    pairs_per_group = HQ_A // HKV_A // 2
    for c0 in range(0, HQ_A // 2, pairs):
        kp, vp = operands[c0 // pairs_per_group]
        cols = [(c0 + p) * LANES for p in range(pairs)]
        qs = jnp.concatenate([q_ref[qrows, c:c + LANES] for c in cols], axis=0)
        s = lax.dot_general(qs, kp, NT_DIMS, preferred_element_type=f32)
        es, ms, sks = [], [], []
        for hh in range(2):
            sh = s[:, hh * n:(hh + 1) * n]
            if mask is not None:
                sh = jnp.where(mask, sh, NEG)
            sk = jnp.zeros((rows, 1), f32)
            for p in range(pairs):
                sk = jnp.where(pair_of_row == p, sink_ref[cols[p] // HD_A + hh] * LOG2E, sk)
            m = jnp.maximum(jnp.max(sh, axis=-1, keepdims=True), sk)
            es.append(jnp.exp2(sh - m).astype(bf16))
            ms.append(m)
            sks.append(sk)
        oe = jnp.dot(jnp.concatenate(es, axis=1), vp, preferred_element_type=f32)
        inv = [1.0 / (oe[:, LANES + hh:LANES + hh + 1] + jnp.exp2(sks[hh] - ms[hh])) for hh in range(2)]
        o = oe[:, :LANES] * jnp.where(lane < HD_A, inv[0], inv[1])
        for p in range(pairs):
            o_ref[qrows, cols[p]:cols[p] + LANES] = o[p * BLOCK:(p + 1) * BLOCK].astype(bf16)


def _attn_a_kernel(sink_ref, q_ref, k_ref, v_ref, kc_ref, vc_ref, o_ref):
    j = pl.program_id(1)
    nstep = SEQ // (QB_A * BLOCK)
    kc = kc_ref[...].astype(f32)
    vc = vc_ref[...].astype(f32)

    @pl.when(j < nstep)
    def _():
        wlen = 3 * BLOCK
        for hb in range(QB_A):
            blk = j * QB_A + hb
            s0 = pl.multiple_of(jnp.clip((blk - 1) * BLOCK, 0, SEQ - wlen), BLOCK)
            kcat = jnp.concatenate([k_ref[pl.ds(s0, wlen), :].astype(f32), kc], axis=0)
            vcat = jnp.concatenate([v_ref[pl.ds(s0, wlen), :].astype(f32), vc], axis=0)
            qpos = blk * BLOCK + lax.broadcasted_iota(jnp.int32, (BLOCK, wlen + CTX_LEN), 0)
            col = lax.broadcasted_iota(jnp.int32, (BLOCK, wlen + CTX_LEN), 1)
            mask = (jnp.abs(qpos - (s0 + col)) <= WINDOW) | (col >= wlen)
            _attend_pairs(q_ref, o_ref, slice(hb * BLOCK, (hb + 1) * BLOCK), sink_ref, kcat, vcat, mask)

    @pl.when(j >= nstep)
    def _():
        for hb in range(CTX_LEN // BLOCK):
            _attend_pairs(q_ref, o_ref, slice(hb * BLOCK, (hb + 1) * BLOCK), sink_ref, kc, vc, None)


def _attn_a(q, k, v, sink):
    assert CTX_LEN == QB_A * BLOCK
    rows = QB_A * BLOCK
    nstep = SEQ // rows
    nk = HKV_A * HD_A
    qrow = lambda b, j, s: (jnp.where(j < nstep, b * nstep + j, T_LAT // rows + b), 0)
    lat_kv = pl.BlockSpec((SEQ, nk), lambda b, j, s: (b, 0))
    ctx_kv = pl.BlockSpec((CTX_LEN, nk), lambda b, j, s: (T_LAT // CTX_LEN + b, 0))
    return pl.pallas_call(
        _attn_a_kernel,
        grid_spec=pltpu.PrefetchScalarGridSpec(
            num_scalar_prefetch=1,
            grid=(BATCH, nstep + 1),
            in_specs=[pl.BlockSpec((rows, HQ_A * HD_A), qrow), lat_kv, lat_kv, ctx_kv, ctx_kv],
            out_specs=pl.BlockSpec((rows, HQ_A * HD_A), qrow),
        ),
        out_shape=jax.ShapeDtypeStruct((T_ALL, HQ_A * HD_A), bf16),
        compiler_params=pltpu.CompilerParams(
            dimension_semantics=("arbitrary", "arbitrary"), vmem_limit_bytes=VMEM_LIMIT),
        name="attn_a",
    )(sink, q, k, v, k, v)


def _attn_b_kernel(q_ref, k_ref, v_ref, kc_ref, vc_ref, o_ref):
    rep = HQ_B // HKV_B
    for h in range(HQ_B):
        g = h // rep
        hs = slice(h * HD_B, (h + 1) * HD_B)
        gs = slice(g * HD_B, (g + 1) * HD_B)
        vs = slice(2 * g * HD_B, 2 * (g + 1) * HD_B)
        q = q_ref[:, hs]
        s1 = lax.dot_general(q, k_ref[:, gs], NT_DIMS, preferred_element_type=f32)
        s2 = lax.dot_general(q, kc_ref[:, gs], NT_DIMS, preferred_element_type=f32)
        m = jnp.maximum(jnp.max(s1, axis=-1, keepdims=True), jnp.max(s2, axis=-1, keepdims=True))
        e1 = jnp.exp2(s1 - m).astype(bf16)
        e2 = jnp.exp2(s2 - m).astype(bf16)
        oe = (jnp.dot(e1, v_ref[:, vs], preferred_element_type=f32)
              + jnp.dot(e2, vc_ref[:, vs], preferred_element_type=f32))
        o_ref[:, hs] = (oe[:, :HD_B] * (1.0 / oe[:, HD_B:HD_B + 1])).astype(bf16)


def _attn_b(q, k, v):
    nblk = SEQ // BQ_B
    nk = HKV_B * HD_B
    qrow = lambda b, j: (b * nblk + j, 0)
    lat = lambda n: pl.BlockSpec((SEQ, n), lambda b, j: (b, 0))
    ctx = lambda n: pl.BlockSpec((CTX_LEN, n), lambda b, j: (T_LAT // CTX_LEN + b, 0))
    return pl.pallas_call(
        _attn_b_kernel,
        grid=(BATCH, nblk),
        in_specs=[pl.BlockSpec((BQ_B, HQ_B * HD_B), qrow), lat(nk), lat(2 * nk), ctx(nk), ctx(2 * nk)],
        out_specs=pl.BlockSpec((BQ_B, HQ_B * HD_B), qrow),
        out_shape=jax.ShapeDtypeStruct((T_LAT, HQ_B * HD_B), bf16),
        compiler_params=pltpu.CompilerParams(
            dimension_semantics=("arbitrary", "arbitrary"), vmem_limit_bytes=VMEM_LIMIT),
        name="attn_b",
    )(q, k, v, k, v)


def _post_attn_kernel(o_ref, wo_ref, x_ref, xc_ref, mod_ref, g1_ref, g2_ref, wrt_ref, brt_ref,
                      xn_ref, f_ref, tit_ref, ti_ref, tw_ref, cnt_ref):
    x_in = _stream_tile(x_ref, xc_ref)
    halves = []
    for h in range(TMP // TM):
        r = slice(h * TM, (h + 1) * TM)
        a = jnp.dot(o_ref[r, :], wo_ref[...], preferred_element_type=f32)
        x = x_in[r] + mod_ref[2:3, :] * _rms(a, g1_ref[...])
        xn_ref[r, :] = x
        fh = _rms(x, g2_ref[...]) * (1.0 + mod_ref[4:5, :]) + mod_ref[3:4, :]
        f_ref[r, :] = fh.astype(bf16)
        halves.append(fh)
    f = jnp.concatenate(halves, axis=0)
    logits = lax.dot_general(wrt_ref[...], f, NT_DIMS, precision=lax.Precision.HIGHEST,
                             preferred_element_type=f32) + brt_ref[...]
    sub = lax.broadcasted_iota(jnp.int32, logits.shape, 0)
    rest = logits
    top_v, top_i = [], []
    for _ in range(TOP_K):
        m = jnp.max(rest, axis=0, keepdims=True)
        idx = jnp.min(jnp.where(rest == m, sub, N_EXPERTS), axis=0, keepdims=True)
        top_v.append(m)
        top_i.append(idx)
        rest = jnp.where(sub == idx, -jnp.inf, rest)
    es = [jnp.exp(v - top_v[0]) for v in top_v]
    inv = 1.0 / (es[0] + es[1] + es[2] + es[3])
    k8 = lax.broadcasted_iota(jnp.int32, (SLOT_ROWS, TMP), 0)
    tit = jnp.full((SLOT_ROWS, TMP), -1.0, f32)
    twt = jnp.zeros((SLOT_ROWS, TMP), f32)
    for k in range(TOP_K):
        tit = jnp.where(k8 == k, top_i[k].astype(f32), tit)
        twt = jnp.where(k8 == k, es[k] * inv, twt)
    tit_ref[...] = tit
    eye = (lax.broadcasted_iota(jnp.int32, (TM, TM), 0)
           == lax.broadcasted_iota(jnp.int32, (TM, TM), 1)).astype(f32)
    to_rows = lambda t: lax.dot_general(eye, t, NT_DIMS, precision=lax.Precision.HIGHEST,
                                        preferred_element_type=f32)
    lane = lax.broadcasted_iota(jnp.int32, (TM, N_EXPERTS), 1).astype(f32)
    for h in range(TMP // TM):
        cols = slice(h * TM, (h + 1) * TM)
        ti = to_rows(tit[:, cols])
        ti_ref[cols, :] = ti
        tw_ref[cols, :] = to_rows(twt[:, cols])
        cnt = jnp.zeros((TM, N_EXPERTS), f32)
        for k in range(TOP_K):
            cnt = cnt + (ti[:, k:k + 1] == lane).astype(f32)
        cnt_ref[h] = jnp.sum(cnt, axis=0, keepdims=True)


def _post_attn(o, wo_bf16, x_lat, x_ctx, mods, layer, g1, g2, wr, br, n_tiles):
    ctx_base = 0 if x_ctx is x_lat else T_LAT // TMP
    seg = functools.partial(_seg_of_tile, tiles_per_seq=SEQ // TMP, lat_tiles=T_LAT // TMP)
    rows = n_tiles * TM
    row = lambda n: pl.BlockSpec((TMP, n), lambda i: (i, 0))
    const = lambda a, b: pl.BlockSpec((a, b), lambda i: (0, 0))
    return pl.pallas_call(
        _post_attn_kernel,
        grid=(rows // TMP,),
        in_specs=[
            row(D_MODEL), const(D_MODEL, D_MODEL), *_stream_specs(TMP, ctx_base),
            pl.BlockSpec((None, None, 6, D_MODEL), lambda i: (layer, seg(i), 0, 0)),
            const(1, D_MODEL), const(1, D_MODEL), const(N_EXPERTS, D_MODEL), const(N_EXPERTS, 1),
        ],
        out_specs=[row(D_MODEL), row(D_MODEL),
                   pl.BlockSpec((SLOT_ROWS, TMP), lambda i: (0, i)),
                   row(SLOT_ROWS), row(SLOT_ROWS),
                   pl.BlockSpec((TMP // TM, 1, N_EXPERTS), lambda i: (i, 0, 0))],
        out_shape=[jax.ShapeDtypeStruct((rows, D_MODEL), f32),
                   jax.ShapeDtypeStruct((rows, D_MODEL), bf16),
                   jax.ShapeDtypeStruct((SLOT_ROWS, rows), f32),
                   jax.ShapeDtypeStruct((rows, SLOT_ROWS), f32),
                   jax.ShapeDtypeStruct((rows, SLOT_ROWS), f32),
                   jax.ShapeDtypeStruct((n_tiles, 1, N_EXPERTS), f32)],
        compiler_params=pltpu.CompilerParams(
            dimension_semantics=("arbitrary",), vmem_limit_bytes=VMEM_LIMIT),
        name=f"post_attn{layer}",
    )(o, wo_bf16, x_lat, x_ctx, mods, g1, g2, wr.T, br.reshape(N_EXPERTS, 1))


def _ffn_tiles_max(n_tiles):
    rows = n_tiles * (TM * TOP_K + N_EXPERTS * (RUN_ALIGN - 1)) + N_EXPERTS * (TME - RUN_ALIGN)
    return -(-rows // TME)


def _route_tables(cnt, n_tiles_max):
    n = cnt.reshape(-1, N_EXPERTS).astype(jnp.int32)
    run = (n + RUN_ALIGN - 1) // RUN_ALIGN * RUN_ALIGN
    loff = jnp.cumsum(run, axis=1) - run
    tot = jnp.sum(run, axis=0)
    gsz = (tot + TME - 1) // TME * TME
    ends = jnp.cumsum(gsz)
    goff = (ends - gsz)[None, :] + jnp.cumsum(run, axis=0) - run
    c0 = jnp.arange(N_CHUNKS, dtype=jnp.int32) * RUN_ALIGN
    owner = jnp.sum(c0[None, :, None] >= (loff + run)[:, None, :], axis=2)
    shift = jnp.where(owner[:, :, None] == jnp.arange(N_EXPERTS)[None, None, :],
                      (goff - loff)[:, None, :], 0)
    tab = jnp.concatenate([jnp.sum(shift, axis=2) + c0[None, :],
                           jnp.sum(run, axis=1, keepdims=True) // RUN_ALIGN], axis=1)[:, None, :]
    n_active = ends[-1] // TME
    tail = jnp.concatenate([ends - gsz + tot, (gsz - tot) // RUN_ALIGN, n_active[None]])[None, :]
    tile = jnp.arange(n_tiles_max, dtype=jnp.int32)
    te = jnp.sum((jnp.minimum(tile, n_active - 1) * TME)[:, None] >= ends[None, :], axis=1)
    of_tile = te[:, None] == jnp.arange(N_EXPERTS)[None, :]
    used = jnp.sum(jnp.where(of_tile, (ends - gsz + tot)[None, :], 0), axis=1) - tile * TME
    half = (used <= TME // 2) & (tile < n_active)
    ids = jnp.arange(N_EXPERTS, dtype=jnp.int32)
    nonempty = gsz > 0
    later = jnp.where((ids[None, :] > ids[:, None]) & nonempty[None, :], ids[None, :], N_EXPERTS)
    nxt_e = jnp.min(later, axis=1)
    nxt_e = jnp.where(nxt_e == N_EXPERTS, -1, nxt_e)
    grp = jnp.sum(jnp.where(of_tile, (jnp.cumsum(nonempty) - 1)[None, :], 0), axis=1)
    nxt = jnp.sum(jnp.where(of_tile, nxt_e[None, :], 0), axis=1)
    loff_f = loff.astype(f32)
    i32 = lambda a: a.astype(jnp.int32)
    return (tab, tail, loff_f[:, None, :], loff_f[:, :, None], i32(te), i32(half), i32(grp), i32(nxt),
            i32(n_active).reshape(1))


def _local_chunk(c):
    if isinstance(c, int):
        return pl.ds(c * RUN_ALIGN, RUN_ALIGN)
    return pl.ds(pl.multiple_of(c * RUN_ALIGN, RUN_ALIGN), RUN_ALIGN)


def _global_chunk(ctab_ref, h, c):
    return pl.ds(pl.multiple_of(ctab_ref[h, 0, c], RUN_ALIGN), RUN_ALIGN)


def _start_chunks(ctab_ref, h, copy):
    for c in range(MIN_CHUNKS):
        copy(c).start(priority=c % 2)
    lax.fori_loop(MIN_CHUNKS, ctab_ref[h, 0, N_CHUNKS], lambda c, carry: (copy(c).start(), carry)[1], 0)


def _wait_chunks(ctab_ref, h, copy, bulk):
    bulk.wait()
    lax.fori_loop(MIN_CHUNKS, ctab_ref[h, 0, N_CHUNKS], lambda c, carry: (copy(c).wait(), carry)[1], 0)


def _chunk(base, c):
    return pl.ds(pl.multiple_of(base + c * RUN_ALIGN, RUN_ALIGN), RUN_ALIGN)


def _dispatch_kernel(ctab_ref, ctab_prev_ref, tail_ref, f_ref, tit_ref, loffc_ref, xs_hbm,
                     lbuf, zbuf, sem, zsem):
    i = pl.program_id(0)
    slot = i % 2
    sub = lax.broadcasted_iota(jnp.int32, (N_EXPERTS, TM), 0).astype(f32)
    before = (lax.broadcasted_iota(jnp.int32, (TM, TM), 0)
              < lax.broadcasted_iota(jnp.int32, (TM, TM), 1)).astype(bf16)
    rows = lax.broadcasted_iota(jnp.int32, (LOCAL_ROWS, TM), 0).astype(f32)

    def chunk_copy(ctab, h, b):
        return lambda c: pltpu.make_async_copy(
            lbuf.at[b, _local_chunk(c)], xs_hbm.at[_global_chunk(ctab, h, c)], sem.at[b])

    def bulk_copy(b):
        n = MIN_CHUNKS * RUN_ALIGN
        return pltpu.make_async_copy(lbuf.at[b, 0:n], xs_hbm.at[0:n], sem.at[b])

    def wait_tiles(ctab, s):
        for h in range(SUB):
            b = s * SUB + h
            _wait_chunks(ctab, h, chunk_copy(ctab, h, b), bulk_copy(b))

    for h in range(SUB):
        tit = tit_ref[:, h * TM:(h + 1) * TM]
        hit = [tit[k:k + 1, :] == sub for k in range(TOP_K)]
        cnt = sum(x.astype(f32) for x in hit)
        base = loffc_ref[h] + jnp.dot(cnt.astype(bf16), before, preferred_element_type=f32)
        onehot = jnp.zeros((LOCAL_ROWS, TM), f32)
        for k in range(TOP_K):
            lp = jnp.sum(jnp.where(hit[k], base, 0.0), axis=0, keepdims=True)
            onehot = onehot + (rows == lp).astype(f32)
        b = slot * SUB + h
        lbuf[b] = jnp.dot(onehot.astype(bf16), f_ref[h * TM:(h + 1) * TM, :], preferred_element_type=f32)
        _start_chunks(ctab_ref, h, chunk_copy(ctab_ref, h, b))

    @pl.when(i > 0)
    def _():
        wait_tiles(ctab_prev_ref, 1 - slot)

    tail_copy = lambda e, c: pltpu.make_async_copy(zbuf, xs_hbm.at[_chunk(tail_ref[0, e], c)], zsem)

    def for_each_tail(fn):
        for e in range(N_EXPERTS):
            def body(c, carry, e=e):
                fn(e, c)
                return carry
            lax.fori_loop(0, tail_ref[0, N_EXPERTS + e], body, 0)

    @pl.when(i == pl.num_programs(0) - 1)
    def _():
        zbuf[...] = jnp.zeros_like(zbuf)
        for_each_tail(lambda e, c: tail_copy(e, c).start())
        for_each_tail(lambda e, c: tail_copy(e, c).wait())
        wait_tiles(ctab_ref, slot)
        lbuf[0, 0:TME, :] = jnp.zeros((TME, D_MODEL), f32)
        n_active = tail_ref[0, 2 * N_EXPERTS]
        n_spare = xs_hbm.shape[0] // TME - n_active
        spare_copy = lambda j: pltpu.make_async_copy(
            lbuf.at[0, 0:TME], xs_hbm.at[pl.ds(pl.multiple_of((n_active + j) * TME, TME), TME)], zsem)
        lax.fori_loop(0, n_spare, lambda j, carry: (spare_copy(j).start(), carry)[1], 0)
        lax.fori_loop(0, n_spare, lambda j, carry: (spare_copy(j).wait(), carry)[1], 0)


def _dispatch(f, tit, tab, tail, loffc, layer, n_tiles, n_tiles_max):
    smem = functools.partial(pl.BlockSpec, memory_space=pltpu.SMEM)
    return pl.pallas_call(
        _dispatch_kernel,
        grid=(n_tiles // SUB,),
        in_specs=[
            smem((SUB, 1, N_CHUNKS + 1), lambda i: (i, 0, 0)),
            smem((SUB, 1, N_CHUNKS + 1), lambda i: (jnp.maximum(i - 1, 0), 0, 0)),
            smem((1, 2 * N_EXPERTS + 1), lambda i: (0, 0)),
            pl.BlockSpec((SUB * TM, D_MODEL), lambda i: (i, 0)),
            pl.BlockSpec((SLOT_ROWS, SUB * TM), lambda i: (0, i)),
            pl.BlockSpec((SUB, N_EXPERTS, 1), lambda i: (i, 0, 0)),
        ],
        out_specs=pl.BlockSpec(memory_space=pl.ANY),
        out_shape=jax.ShapeDtypeStruct((n_tiles_max * TME, D_MODEL), f32),
        scratch_shapes=[pltpu.VMEM((2 * SUB, LOCAL_ROWS, D_MODEL), f32),
                        pltpu.VMEM((RUN_ALIGN, D_MODEL), f32),
                        pltpu.SemaphoreType.DMA((2 * SUB,)), pltpu.SemaphoreType.DMA(())],
        compiler_params=pltpu.CompilerParams(
            dimension_semantics=("arbitrary",), vmem_limit_bytes=VMEM_LIMIT),
        name=f"dispatch{layer}",
    )(tab, tab, tail, f, tit, loffc)


def _moe_ffn_kernel(te_ref, half_ref, grp_ref, nxt_ref, na_ref, x_ref, w1_hbm, b1_ref, w2_hbm, b2_ref,
                    y_ref, w1f, w2f, w1b, w2b, sem, *, layer):
    i = pl.program_id(0)
    na = na_ref[0]

    def weight_copies(expert, s):
        return (pltpu.make_async_copy(w1_hbm.at[layer, expert], w1f.at[s], sem.at[0, s]),
                pltpu.make_async_copy(w2_hbm.at[layer, expert], w2f.at[s], sem.at[1, s]))

    def ffn(rows):
        u = jnp.dot(x_ref[rows, :].astype(bf16), w1b[...], preferred_element_type=f32) + b1_ref[...]
        glu = jnp.minimum(u[:, :D_FF], SWIGLU_LIMIT)
        lin = jnp.clip(u[:, D_FF:], -SWIGLU_LIMIT, SWIGLU_LIMIT)
        act = glu * jax.nn.sigmoid(SWIGLU_ALPHA * glu) * (lin + 1.0)
        y_ref[rows, :] = jnp.dot(act.astype(bf16), w2b[...], preferred_element_type=f32) + b2_ref[...]

    @pl.when(i == 0)
    def _():
        for cp in weight_copies(te_ref[0], 0):
            cp.start()

    @pl.when(i < na)
    def _():
        @pl.when((i == 0) | (te_ref[i] != te_ref[jnp.maximum(i - 1, 0)]))
        def _():
            s = grp_ref[i] % 2
            for cp in weight_copies(te_ref[i], s):
                cp.wait()
            w1b[...] = w1f[s].astype(bf16)
            w2b[...] = w2f[s].astype(bf16)

            @pl.when(nxt_ref[i] >= 0)
            def _():
                for cp in weight_copies(nxt_ref[i], 1 - s):
                    cp.start()

        @pl.when(half_ref[i] == 0)
        def _():
            ffn(slice(0, TME // 2))
            ffn(slice(TME // 2, TME))

        @pl.when(half_ref[i] != 0)
        def _():
            ffn(slice(0, TME // 2))
            y_ref[TME // 2:, :] = jnp.zeros((TME // 2, D_MODEL), f32)


def _moe_ffn(xs_sorted, te, half, grp, nxt, na, w1, b1, w2, b2, layer, n_tiles_max):
    ex = lambda i, te, *_: (layer, te[i], 0, 0)
    tile = pl.BlockSpec((TME, D_MODEL),
                        lambda i, te, half, grp, nxt, na: (jnp.maximum(jnp.minimum(i, na[0] - 1), 0), 0))
    n_prefetch = 5
    return pl.pallas_call(
        functools.partial(_moe_ffn_kernel, layer=layer),
        grid_spec=pltpu.PrefetchScalarGridSpec(
            num_scalar_prefetch=n_prefetch,
            grid=(n_tiles_max,),
            in_specs=[
                tile,
                pl.BlockSpec(memory_space=pl.ANY),
                pl.BlockSpec((None, None, 1, 2 * D_FF), ex),
                pl.BlockSpec(memory_space=pl.ANY),
                pl.BlockSpec((None, None, 1, D_MODEL), ex),
            ],
            out_specs=tile,
            scratch_shapes=[pltpu.VMEM((2, D_MODEL, 2 * D_FF), f32), pltpu.VMEM((2, D_FF, D_MODEL), f32),
                            pltpu.VMEM((D_MODEL, 2 * D_FF), bf16), pltpu.VMEM((D_FF, D_MODEL), bf16),
                            pltpu.SemaphoreType.DMA((2, 2))],
        ),
        out_shape=jax.ShapeDtypeStruct((n_tiles_max * TME, D_MODEL), f32),
        input_output_aliases={n_prefetch: 0},
        compiler_params=pltpu.CompilerParams(
            dimension_semantics=("arbitrary",), vmem_limit_bytes=VMEM_LIMIT),
        name=f"moe_ffn{layer}",
    )(te, half, grp, nxt, na, xs_sorted, w1, b1.reshape(DEPTH, N_EXPERTS, 1, 2 * D_FF), w2,
      b2.reshape(DEPTH, N_EXPERTS, 1, D_MODEL))


def _combine_kernel(ctab_ref, ctab_next_ref, y_hbm, ti_ref, tw_ref, loffr_ref, x_ref, mod_ref, g_ref,
                    xn_ref, ybuf, sem):
    i = pl.program_id(0)
    slot = i % 2

    def chunk_copy(ctab, h, b):
        return lambda c: pltpu.make_async_copy(
            y_hbm.at[_global_chunk(ctab, h, c)], ybuf.at[b, _local_chunk(c)], sem.at[b])

    def bulk_copy(b):
        n = MIN_CHUNKS * RUN_ALIGN
        return pltpu.make_async_copy(y_hbm.at[0:n], ybuf.at[b, 0:n], sem.at[b])

    def start_tiles(ctab, s):
        for h in range(SUB):
            _start_chunks(ctab, h, chunk_copy(ctab, h, s * SUB + h))

    @pl.when(i == 0)
    def _():
        ybuf[...] = jnp.zeros_like(ybuf)
        start_tiles(ctab_ref, 0)

    @pl.when(i + 1 < pl.num_programs(0))
    def _():
        start_tiles(ctab_next_ref, 1 - slot)

    lane = lax.broadcasted_iota(jnp.int32, (TM, N_EXPERTS), 1).astype(f32)
    before = (lax.broadcasted_iota(jnp.int32, (TM, TM), 1)
              < lax.broadcasted_iota(jnp.int32, (TM, TM), 0)).astype(bf16)
    cols = lax.broadcasted_iota(jnp.int32, (TM, LOCAL_ROWS), 1).astype(f32)
    for h in range(SUB):
        tok = slice(h * TM, (h + 1) * TM)
        b = slot * SUB + h
        ti = ti_ref[tok, :]
        tw = tw_ref[tok, :]
        hit = [ti[:, k:k + 1] == lane for k in range(TOP_K)]
        cnt = sum(x.astype(f32) for x in hit)
        base = loffr_ref[h] + jnp.dot(before, cnt.astype(bf16), preferred_element_type=f32)
        w = jnp.zeros((TM, LOCAL_ROWS), f32)
        for k in range(TOP_K):
            lp = jnp.sum(jnp.where(hit[k], base, 0.0), axis=1, keepdims=True)
            w = jnp.where(cols == lp, tw[:, k:k + 1], w)
        _wait_chunks(ctab_ref, h, chunk_copy(ctab_ref, h, b), bulk_copy(b))
        acc = jnp.dot(w.astype(bf16), ybuf[b].astype(bf16), preferred_element_type=f32)
        xn_ref[tok, :] = x_ref[tok, :] + mod_ref[5:6, :] * _rms(acc, g_ref[...])


def _combine(y, tab, ti, tw, loffr, xs, mods, layer, g, n_tiles):
    rows = SUB * TM
    n_steps = n_tiles // SUB
    seg = functools.partial(_seg_of_tile, tiles_per_seq=SEQ // rows, lat_tiles=T_LAT // rows)
    row = lambda n: pl.BlockSpec((rows, n), lambda i: (i, 0))
    smem = functools.partial(pl.BlockSpec, memory_space=pltpu.SMEM)
    return pl.pallas_call(
        _combine_kernel,
        grid=(n_steps,),
        in_specs=[
            smem((SUB, 1, N_CHUNKS + 1), lambda i: (i, 0, 0)),
            smem((SUB, 1, N_CHUNKS + 1), lambda i: (jnp.minimum(i + 1, n_steps - 1), 0, 0)),
            pl.BlockSpec(memory_space=pl.ANY),
            row(SLOT_ROWS), row(SLOT_ROWS),
            pl.BlockSpec((SUB, 1, N_EXPERTS), lambda i: (i, 0, 0)),
            row(D_MODEL),
            pl.BlockSpec((None, None, 6, D_MODEL), lambda i: (layer, seg(i), 0, 0)),
            pl.BlockSpec((1, D_MODEL), lambda i: (0, 0)),
        ],
        out_specs=row(D_MODEL),
        out_shape=jax.ShapeDtypeStruct((n_tiles * TM, D_MODEL), f32),
        scratch_shapes=[pltpu.VMEM((2 * SUB, LOCAL_ROWS, D_MODEL), f32),
                        pltpu.SemaphoreType.DMA((2 * SUB,))],
        compiler_params=pltpu.CompilerParams(
            dimension_semantics=("arbitrary",), vmem_limit_bytes=VMEM_LIMIT),
        name=f"combine{layer}",
    )(tab, tab, y, ti, tw, loffr, xs, mods, g)


def _moe(f, tit, ti, tw, cnt, xs, mods, layer, g, w1, b1, w2, b2, n_tiles):
    n_tiles_max = _ffn_tiles_max(n_tiles)
    tab, tail, loffr, loffc, te, half, grp, nxt, na = _route_tables(cnt, n_tiles_max)
    xs_sorted = _dispatch(f, tit, tab, tail, loffc, layer, n_tiles, n_tiles_max)
    y = _moe_ffn(xs_sorted, te, half, grp, nxt, na, w1, b1, w2, b2, layer, n_tiles_max)
    return _combine(y, tab, ti, tw, loffr, xs, mods, layer, g, n_tiles)


def kernel(x, c, ctx, c_ctx, w_ada, b_ada, norm_g, a_w_qkv, a_w_o, a_sink, b_w_qkv, b_q_norm, b_k_norm,
           b_w_o, moe_w_router, moe_b_router, moe_w1, moe_b1, moe_w2, moe_b2):
    assert DEPTH == 2 and x.shape == (BATCH, SEQ, D_MODEL) and ctx.shape == (BATCH, CTX_LEN, D_MODEL)
    x_lat, x_ctx = x.reshape(T_LAT, D_MODEL), ctx.reshape(T_CTX, D_MODEL)
    c_all = jnp.concatenate(
        [c, c_ctx[None, :], jnp.zeros((MOD_ROWS - BATCH - 1, D_MODEL), f32)], axis=0)
    mods = _adaln(c_all, w_ada, b_ada).reshape(DEPTH, MOD_ROWS, 6, D_MODEL)
    ones = jnp.ones((1, LANES), f32)
    g = lambda i, j: norm_g[i, j][None, :]

    q, k, v = _qkv(x_lat, x_ctx, mods, 0, g(0, 0), a_w_qkv[0].astype(bf16), _rope_tables(HD_A), ones, ones,
                   hq=HQ_A, hkv=HKV_A, hd=HD_A, qk_norm=False, v_ones=False, tm=TMP, chain=TM)
    o = _attn_a(q, k, v, a_sink[0])
    xs, *routed = _post_attn(o, a_w_o[0].astype(bf16), x_lat, x_ctx, mods, 0, g(0, 1), g(0, 2),
                             moe_w_router[0], moe_b_router[0], ALL_TILES)
    xs = _moe(*routed, xs, mods, 0, g(0, 3), moe_w1, moe_b1, moe_w2, moe_b2, ALL_TILES)

    q, k, v = _qkv(xs, xs, mods, 1, g(1, 0), b_w_qkv[0].astype(bf16), _rope_tables(HD_B),
                   b_q_norm[0][None, :], b_k_norm[0][None, :],
                   hq=HQ_B, hkv=HKV_B, hd=HD_B, qk_norm=True, v_ones=True, tm=TM, chain=TM)
    o = _attn_b(q, k, v)
    xl, *routed = _post_attn(o, b_w_o[0].astype(bf16), xs, xs, mods, 1, g(1, 1), g(1, 2),
                             moe_w_router[1], moe_b_router[1], LAT_TILES)
    xl = _moe(*routed, xl, mods, 1, g(1, 3), moe_w1, moe_b1, moe_w2, moe_b2, LAT_TILES)
    return xl.reshape(BATCH, SEQ, D_MODEL)
```
